```python
import jax, jax.numpy as jnp
from jax import lax
import numpy as np

D_MODEL = 2048
BATCH = 4
SEQ = 2048
DEPTH = 2
DEC_BATCH = 128
DEC_SEQ = 8
PAST_LEN = 16384
PAGE_SIZE = 128

N_HEADS = 4
D_QK = D_MODEL // 2
D_V = D_MODEL
DK_HEAD = D_QK // N_HEADS
DV_HEAD = D_V // N_HEADS
GATE_RANK = 16
GATE_NORMALIZER = 16
CHUNK = 16
D_POOL = D_MODEL // 2
POOL_GROUPS = 4
POOL_WINDOWS = (2, 4, 8, 16)
POOL_BUF = 15
POOL_IN_G = D_POOL // POOL_GROUPS
POOL_OUT_G = D_MODEL // POOL_GROUPS
N_IN = 2 * D_QK + 2 * D_V + D_POOL + 2 * D_MODEL + GATE_RANK
D_FF = ((8 * D_MODEL + 3 * 256 - 1) // (3 * 256)) * 256
EPS = 1e-6

kernel_name = "gla_pool_hybrid_adaln_decode_step"


def _rmsnorm(x, g):
    xf = x.astype(jnp.float32)
    y = xf * lax.rsqrt(jnp.mean(xf * xf, axis=-1, keepdims=True) + EPS)
    return (y * g.astype(jnp.float32)).astype(x.dtype)


def _gla(q, k, v, loga, s0):
    B, L, H, _ = q.shape
    n = -(-L // CHUNK)
    pad = n * CHUNK - L

    def prep(a):
        a = jnp.pad(a.astype(jnp.float32), ((0, 0), (0, pad), (0, 0), (0, 0)))
        return a.reshape(B, n, CHUNK, H, a.shape[-1]).transpose(1, 0, 3, 2, 4)

    q, k, v, loga = prep(q), prep(k), prep(v), prep(loga)
    b = jnp.cumsum(loga, axis=3)
    b_last = b[:, :, :, -1:, :]
    qe = q * jnp.exp(b)
    ke = k * jnp.exp(-b)
    kd = k * jnp.exp(b_last - b)
    decay = jnp.exp(b_last[:, :, :, 0, :])
    mask = jnp.tril(jnp.ones((CHUNK, CHUNK), dtype=bool))
    att = jnp.where(mask, jnp.einsum('nbhcd,nbhsd->nbhcs', qe, ke), 0.0)
    o_intra = jnp.einsum('nbhcs,nbhse->nbhce', att, v)

    def step(s, inp):
        qe_n, kd_n, v_n, dec_n = inp
        o = jnp.einsum('bhcd,bhde->bhce', qe_n, s)
        s = dec_n[..., None] * s + jnp.einsum('bhcd,bhce->bhde', kd_n, v_n)
        return s, o

    s_fin, o_inter = lax.scan(step, s0.astype(jnp.float32), (qe, kd, v, decay))
    o = (o_intra + o_inter).transpose(1, 0, 3, 2, 4).reshape(B, n * CHUNK, H, DV_HEAD)[:, :L]
    return o, s_fin


def _pool(u_ext, pos0, w_pool, pool_scale):
    B, T, _ = u_ext.shape
    L = T - POOL_BUF
    uf = u_ext.astype(jnp.float32).reshape(B, T, POOL_GROUPS, POOL_IN_G)
    cs = jnp.concatenate([jnp.zeros((B, 1, POOL_GROUPS, POOL_IN_G), jnp.float32),
                          jnp.cumsum(uf, axis=1)], axis=1)
    end = cs[:, POOL_BUF + 1:]
    u_new = uf[:, POOL_BUF:]
    pos = pos0 + jnp.arange(L)
    outs = []
    for g, w in enumerate(POOL_WINDOWS):
        start = cs[:, POOL_BUF + 1 - w: POOL_BUF + 1 - w + L, g]
        cnt = jnp.minimum(pos + 1, w).astype(jnp.float32)[None, :, None]
        outs.append((end[:, :, g] - start) / cnt - u_new[:, :, g])
    d = jnp.stack(outs, axis=2).astype(u_ext.dtype)
    y = jnp.einsum('blgi,gio->blgo', d, w_pool).reshape(B, L, D_MODEL)
    return y * pool_scale


def _layer(x, c, s0, buf0, pos0, w_ada, b_ada, norm1_g, w_in, w_a2, b_a, gla_norm_g,
           w_pool, pool_scale, w_o, norm2_g, w_gu, w_down):
    B, L, _ = x.shape
    mod = jax.nn.silu(c) @ w_ada + b_ada
    sh1, sc1, g1, sh2, sc2, g2 = [m[:, None, :] for m in jnp.split(mod, 6, axis=-1)]

    h = _rmsnorm(x, norm1_g) * (1 + sc1) + sh1
    p = h @ w_in
    sizes = [D_QK, D_QK, D_V, D_V, D_POOL, D_MODEL, D_MODEL]
    idx = [int(i) for i in np.cumsum(sizes)]
    q, k, v, og, u, ga, gb, alr = jnp.split(p, idx, axis=-1)
    gk = alr @ w_a2 + b_a
    loga = jax.nn.log_sigmoid(gk.astype(jnp.float32)) / GATE_NORMALIZER
    q = q.reshape(B, L, N_HEADS, DK_HEAD) * (DK_HEAD ** -0.5)
    k = k.reshape(B, L, N_HEADS, DK_HEAD)
    v = v.reshape(B, L, N_HEADS, DV_HEAD)
    o, s_new = _gla(q, k, v, loga.reshape(B, L, N_HEADS, DK_HEAD), s0)
    o = _rmsnorm(o, gla_norm_g).reshape(B, L, D_V).astype(x.dtype)
    y_a = o * jax.nn.silu(og)

    u_ext = jnp.concatenate([buf0.astype(u.dtype), u], axis=1)
    y_b = _pool(u_ext, pos0, w_pool, pool_scale)
    buf_new = u_ext[:, -POOL_BUF:]

    merged = jax.nn.sigmoid(ga) * y_a + jax.nn.sigmoid(gb) * y_b
    x = x + g1 * (merged @ w_o)

    h2 = _rmsnorm(x, norm2_g) * (1 + sc2) + sh2
    gate, up = jnp.split(h2 @ w_gu, 2, axis=-1)
    x = x + g2 * ((jax.nn.silu(gate) * up) @ w_down)
    return x, s_new, buf_new


def setup_inputs(seed: int = 0) -> dict:
    key = jax.random.key(seed)
    ks = jax.random.split(key, 24)
    f32 = jnp.float32
    nrm = lambda k, s, sc: jax.random.normal(k, s, f32) * sc
    return {
        "x_prompt": nrm(ks[0], (BATCH, SEQ, D_MODEL), 1.0),
        "x_sample": nrm(ks[1], (DEC_BATCH, DEC_SEQ, D_MODEL), 1.0),
        "state_gla": nrm(ks[2], (DEPTH, DEC_BATCH, N_HEADS, DK_HEAD, DV_HEAD), 1.0),
        "state_pool": nrm(ks[3], (DEPTH, DEC_BATCH, POOL_BUF, D_POOL), 1.0),
        "c_prompt": nrm(ks[4], (BATCH, D_MODEL), 1.0),
        "c_sample": nrm(ks[5], (DEC_BATCH, D_MODEL), 1.0),
        "w_ada": nrm(ks[6], (DEPTH, D_MODEL, 6 * D_MODEL), 0.5 * D_MODEL ** -0.5),
        "b_ada": nrm(ks[7], (DEPTH, 6 * D_MODEL), 0.02),
        "norm1_g": 1.0 + nrm(ks[8], (DEPTH, D_MODEL), 0.1),
        "w_in": nrm(ks[9], (DEPTH, D_MODEL, N_IN), D_MODEL ** -0.5),
        "w_a2": nrm(ks[10], (DEPTH, GATE_RANK, D_QK), GATE_RANK ** -0.5),
        "b_a": nrm(ks[11], (DEPTH, D_QK), 0.1),
        "gla_norm_g": 1.0 + nrm(ks[12], (DEPTH, DV_HEAD), 0.1),
        "w_pool": nrm(ks[13], (DEPTH, POOL_GROUPS, POOL_IN_G, POOL_OUT_G), POOL_IN_G ** -0.5),
        "pool_scale": 1.0 + nrm(ks[14], (DEPTH, D_MODEL), 0.1),
        "w_o": nrm(ks[15], (DEPTH, D_MODEL, D_MODEL), D_MODEL ** -0.5),
        "norm2_g": 1.0 + nrm(ks[16], (DEPTH, D_MODEL), 0.1),
        "w_gu": nrm(ks[17], (DEPTH, D_MODEL, 2 * D_FF), D_MODEL ** -0.5),
        "w_down": nrm(ks[18], (DEPTH, D_FF, D_MODEL), D_FF ** -0.5),
        "final_norm_g": 1.0 + nrm(ks[19], (D_MODEL,), 0.1),
    }


def reference(x_prompt, x_sample, state_gla, state_pool, c_prompt, c_sample,
              w_ada, b_ada, norm1_g, w_in, w_a2, b_a, gla_norm_g, w_pool, pool_scale,
              w_o, norm2_g, w_gu, w_down, final_norm_g):
    xp, xs = x_prompt, x_sample
    gla_p, pool_p, gla_s, pool_s = [], [], [], []
    for l in range(DEPTH):
        lw = (w_ada[l], b_ada[l], norm1_g[l], w_in[l], w_a2[l], b_a[l], gla_norm_g[l],
              w_pool[l], pool_scale[l], w_o[l], norm2_g[l], w_gu[l], w_down[l])
        s0_p = jnp.zeros((BATCH, N_HEADS, DK_HEAD, DV_HEAD), jnp.float32)
        buf0_p = jnp.zeros((BATCH, POOL_BUF, D_POOL), xp.dtype)
        xp, s_p, b_p = _layer(xp, c_prompt, s0_p, buf0_p, 0, *lw)
        xs, s_s, b_s = _layer(xs, c_sample, state_gla[l], state_pool[l], PAST_LEN, *lw)
        gla_p.append(s_p)
        pool_p.append(b_p)
        gla_s.append(s_s)
        pool_s.append(b_s)
    y_prompt = _rmsnorm(xp, final_norm_g)
    y_sample = _rmsnorm(xs, final_norm_g)
    return (y_prompt, y_sample, jnp.stack(gla_p), jnp.stack(pool_p), jnp.stack(gla_s), jnp.stack(pool_s))
```

```python
import functools

import jax
import jax.numpy as jnp
from jax import lax
from jax.experimental import pallas as pl
from jax.experimental.pallas import tpu as pltpu

F32 = jnp.float32
BF16 = jnp.bfloat16

N_HEADS = 4
POOL_WINDOWS = (2, 4, 8, 16)
POOL_BUF = 15
CARRY_ROWS = POOL_BUF + 1
GATE_NORMALIZER = 16.0
PAST_LEN = 16384
EPS = 1e-6
STABLE_BLOCK = 16
LANE = 128
VMEM_LIMIT = 56 * 1024 * 1024


def _cparams(*sem):
    return pltpu.CompilerParams(dimension_semantics=sem, vmem_limit_bytes=VMEM_LIMIT)


def _largest_tile(n, cap):
    best = None
    for t in range(LANE, cap + 1, LANE):
        if n % t == 0:
            best = t
    assert best is not None, (n, cap)
    return best


def _dot(a, b):
    return jnp.dot(a, b, preferred_element_type=F32)


def _dot_nt(a, b):
    return lax.dot_general(a, b, (((1,), (1,)), ((), ())), preferred_element_type=F32)


def _dot_tn(a, b):
    return lax.dot_general(a, b, (((0,), (0,)), ((), ())), preferred_element_type=F32)


def _split3(x):
    hi = x.astype(BF16)
    r1 = x - hi.astype(F32)
    mid = r1.astype(BF16)
    lo = (r1 - mid.astype(F32)).astype(BF16)
    return hi, mid, lo


def _sigmoid(x):
    return 1.0 / (1.0 + jnp.exp(-x))


def _silu(x):
    return x * _sigmoid(x)


def _log_sigmoid(x):
    return jnp.minimum(x, 0.0) - jnp.log1p(jnp.exp(-jnp.abs(x)))


def _rms(x):
    return x * lax.rsqrt(jnp.mean(x * x, axis=-1, keepdims=True) + EPS)


def _rows_bcast(x, row, n):
    return jnp.broadcast_to(x[row:row + 1, :], (n, x.shape[1]))


def _ada_kernel(c_ref, w_ref, b_ref, o_ref):
    a = _silu(c_ref[...]).astype(BF16)
    o_ref[...] = _dot(a, w_ref[...].astype(BF16)) + b_ref[...]


def _ada_call(c_all, w_ada, b_ada):
    depth, d, n6 = w_ada.shape
    bp = c_all.shape[0]
    tn = 1024
    return pl.pallas_call(
        _ada_kernel,
        grid=(depth, n6 // tn),
        in_specs=[
            pl.BlockSpec((bp, d), lambda l, j: (0, 0)),
            pl.BlockSpec((None, d, tn), lambda l, j: (l, 0, j)),
            pl.BlockSpec((None, 1, tn), lambda l, j: (l, 0, j)),
        ],
        out_specs=pl.BlockSpec((None, bp, tn), lambda l, j: (l, 0, j)),
        out_shape=jax.ShapeDtypeStruct((depth, bp, n6), F32),
        compiler_params=_cparams("arbitrary", "arbitrary"),
        name="ada_mod",
    )(c_all, w_ada, b_ada.reshape(depth, 1, n6))


def _mod_spec(bb, d, layer, j):
    return pl.BlockSpec((None, bb, 1, d), lambda b, t: (layer, b, 0, j))


def _pre_kernel(x_ref, g_ref, sc_ref, sh_ref, *rest):
    o_ref = rest[-1]
    x = x_ref[...]
    bb, tt, d = x.shape
    h = _rms(x) * g_ref[...] * (1.0 + sc_ref[...]) + sh_ref[...]
    o_ref[...] = h.reshape(bb * tt, d).astype(BF16)


def _row_tiles(b, t, target):
    if t >= target:
        return 1, target
    return target // t, t


def _pre_call(x3, mod4, layer, norm_g, m_total, row_off, prev, target=512):
    b, t, d = x3.shape
    bb, tt = _row_tiles(b, t, target)
    nt = t // tt
    off = row_off // (bb * tt)
    in_specs = [
        pl.BlockSpec((bb, tt, d), lambda i, j: (i, j, 0)),
        pl.BlockSpec((1, d), lambda i, j: (0, 0)),
        _mod_spec(bb, d, layer, 1),
        _mod_spec(bb, d, layer, 0),
    ]
    args = [x3, norm_g.reshape(1, d), mod4, mod4]
    aliases = {}
    if prev is not None:
        in_specs.append(pl.BlockSpec(memory_space=pl.ANY))
        args.append(prev)
        aliases = {4: 0}
    return pl.pallas_call(
        _pre_kernel,
        grid=(b // bb, nt),
        in_specs=in_specs,
        out_specs=pl.BlockSpec((bb * tt, d), lambda i, j: (off + i * nt + j, 0)),
        out_shape=jax.ShapeDtypeStruct((m_total, d), BF16),
        input_output_aliases=aliases,
        compiler_params=_cparams("arbitrary", "arbitrary"),
        name="prologue",
    )(*args)


def _mm_in_kernel(a_ref, w_ref, wl_ref, p_ref, alr_ref):
    a = a_ref[...]
    p_ref[...] = _dot(a, w_ref[...])

    @pl.when(pl.program_id(1) == 0)
    def _():
        alr_ref[...] = _dot(a, wl_ref[...])


def _mm_in_call(h, w, w_alr, tm=1024):
    m, k = h.shape
    n = w.shape[1]
    tn = _largest_tile(n, 1024)
    return pl.pallas_call(
        _mm_in_kernel,
        grid=(m // tm, n // tn),
        in_specs=[
            pl.BlockSpec((tm, k), lambda i, j: (i, 0)),
            pl.BlockSpec((k, tn), lambda i, j: (0, j)),
            pl.BlockSpec((k, LANE), lambda i, j: (0, 0)),
        ],
        out_specs=[
            pl.BlockSpec((tm, tn), lambda i, j: (i, j)),
            pl.BlockSpec((tm, LANE), lambda i, j: (i, 0)),
        ],
        out_shape=[jax.ShapeDtypeStruct((m, n), F32), jax.ShapeDtypeStruct((m, LANE), F32)],
        compiler_params=_cparams("arbitrary", "arbitrary"),
        name="in_proj",
    )(h, w, w_alr)


def _iota2(shape, dim):
    return lax.broadcasted_iota(jnp.int32, shape, dim)


def _blk(x, n):
    assert n & (n - 1) == 0
    return lax.shift_right_logical(x, jnp.int32(n.bit_length() - 1))


def _rem(x, n):
    assert n & (n - 1) == 0
    return x & jnp.int32(n - 1)


def _gla_factors(q, k, loga, seq):
    t, dk = q.shape
    row = _iota2((t, t), 0)
    col = _iota2((t, t), 1)
    tri = ((col <= row) & (_blk(row, seq) == _blk(col, seq))).astype(BF16)
    hi, mid, lo = _split3(loga)
    b = _dot(tri, hi) + _dot(tri, mid) + _dot(tri, lo)

    nb = min(STABLE_BLOCK, seq)
    parts = []
    for j in range(t // nb):
        start = j * nb
        if start % seq == 0:
            parts.append(jnp.zeros((nb, dk), F32))
        else:
            parts.append(_rows_bcast(b, start - 1, nb))
    e0 = b - (parts[0] if len(parts) == 1 else jnp.concatenate(parts, axis=0))
    qe0 = (q * jnp.exp(e0)).astype(BF16)
    ke0 = (k * jnp.exp(-e0)).astype(BF16)
    att = jnp.where((_blk(row, nb) == _blk(col, nb)) & (col <= row), _dot_nt(qe0, ke0), 0.0)

    rowv = _iota2((t, 1), 0)
    n = nb * 2
    while n <= seq:
        half = n // 2
        mids = [_rows_bcast(b, j * n + half - 1, n) for j in range(t // n)]
        x = b - (mids[0] if len(mids) == 1 else jnp.concatenate(mids, axis=0))
        second = _rem(rowv, n) >= half
        f = jnp.exp(jnp.where(second, x, -x))
        qk = jnp.where(second, q, k) * f
        ql = jnp.where(second, qk, 0.0).astype(BF16)
        kl = jnp.where(second, 0.0, qk).astype(BF16)
        att = att + jnp.where(_blk(row, n) == _blk(col, n), _dot_nt(ql, kl), 0.0)
        n *= 2

    lasts = [_rows_bcast(b, (j + 1) * seq - 1, seq) for j in range(t // seq)]
    b_last = lasts[0] if len(lasts) == 1 else jnp.concatenate(lasts, axis=0)
    qe = q * jnp.exp(b)
    kd = k * jnp.exp(b_last - b)
    return att, qe, kd, (hi, mid, lo)


def _decay_cols(loga3, sel):
    hi, mid, lo = loga3
    return jnp.exp(_dot_tn(hi, sel) + _dot_tn(mid, sel) + _dot_tn(lo, sel))


def _lane_tile(x, width):
    reps = width // x.shape[1]
    return x if reps == 1 else jnp.concatenate([x] * reps, axis=1)


def _pool_delta(u, carry, w, pos0, seq):
    t = u.shape[0]
    npiece = t // seq
    row = _iota2((t, t), 0)
    col = _iota2((t, t), 1)
    band = ((_blk(row, seq) == _blk(col, seq)) & (col <= row) & (col > row - w)).astype(BF16)
    rc = _iota2((t, npiece * CARRY_ROWS), 0)
    cc = _iota2((t, npiece * CARRY_ROWS), 1)
    cband = ((_blk(rc, seq) == _blk(cc, CARRY_ROWS))
             & (_rem(cc, CARRY_ROWS) > _rem(rc, seq) + CARRY_ROWS - w)).astype(BF16)
    s = jnp.zeros(u.shape, F32)
    for part in _split3(u):
        s = s + _dot(band, part)
    for part in _split3(carry):
        s = s + _dot(cband, part)
    cnt = jnp.minimum(pos0 + 1, w).astype(F32)
    return s / cnt - u


def _mix_epilogue(o, og, ga, gb, d, gn, wp, ps):
    y_a = _rms(o) * gn * _silu(og)
    y_b = _dot(d.astype(BF16), wp) * ps
    return _sigmoid(ga) * y_a + _sigmoid(gb) * y_b


def _gate_log_decay(alr, wa2, ba):
    gk = _dot(alr.astype(BF16), wa2.astype(BF16)) + ba
    return _log_sigmoid(gk) * (1.0 / GATE_NORMALIZER)


def _mix_prompt_kernel(q_ref, k_ref, v_ref, og_ref, u_ref, ga_ref, gb_ref, alr_ref,
                       wa2_ref, ba_ref, gn_ref, wp_ref, ps_ref,
                       m_ref, s_out_ref, pool_out_ref, s_ref, car_ref):
    h = pl.program_id(1)
    c = pl.program_id(2)
    nc = pl.num_programs(2)
    t, dk = q_ref.shape
    dv = v_ref.shape[1]

    @pl.when(c == 0)
    def _():
        s_ref[...] = jnp.zeros_like(s_ref)
        car_ref[...] = jnp.zeros_like(car_ref)

    q = q_ref[...] * (dk ** -0.5)
    k = k_ref[...]
    v = v_ref[...]
    loga = _gate_log_decay(alr_ref[...], wa2_ref[...], ba_ref[...])
    att, qe, kd, loga3 = _gla_factors(q, k, loga, t)
    s = s_ref[...]
    vb = v.astype(BF16)
    o = _dot(att.astype(BF16), vb) + _dot(qe.astype(BF16), s.astype(BF16))
    decay = _decay_cols(loga3, jnp.ones((t, LANE), BF16))
    s_new = _lane_tile(decay, dv) * s + _dot_tn(kd.astype(BF16), vb)
    s_ref[...] = s_new

    u = u_ref[...]
    w = jnp.left_shift(2, h)
    pos = c * t + _iota2((t, 1), 0)
    d = _pool_delta(u, car_ref[...], w, pos, t)
    car_ref[...] = u[t - CARRY_ROWS:, :]

    merged = _mix_epilogue(o, og_ref[...], ga_ref[...], gb_ref[...], d,
                           gn_ref[...], wp_ref[...].astype(BF16), ps_ref[...])
    m_ref[...] = merged.astype(BF16)

    @pl.when(c == nc - 1)
    def _():
        s_out_ref[...] = s_new
        pool_out_ref[...] = u[t - POOL_BUF:, :]


def _mix_prompt_call(p, alr, batch, seq_len, w_a2, b_a, gla_norm_g, w_pool, pool_scale, chunk=128):
    dqk = w_a2.shape[1]
    dk = dqk // N_HEADS
    groups, pin, pout = w_pool.shape
    dv = pout
    d_model = pool_scale.shape[0]
    nc = seq_len // chunk
    m_rows = batch * seq_len
    k_off = dqk // dk
    v_off = 2 * dqk // dv
    og_off = v_off + N_HEADS
    u_off = (2 * dqk + 2 * d_model) // pin
    ga_off = (2 * dqk + 2 * d_model + groups * pin) // dv
    gb_off = ga_off + N_HEADS

    def rows(b, h, c):
        return b * nc + c

    def colspec(width, off):
        return pl.BlockSpec((chunk, width), lambda b, h, c: (rows(b, h, c), off + h))

    in_specs = [
        colspec(dk, 0), colspec(dk, k_off), colspec(dv, v_off), colspec(dv, og_off),
        colspec(pin, u_off), colspec(dv, ga_off), colspec(dv, gb_off),
        pl.BlockSpec((chunk, LANE), lambda b, h, c: (rows(b, h, c), 0)),
        pl.BlockSpec((LANE, dk), lambda b, h, c: (0, h)),
        pl.BlockSpec((1, dk), lambda b, h, c: (0, h)),
        pl.BlockSpec((1, dv), lambda b, h, c: (0, 0)),
        pl.BlockSpec((None, pin, pout), lambda b, h, c: (h, 0, 0)),
        pl.BlockSpec((1, dv), lambda b, h, c: (0, h)),
    ]
    out_specs = [
        pl.BlockSpec((chunk, dv), lambda b, h, c: (rows(b, h, c), h)),
        pl.BlockSpec((None, None, dk, dv), lambda b, h, c: (b, h, 0, 0)),
        pl.BlockSpec((None, POOL_BUF, pin), lambda b, h, c: (b, 0, h)),
    ]
    out_shape = [
        jax.ShapeDtypeStruct((m_rows, d_model), BF16),
        jax.ShapeDtypeStruct((batch, N_HEADS, dk, dv), F32),
        jax.ShapeDtypeStruct((batch, POOL_BUF, groups * pin), F32),
    ]
    return pl.pallas_call(
        _mix_prompt_kernel,
        grid=(batch, N_HEADS, nc),
        in_specs=in_specs,
        out_specs=out_specs,
        out_shape=out_shape,
        scratch_shapes=[pltpu.VMEM((dk, dv), F32), pltpu.VMEM((CARRY_ROWS, pin), F32)],
        compiler_params=_cparams("arbitrary", "arbitrary", "arbitrary"),
        name="mix_prompt",
    )(p, p, p, p, p, p, p, alr, w_a2, b_a.reshape(1, dqk), gla_norm_g.reshape(1, dv), w_pool,
      pool_scale.reshape(1, d_model))


def _mix_sample_kernel(p_ref, alr_ref, s0_ref, buf_ref, wa2_ref, ba_ref, gn_ref, wp_ref, ps_ref, *rest,
                       seq, dqk, d_model, n_alias):
    m_ref, s_out_ref, pool_out_ref, car_ref = rest[n_alias:]
    nseq = s0_ref.shape[0]
    t = nseq * seq
    dk = dqk // N_HEADS
    dv = s0_ref.shape[-1]
    pin = wp_ref.shape[1]
    groups = wp_ref.shape[0]
    v0 = 2 * dqk
    og0 = v0 + d_model
    u0 = og0 + d_model
    ga0 = u0 + groups * pin
    gb0 = ga0 + d_model

    car_ref[...] = jnp.zeros_like(car_ref)
    for j in range(nseq):
        car_ref[j * CARRY_ROWS + 1:(j + 1) * CARRY_ROWS, :] = buf_ref[j]

    loga_all = _gate_log_decay(alr_ref[...], wa2_ref[...], ba_ref[...])
    rowv = _iota2((t, 1), 0)
    pos = PAST_LEN + _rem(rowv, seq)
    keep = POOL_BUF - seq
    gn = gn_ref[...]

    for h in range(N_HEADS):
        q = p_ref[:, h * dk:(h + 1) * dk] * (dk ** -0.5)
        k = p_ref[:, dqk + h * dk:dqk + (h + 1) * dk]
        v = p_ref[:, v0 + h * dv:v0 + (h + 1) * dv]
        loga = loga_all[:, h * dk:(h + 1) * dk]
        att, qe, kd, loga3 = _gla_factors(q, k, loga, seq)
        vb = v.astype(BF16)
        kdb = kd.astype(BF16)
        qeb = qe.astype(BF16)
        o_intra = _dot(att.astype(BF16), vb)
        o_parts = []
        for j in range(nseq):
            s0 = s0_ref[j, h]
            lo, hi = j * seq, (j + 1) * seq
            o_parts.append(o_intra[lo:hi] + _dot(qeb[lo:hi], s0.astype(BF16)))
            sel = (_blk(_iota2((t, LANE), 0), seq) == j).astype(BF16)
            decay = _decay_cols(loga3, sel)
            s_out_ref[j, h] = _lane_tile(decay, dv) * s0 + _dot_tn(kdb[lo:hi], vb[lo:hi])
        o = jnp.concatenate(o_parts, axis=0)

        u = p_ref[:, u0 + h * pin:u0 + (h + 1) * pin]
        d = _pool_delta(u, car_ref[:, h * pin:(h + 1) * pin], POOL_WINDOWS[h], pos, seq)
        for j in range(nseq):
            pool_out_ref[j, 0:keep, h * pin:(h + 1) * pin] = buf_ref[j, seq:POOL_BUF, h * pin:(h + 1) * pin]
            pool_out_ref[j, keep:POOL_BUF, h * pin:(h + 1) * pin] = u[j * seq:(j + 1) * seq]

        merged = _mix_epilogue(o, p_ref[:, og0 + h * dv:og0 + (h + 1) * dv],
                               p_ref[:, ga0 + h * dv:ga0 + (h + 1) * dv],
                               p_ref[:, gb0 + h * dv:gb0 + (h + 1) * dv], d,
                               gn, wp_ref[h].astype(BF16), ps_ref[:, h * dv:(h + 1) * dv])
        m_ref[:, h * dv:(h + 1) * dv] = merged.astype(BF16)


def _mix_sample_call(p, alr, row_off, seq, layer, state_gla, state_pool, w_a2, b_a, gla_norm_g, w_pool,
                     pool_scale, prev_gla, prev_pool, nseq=2):
    depth, bs, nh, dk, dv = state_gla.shape
    dqk = w_a2.shape[1]
    groups, pin, pout = w_pool.shape
    d_model = pool_scale.shape[0]
    n_p = p.shape[1]
    assert seq <= POOL_BUF and seq % 8 == 0
    t = nseq * seq
    off = row_off // t
    in_specs = [
        pl.BlockSpec((t, n_p), lambda i: (off + i, 0)),
        pl.BlockSpec((t, LANE), lambda i: (off + i, 0)),
        pl.BlockSpec((None, nseq, nh, dk, dv), lambda i: (layer, i, 0, 0, 0)),
        pl.BlockSpec((None, nseq, POOL_BUF, groups * pin), lambda i: (layer, i, 0, 0)),
        pl.BlockSpec((LANE, dqk), lambda i: (0, 0)),
        pl.BlockSpec((1, dqk), lambda i: (0, 0)),
        pl.BlockSpec((1, dv), lambda i: (0, 0)),
        pl.BlockSpec((groups, pin, pout), lambda i: (0, 0, 0)),
        pl.BlockSpec((1, d_model), lambda i: (0, 0)),
    ]
    args = [p, alr, state_gla, state_pool, w_a2, b_a.reshape(1, dqk), gla_norm_g.reshape(1, dv), w_pool,
            pool_scale.reshape(1, d_model)]
    aliases = {}
    n_alias = 0
    if prev_gla is not None:
        in_specs += [pl.BlockSpec(memory_space=pl.ANY), pl.BlockSpec(memory_space=pl.ANY)]
        args += [prev_gla, prev_pool]
        aliases = {9: 1, 10: 2}
        n_alias = 2
    out_specs = [
        pl.BlockSpec((t, d_model), lambda i: (i, 0)),
        pl.BlockSpec((None, nseq, nh, dk, dv), lambda i: (layer, i, 0, 0, 0)),
        pl.BlockSpec((None, nseq, POOL_BUF, groups * pin), lambda i: (layer, i, 0, 0)),
    ]
    out_shape = [
        jax.ShapeDtypeStruct((bs * seq, d_model), BF16),
        jax.ShapeDtypeStruct(state_gla.shape, F32),
        jax.ShapeDtypeStruct(state_pool.shape, F32),
    ]
    return pl.pallas_call(
        functools.partial(_mix_sample_kernel, seq=seq, dqk=dqk, d_model=d_model, n_alias=n_alias),
        grid=(bs // nseq,),
        in_specs=in_specs,
        out_specs=out_specs,
        out_shape=out_shape,
        scratch_shapes=[pltpu.VMEM((nseq * CARRY_ROWS, groups * pin), F32)],
        input_output_aliases=aliases,
        compiler_params=_cparams("arbitrary"),
        name="mix_sample",
    )(*args)


def _mm_o_kernel(a_ref, w_ref, x_ref, g1_ref, ng_ref, sc_ref, sh_ref, *rest):
    x1_ref, h_ref = rest[-2:]
    bb, tt, d = x_ref.shape
    y = _dot(a_ref[...], w_ref[...]).reshape(bb, tt, d)
    x1 = x_ref[...] + g1_ref[...] * y
    x1_ref[...] = x1
    h = _rms(x1) * ng_ref[...] * (1.0 + sc_ref[...]) + sh_ref[...]
    h_ref[...] = h.reshape(bb * tt, d).astype(BF16)


def _mm_o_call(merged, x3, w_o, mod4, layer, norm2_g, m_total, row_off, prev, target=256):
    b, t, d = x3.shape
    bb, tt = _row_tiles(b, t, target)
    nt = t // tt
    rows = bb * tt
    off = row_off // rows
    in_specs = [
        pl.BlockSpec((rows, d), lambda i, j: (i * nt + j, 0)),
        pl.BlockSpec((d, d), lambda i, j: (0, 0)),
        pl.BlockSpec((bb, tt, d), lambda i, j: (i, j, 0)),
        _mod_spec(bb, d, layer, 2),
        pl.BlockSpec((1, d), lambda i, j: (0, 0)),
        _mod_spec(bb, d, layer, 4),
        _mod_spec(bb, d, layer, 3),
    ]
    args = [merged, w_o, x3, mod4, norm2_g.reshape(1, d), mod4, mod4]
    aliases = {}
    if prev is not None:
        in_specs.append(pl.BlockSpec(memory_space=pl.ANY))
        args.append(prev)
        aliases = {7: 1}
    return pl.pallas_call(
        _mm_o_kernel,
        grid=(b // bb, nt),
        in_specs=in_specs,
        out_specs=[
            pl.BlockSpec((bb, tt, d), lambda i, j: (i, j, 0)),
            pl.BlockSpec((rows, d), lambda i, j: (off + i * nt + j, 0)),
        ],
        out_shape=[jax.ShapeDtypeStruct((b, t, d), F32), jax.ShapeDtypeStruct((m_total, d), BF16)],
        input_output_aliases=aliases,
        compiler_params=_cparams("arbitrary", "arbitrary"),
        name="out_proj",
    )(*args)


def _mm_gu_kernel(a_ref, wg_ref, wu_ref, o_ref):
    a = a_ref[...]
    o_ref[...] = (_silu(_dot(a, wg_ref[...])) * _dot(a, wu_ref[...])).astype(BF16)


def _mm_gu_call(h, w_gu, tm=1024):
    m, k = h.shape
    dff = w_gu.shape[1] // 2
    tn = _largest_tile(dff, 512)
    nj = dff // tn
    return pl.pallas_call(
        _mm_gu_kernel,
        grid=(m // tm, nj),
        in_specs=[
            pl.BlockSpec((tm, k), lambda i, j: (i, 0)),
            pl.BlockSpec((k, tn), lambda i, j: (0, j)),
            pl.BlockSpec((k, tn), lambda i, j: (0, nj + j)),
        ],
        out_specs=pl.BlockSpec((tm, tn), lambda i, j: (i, j)),
        out_shape=jax.ShapeDtypeStruct((m, dff), BF16),
        compiler_params=_cparams("arbitrary", "arbitrary"),
        name="gate_up",
    )(h, w_gu, w_gu)


def _mm_down_kernel(a_ref, w_ref, x_ref, g2_ref, ng_ref, *rest, final):
    if final:
        y_ref, acc_ref = rest[-2:]
    else:
        sc_ref, sh_ref = rest[:2]
        x2_ref, h_ref, acc_ref = rest[-3:]
    kk = pl.program_id(2)
    bb, tt, d = x_ref.shape

    @pl.when(kk == 0)
    def _():
        acc_ref[...] = jnp.zeros_like(acc_ref)

    acc_ref[...] += _dot(a_ref[...], w_ref[...])

    @pl.when(kk == pl.num_programs(2) - 1)
    def _():
        x2 = x_ref[...] + g2_ref[...] * acc_ref[...].reshape(bb, tt, d)
        if final:
            y_ref[...] = _rms(x2) * ng_ref[...]
        else:
            x2_ref[...] = x2
            h = _rms(x2) * ng_ref[...] * (1.0 + sc_ref[...]) + sh_ref[...]
            h_ref[...] = h.reshape(bb * tt, d).astype(BF16)


def _mm_down_call(act, act_row_off, x3, w_down, mod4, layer, norm_g, final, m_total, row_off, prev,
                  target=512):
    b, t, d = x3.shape
    dff = w_down.shape[0]
    bk = _largest_tile(dff, 1536)
    nk = dff // bk
    bb, tt = _row_tiles(b, t, target)
    nt = t // tt
    rows = bb * tt
    a_off = act_row_off // rows
    off = row_off // rows
    in_specs = [
        pl.BlockSpec((rows, bk), lambda i, j, k: (a_off + i * nt + j, k)),
        pl.BlockSpec((bk, d), lambda i, j, k: (k, 0)),
        pl.BlockSpec((bb, tt, d), lambda i, j, k: (i, j, 0)),
        pl.BlockSpec((None, bb, 1, d), lambda i, j, k: (layer, i, 0, 5)),
        pl.BlockSpec((1, d), lambda i, j, k: (0, 0)),
    ]
    args = [act, w_down, x3, mod4, norm_g.reshape(1, d)]
    aliases = {}
    if final:
        out_specs = [pl.BlockSpec((bb, tt, d), lambda i, j, k: (i, j, 0))]
        out_shape = [jax.ShapeDtypeStruct((b, t, d), F32)]
    else:
        in_specs += [
            pl.BlockSpec((None, bb, 1, d), lambda i, j, k: (layer + 1, i, 0, 1)),
            pl.BlockSpec((None, bb, 1, d), lambda i, j, k: (layer + 1, i, 0, 0)),
        ]
        args += [mod4, mod4]
        if prev is not None:
            in_specs.append(pl.BlockSpec(memory_space=pl.ANY))
            args.append(prev)
            aliases = {7: 1}
        out_specs = [
            pl.BlockSpec((bb, tt, d), lambda i, j, k: (i, j, 0)),
            pl.BlockSpec((rows, d), lambda i, j, k: (off + i * nt + j, 0)),
        ]
        out_shape = [jax.ShapeDtypeStruct((b, t, d), F32), jax.ShapeDtypeStruct((m_total, d), BF16)]
    return pl.pallas_call(
        functools.partial(_mm_down_kernel, final=final),
        grid=(b // bb, nt, nk),
        in_specs=in_specs,
        out_specs=out_specs,
        out_shape=out_shape,
        scratch_shapes=[pltpu.VMEM((rows, d), F32)],
        input_output_aliases=aliases,
        compiler_params=_cparams("arbitrary", "arbitrary", "arbitrary"),
        name="down_proj",
    )(*args)


def kernel(x_prompt, x_sample, state_gla, state_pool, c_prompt, c_sample, w_ada, b_ada, norm1_g, w_in, w_a2, b_a,
           gla_norm_g, w_pool, pool_scale, w_o, norm2_g, w_gu, w_down, final_norm_g):
    depth, d, n6 = w_ada.shape
    bp, lp, _ = x_prompt.shape
    bs, ls, _ = x_sample.shape
    mp, ms = bp * lp, bs * ls
    m_total = mp + ms
    rank, dqk = w_a2.shape[1:]
    n_main = w_in.shape[2] - rank

    c_all = jnp.concatenate([c_prompt, c_sample], axis=0)
    pad = (-c_all.shape[0]) % 8
    c_all = jnp.pad(c_all, ((0, pad), (0, 0)))
    mod = _ada_call(c_all, w_ada, b_ada)
    mod_p = mod[:, :bp].reshape(depth, bp, 1, n6)
    mod_s = mod[:, bp:bp + bs].reshape(depth, bs, 1, n6)

    w_in_b = w_in[:, :, :n_main].astype(BF16)
    w_alr_b = jnp.pad(w_in[:, :, n_main:], ((0, 0), (0, 0), (0, LANE - rank))).astype(BF16)
    w_o_b = w_o.astype(BF16)
    w_gu_b = w_gu.astype(BF16)
    w_down_b = w_down.astype(BF16)
    w_a2_p = jnp.pad(w_a2, ((0, 0), (0, LANE - rank), (0, 0)))

    xp, xs = x_prompt, x_sample
    h = _pre_call(xp, mod_p, 0, norm1_g[0], m_total, 0, None)
    h = _pre_call(xs, mod_s, 0, norm1_g[0], m_total, mp, h)

    gla_p, pool_p = [], []
    gla_s = pool_s = None
    for l in range(depth):
        p, alr = _mm_in_call(h, w_in_b[l], w_alr_b[l])
        merged_p, s_p, b_p = _mix_prompt_call(p, alr, bp, lp, w_a2_p[l], b_a[l], gla_norm_g[l], w_pool[l],
                                              pool_scale[l])
        merged_s, gla_s, pool_s = _mix_sample_call(p, alr, mp, ls, l, state_gla, state_pool, w_a2_p[l], b_a[l],
                                                   gla_norm_g[l], w_pool[l], pool_scale[l], gla_s, pool_s)
        gla_p.append(s_p)
        pool_p.append(b_p)

        xp, h2 = _mm_o_call(merged_p, xp, w_o_b[l], mod_p, l, norm2_g[l], m_total, 0, None)
        xs, h2 = _mm_o_call(merged_s, xs, w_o_b[l], mod_s, l, norm2_g[l], m_total, mp, h2)
        act = _mm_gu_call(h2, w_gu_b[l])
        if l + 1 < depth:
            xp, h = _mm_down_call(act, 0, xp, w_down_b[l], mod_p, l, norm1_g[l + 1], False, m_total, 0, None)
            xs, h = _mm_down_call(act, mp, xs, w_down_b[l], mod_s, l, norm1_g[l + 1], False, m_total, mp, h)
        else:
            (yp,) = _mm_down_call(act, 0, xp, w_down_b[l], mod_p, l, final_norm_g, True, m_total, 0, None)
            (ys,) = _mm_down_call(act, mp, xs, w_down_b[l], mod_s, l, final_norm_g, True, m_total, mp, None)
    return yp, ys, jnp.stack(gla_p), jnp.stack(pool_p), gla_s, pool_s
```

```python
import functools

import jax
import jax.numpy as jnp
from jax import lax
from jax.experimental import pallas as pl
from jax.experimental.pallas import tpu as pltpu

F32 = jnp.float32
BF16 = jnp.bfloat16

N_HEADS = 4
POOL_WINDOWS = (2, 4, 8, 16)
POOL_BUF = 15
CARRY_ROWS = POOL_BUF + 1
GATE_NORMALIZER = 16.0
PAST_LEN = 16384
EPS = 1e-6
STABLE_BLOCK = 16
LANE = 128
VMEM_LIMIT = 56 * 1024 * 1024


def _cparams(*sem):
    return pltpu.CompilerParams(dimension_semantics=sem, vmem_limit_bytes=VMEM_LIMIT)


def _largest_tile(n, cap):
    best = None
    for t in range(LANE, cap + 1, LANE):
        if n % t == 0:
            best = t
    assert best is not None, (n, cap)
    return best


def _dot(a, b):
    return jnp.dot(a, b, preferred_element_type=F32)


def _dot_nt(a, b):
    return lax.dot_general(a, b, (((1,), (1,)), ((), ())), preferred_element_type=F32)


def _dot_tn(a, b):
    return lax.dot_general(a, b, (((0,), (0,)), ((), ())), preferred_element_type=F32)


def _split3(x):
    hi = x.astype(BF16)
    r1 = x - hi.astype(F32)
    mid = r1.astype(BF16)
    lo = (r1 - mid.astype(F32)).astype(BF16)
    return hi, mid, lo


def _dot3(m, parts):
    return _dot(m, parts[0]) + _dot(m, parts[1]) + _dot(m, parts[2])


def _sigmoid(x):
    return 1.0 / (1.0 + jnp.exp(-x))


def _silu(x):
    return x * _sigmoid(x)


def _log_sigmoid(x):
    return jnp.minimum(x, 0.0) - jnp.log(1.0 + jnp.exp(-jnp.abs(x)))


def _rms(x):
    return x * lax.rsqrt(jnp.mean(x * x, axis=-1, keepdims=True) + EPS)


def _rows_bcast(x, row, n):
    return jnp.broadcast_to(x[row:row + 1, :], (n, x.shape[1]))


def _cat(parts, axis):
    return parts[0] if len(parts) == 1 else jnp.concatenate(parts, axis=axis)


def _iota2(shape, dim):
    return lax.broadcasted_iota(jnp.int32, shape, dim)


def _blk(x, n):
    assert n & (n - 1) == 0
    return lax.shift_right_logical(x, jnp.int32(n.bit_length() - 1))


def _rem(x, n):
    assert n & (n - 1) == 0
    return x & jnp.int32(n - 1)


def _layer_spec(block, layer, tail):
    return pl.BlockSpec((None,) + block, lambda *g: (layer,) + tail(*g))


def _row_tiles(b, t, target):
    if t >= target:
        return 1, target
    return target // t, t


def _mod_spec(bb, d, layer, j):
    return pl.BlockSpec((None, bb, 1, d), lambda *g: (layer, g[0], 0, j))


def _ada_kernel(c_ref, w_ref, b_ref, o_ref):
    a = _silu(c_ref[...]).astype(BF16)
    o_ref[...] = _dot(a, w_ref[...].astype(BF16)) + b_ref[...]


def _ada_call(c_all, w_ada, b_ada):
    depth, d, n6 = w_ada.shape
    bp = c_all.shape[0]
    tn = _largest_tile(n6, 1024)
    return pl.pallas_call(
        _ada_kernel,
        grid=(depth, n6 // tn),
        in_specs=[
            pl.BlockSpec((bp, d), lambda l, j: (0, 0)),
            pl.BlockSpec((None, d, tn), lambda l, j: (l, 0, j)),
            pl.BlockSpec((None, 1, tn), lambda l, j: (l, 0, j)),
        ],
        out_specs=pl.BlockSpec((None, bp, tn), lambda l, j: (l, 0, j)),
        out_shape=jax.ShapeDtypeStruct((depth, bp, n6), F32),
        compiler_params=_cparams("arbitrary", "arbitrary"),
        name="ada_mod",
    )(c_all, w_ada, b_ada.reshape(depth, 1, n6))


def _pre_kernel(x_ref, g_ref, sc_ref, sh_ref, *rest):
    o_ref = rest[-1]
    x = x_ref[...]
    bb, tt, d = x.shape
    h = _rms(x) * g_ref[...] * (1.0 + sc_ref[...]) + sh_ref[...]
    o_ref[...] = h.reshape(bb * tt, d).astype(BF16)


def _pre_call(x3, mod4, layer, norm_g, m_total, row_off, prev, target=512):
    b, t, d = x3.shape
    bb, tt = _row_tiles(b, t, target)
    nt = t // tt
    off = row_off // (bb * tt)
    in_specs = [
        pl.BlockSpec((bb, tt, d), lambda i, j: (i, j, 0)),
        _layer_spec((1, d), layer, lambda i, j: (0, 0)),
        _mod_spec(bb, d, layer, 1),
        _mod_spec(bb, d, layer, 0),
    ]
    args = [x3, norm_g, mod4, mod4]
    aliases = {}
    if prev is not None:
        in_specs.append(pl.BlockSpec(memory_space=pl.ANY))
        args.append(prev)
        aliases = {4: 0}
    return pl.pallas_call(
        _pre_kernel,
        grid=(b // bb, nt),
        in_specs=in_specs,
        out_specs=pl.BlockSpec((bb * tt, d), lambda i, j: (off + i * nt + j, 0)),
        out_shape=jax.ShapeDtypeStruct((m_total, d), BF16),
        input_output_aliases=aliases,
        compiler_params=_cparams("arbitrary", "arbitrary"),
        name="prologue",
    )(*args)


def _mm_in_kernel(a_ref, w_ref, p_ref, wb_ref):
    @pl.when(pl.program_id(1) == 0)
    def _():
        wb_ref[...] = w_ref[...].astype(BF16)

    p_ref[...] = _dot(a_ref[...], wb_ref[...])


def _mm_in_call(h, w_in, layer, n_main, tm=1024):
    m, k = h.shape
    tn = _largest_tile(n_main, 1024)
    return pl.pallas_call(
        _mm_in_kernel,
        grid=(n_main // tn, m // tm),
        in_specs=[
            pl.BlockSpec((tm, k), lambda j, i: (i, 0)),
            _layer_spec((k, tn), layer, lambda j, i: (0, j)),
        ],
        out_specs=pl.BlockSpec((tm, tn), lambda j, i: (i, j)),
        out_shape=jax.ShapeDtypeStruct((m, n_main), F32),
        scratch_shapes=[pltpu.VMEM((k, tn), BF16)],
        compiler_params=_cparams("arbitrary", "arbitrary"),
        name="in_proj",
    )(h, w_in)


def _mm_alr_kernel(a_ref, w_ref, o_ref):
    o_ref[...] = _dot(a_ref[...], w_ref[...])


def _mm_alr_call(h, w_alr, layer, tm=1024):
    m, k = h.shape
    return pl.pallas_call(
        _mm_alr_kernel,
        grid=(m // tm,),
        in_specs=[
            pl.BlockSpec((tm, k), lambda i: (i, 0)),
            _layer_spec((k, LANE), layer, lambda i: (0, 0)),
        ],
        out_specs=pl.BlockSpec((tm, LANE), lambda i: (i, 0)),
        out_shape=jax.ShapeDtypeStruct((m, LANE), F32),
        compiler_params=_cparams("arbitrary"),
        name="gate_lowrank",
    )(h, w_alr)


def _gla_consts(t, seq):
    row = _iota2((t, t), 0)
    col = _iota2((t, t), 1)
    rowv = _iota2((t, 1), 0)
    nb = min(STABLE_BLOCK, seq)
    c = {"t": t, "seq": seq, "nb": nb}
    c["tri"] = ((col <= row) & (_blk(row, seq) == _blk(col, seq))).astype(BF16)
    c["mask0"] = (_blk(row, nb) == _blk(col, nb)) & (col <= row)
    levels = []
    n = nb * 2
    while n <= seq:
        levels.append((n, _rem(rowv, n) >= n // 2, _blk(row, n) == _blk(col, n)))
        n *= 2
    c["levels"] = levels
    return c


def _gla_factors(q, k, b, c):
    t, seq, nb = c["t"], c["seq"], c["nb"]
    dk = q.shape[1]
    parts = []
    for j in range(t // nb):
        start = j * nb
        parts.append(jnp.zeros((nb, dk), F32) if start % seq == 0 else _rows_bcast(b, start - 1, nb))
    e0 = b - _cat(parts, 0)
    qe0 = (q * jnp.exp(e0)).astype(BF16)
    ke0 = (k * jnp.exp(-e0)).astype(BF16)
    att = jnp.where(c["mask0"], _dot_nt(qe0, ke0), 0.0)
    for n, second, same in c["levels"]:
        half = n // 2
        x = b - _cat([_rows_bcast(b, j * n + half - 1, n) for j in range(t // n)], 0)
        f = jnp.exp(jnp.where(second, x, -x))
        qk = jnp.where(second, q, k) * f
        ql = jnp.where(second, qk, 0.0).astype(BF16)
        kl = jnp.where(second, 0.0, qk).astype(BF16)
        att = att + jnp.where(same, _dot_nt(ql, kl), 0.0)
    b_last = _cat([_rows_bcast(b, (j + 1) * seq - 1, seq) for j in range(t // seq)], 0)
    qe = q * jnp.exp(b)
    kd = k * jnp.exp(b_last - b)
    return att, qe, kd


def _lane_tile(x, width):
    return _cat([x] * (width // x.shape[1]), 1)


def _pool_delta(u, carry, w, pos0, seq):
    t = u.shape[0]
    npiece = t // seq
    row = _iota2((t, t), 0)
    col = _iota2((t, t), 1)
    band = ((_blk(row, seq) == _blk(col, seq)) & (col <= row) & (col > row - w)).astype(BF16)
    rc = _iota2((t, npiece * CARRY_ROWS), 0)
    cc = _iota2((t, npiece * CARRY_ROWS), 1)
    cband = ((_blk(rc, seq) == _blk(cc, CARRY_ROWS))
             & (_rem(cc, CARRY_ROWS) > _rem(rc, seq) + CARRY_ROWS - w)).astype(BF16)
    s = _dot3(band, _split3(u)) + _dot3(cband, _split3(carry))
    cnt = jnp.minimum(pos0 + 1, w).astype(F32)
    return s / cnt - u


def _mix_epilogue(o, og, ga, gb, d, gn, wp, ps):
    y_a = _rms(o) * gn * _silu(og)
    y_b = _dot(d.astype(BF16), wp) * ps
    return _sigmoid(ga) * y_a + _sigmoid(gb) * y_b


def _gate_log_decay(alr, wa2, ba):
    gk = _dot(alr.astype(BF16), wa2) + ba
    return _log_sigmoid(gk) * (1.0 / GATE_NORMALIZER)


def _p_columns(dqk, d_model, d_pool):
    q0, k0, v0 = 0, dqk, 2 * dqk
    og0 = v0 + d_model
    u0 = og0 + d_model
    ga0 = u0 + d_pool
    gb0 = ga0 + d_model
    return q0, k0, v0, og0, u0, ga0, gb0


def _mix_prompt_kernel(p_ref, alr_ref, wa2_ref, ba_ref, gn_ref, wp_ref, ps_ref,
                       m_ref, s_ref, pool_out_ref, car_ref, *, dqk, d_model):
    c = pl.program_id(1)
    t = p_ref.shape[0]
    groups, pin, dv = wp_ref.shape
    dk = dqk // N_HEADS
    q0, k0, v0, og0, u0, ga0, gb0 = _p_columns(dqk, d_model, groups * pin)

    @pl.when(c == 0)
    def _():
        s_ref[...] = jnp.zeros_like(s_ref)
        car_ref[...] = jnp.zeros_like(car_ref)

    consts = _gla_consts(t, t)
    loga3 = _split3(_gate_log_decay(alr_ref[...], wa2_ref[...], ba_ref[...]))
    b_all = _dot3(consts["tri"], loga3)
    ones = jnp.ones((t, LANE), BF16)
    decay_all = jnp.exp(_dot_tn(loga3[0], ones) + _dot_tn(loga3[1], ones) + _dot_tn(loga3[2], ones))
    pos = c * t + _iota2((t, 1), 0)
    gn = gn_ref[...]

    for h in range(N_HEADS):
        q = p_ref[:, q0 + h * dk:q0 + (h + 1) * dk] * (dk ** -0.5)
        k = p_ref[:, k0 + h * dk:k0 + (h + 1) * dk]
        vb = p_ref[:, v0 + h * dv:v0 + (h + 1) * dv].astype(BF16)
        att, qe, kd = _gla_factors(q, k, b_all[:, h * dk:(h + 1) * dk], consts)
        s = s_ref[h]
        o = _dot(att.astype(BF16), vb) + _dot(qe.astype(BF16), s.astype(BF16))
        decay = decay_all[h * dk:(h + 1) * dk]
        s_ref[h] = _lane_tile(decay, dv) * s + _dot_tn(kd.astype(BF16), vb)

        u = p_ref[:, u0 + h * pin:u0 + (h + 1) * pin]
        d = _pool_delta(u, car_ref[:, h * pin:(h + 1) * pin], POOL_WINDOWS[h], pos, t)
        merged = _mix_epilogue(o, p_ref[:, og0 + h * dv:og0 + (h + 1) * dv],
                               p_ref[:, ga0 + h * dv:ga0 + (h + 1) * dv],
                               p_ref[:, gb0 + h * dv:gb0 + (h + 1) * dv], d,
                               gn, wp_ref[h], ps_ref[:, h * dv:(h + 1) * dv])
        m_ref[:, h * dv:(h + 1) * dv] = merged.astype(BF16)

    car_ref[...] = p_ref[t - CARRY_ROWS:, u0:u0 + groups * pin]

    @pl.when(c == pl.num_programs(1) - 1)
    def _():
        pool_out_ref[...] = p_ref[t - POOL_BUF:, u0:u0 + groups * pin]


def _mix_prompt_call(p, alr, batch, seq_len, layer, w_a2, b_a, gla_norm_g, w_pool, pool_scale, chunk=128):
    dqk = w_a2.shape[2]
    dk = dqk // N_HEADS
    groups, pin, dv = w_pool.shape[1:]
    d_model = pool_scale.shape[2]
    n_p = p.shape[1]
    nc = seq_len // chunk
    const = lambda b, c: (0, 0)
    in_specs = [
        pl.BlockSpec((chunk, n_p), lambda b, c: (b * nc + c, 0)),
        pl.BlockSpec((chunk, LANE), lambda b, c: (b * nc + c, 0)),
        _layer_spec((LANE, dqk), layer, const),
        _layer_spec((1, dqk), layer, const),
        _layer_spec((1, dv), layer, const),
        _layer_spec((groups, pin, dv), layer, lambda b, c: (0, 0, 0)),
        _layer_spec((1, d_model), layer, const),
    ]
    out_specs = [
        pl.BlockSpec((chunk, d_model), lambda b, c: (b * nc + c, 0)),
        pl.BlockSpec((None, N_HEADS, dk, dv), lambda b, c: (b, 0, 0, 0)),
        pl.BlockSpec((None, POOL_BUF, groups * pin), lambda b, c: (b, 0, 0)),
    ]
    out_shape = [
        jax.ShapeDtypeStruct((batch * seq_len, d_model), BF16),
        jax.ShapeDtypeStruct((batch, N_HEADS, dk, dv), F32),
        jax.ShapeDtypeStruct((batch, POOL_BUF, groups * pin), F32),
    ]
    return pl.pallas_call(
        functools.partial(_mix_prompt_kernel, dqk=dqk, d_model=d_model),
        grid=(batch, nc),
        in_specs=in_specs,
        out_specs=out_specs,
        out_shape=out_shape,
        scratch_shapes=[pltpu.VMEM((CARRY_ROWS, groups * pin), F32)],
        compiler_params=_cparams("arbitrary", "arbitrary"),
        name="mix_prompt",
    )(p, alr, w_a2, b_a, gla_norm_g, w_pool, pool_scale)


def _mix_sample_kernel(p_ref, alr_ref, s0_ref, buf_ref, wa2_ref, ba_ref, gn_ref, wp_ref, ps_ref, *rest,
                       seq, dqk, d_model, n_alias):
    m_ref, s_out_ref, pool_out_ref, car_ref = rest[n_alias:]
    nseq = s0_ref.shape[0]
    t = nseq * seq
    groups, pin, dv = wp_ref.shape
    dk = dqk // N_HEADS
    q0, k0, v0, og0, u0, ga0, gb0 = _p_columns(dqk, d_model, groups * pin)

    car_ref[...] = jnp.zeros_like(car_ref)
    for j in range(nseq):
        car_ref[j * CARRY_ROWS + 1:(j + 1) * CARRY_ROWS, :] = buf_ref[j]

    consts = _gla_consts(t, seq)
    loga3 = _split3(_gate_log_decay(alr_ref[...], wa2_ref[...], ba_ref[...]))
    b_all = _dot3(consts["tri"], loga3)
    piece = _blk(_iota2((t, LANE), 0), seq)
    decay_all = []
    for j in range(nseq):
        sel = (piece == j).astype(BF16)
        decay_all.append(jnp.exp(_dot_tn(loga3[0], sel) + _dot_tn(loga3[1], sel) + _dot_tn(loga3[2], sel)))
    pos = PAST_LEN + _rem(_iota2((t, 1), 0), seq)
    keep = POOL_BUF - seq
    gn = gn_ref[...]

    for h in range(N_HEADS):
        q = p_ref[:, q0 + h * dk:q0 + (h + 1) * dk] * (dk ** -0.5)
        k = p_ref[:, k0 + h * dk:k0 + (h + 1) * dk]
        vb = p_ref[:, v0 + h * dv:v0 + (h + 1) * dv].astype(BF16)
        att, qe, kd = _gla_factors(q, k, b_all[:, h * dk:(h + 1) * dk], consts)
        kdb = kd.astype(BF16)
        qeb = qe.astype(BF16)
        o_intra = _dot(att.astype(BF16), vb)
        o_parts = []
        for j in range(nseq):
            s0 = s0_ref[j, h]
            lo, hi = j * seq, (j + 1) * seq
            o_parts.append(o_intra[lo:hi] + _dot(qeb[lo:hi], s0.astype(BF16)))
            decay = decay_all[j][h * dk:(h + 1) * dk]
            s_out_ref[j, h] = _lane_tile(decay, dv) * s0 + _dot_tn(kdb[lo:hi], vb[lo:hi])
        o = _cat(o_parts, 0)

        u = p_ref[:, u0 + h * pin:u0 + (h + 1) * pin]
        d = _pool_delta(u, car_ref[:, h * pin:(h + 1) * pin], POOL_WINDOWS[h], pos, seq)
        merged = _mix_epilogue(o, p_ref[:, og0 + h * dv:og0 + (h + 1) * dv],
                               p_ref[:, ga0 + h * dv:ga0 + (h + 1) * dv],
                               p_ref[:, gb0 + h * dv:gb0 + (h + 1) * dv], d,
                               gn, wp_ref[h], ps_ref[:, h * dv:(h + 1) * dv])
        m_ref[:, h * dv:(h + 1) * dv] = merged.astype(BF16)

    for j in range(nseq):
        pool_out_ref[j, 0:keep, :] = buf_ref[j, seq:POOL_BUF, :]
        pool_out_ref[j, keep:POOL_BUF, :] = p_ref[j * seq:(j + 1) * seq, u0:u0 + groups * pin]


def _mix_sample_call(p, alr, row_off, seq, layer, state_gla, state_pool, w_a2, b_a, gla_norm_g, w_pool,
                     pool_scale, prev_gla, prev_pool, nseq=2):
    depth, bs, nh, dk, dv = state_gla.shape
    dqk = w_a2.shape[2]
    groups, pin, _ = w_pool.shape[1:]
    d_model = pool_scale.shape[2]
    n_p = p.shape[1]
    assert seq <= POOL_BUF and seq % 8 == 0
    t = nseq * seq
    off = row_off // t
    const = lambda i: (0, 0)
    in_specs = [
        pl.BlockSpec((t, n_p), lambda i: (off + i, 0)),
        pl.BlockSpec((t, LANE), lambda i: (off + i, 0)),
        pl.BlockSpec((None, nseq, nh, dk, dv), lambda i: (layer, i, 0, 0, 0)),
        pl.BlockSpec((None, nseq, POOL_BUF, groups * pin), lambda i: (layer, i, 0, 0)),
        _layer_spec((LANE, dqk), layer, const),
        _layer_spec((1, dqk), layer, const),
        _layer_spec((1, dv), layer, const),
        _layer_spec((groups, pin, dv), layer, lambda i: (0, 0, 0)),
        _layer_spec((1, d_model), layer, const),
    ]
    args = [p, alr, state_gla, state_pool, w_a2, b_a, gla_norm_g, w_pool, pool_scale]
    aliases = {}
    n_alias = 0
    if prev_gla is not None:
        in_specs += [pl.BlockSpec(memory_space=pl.ANY), pl.BlockSpec(memory_space=pl.ANY)]
        args += [prev_gla, prev_pool]
        aliases = {9: 1, 10: 2}
        n_alias = 2
    out_specs = [
        pl.BlockSpec((t, d_model), lambda i: (i, 0)),
        pl.BlockSpec((None, nseq, nh, dk, dv), lambda i: (layer, i, 0, 0, 0)),
        pl.BlockSpec((None, nseq, POOL_BUF, groups * pin), lambda i: (layer, i, 0, 0)),
    ]
    out_shape = [
        jax.ShapeDtypeStruct((bs * seq, d_model), BF16),
        jax.ShapeDtypeStruct(state_gla.shape, F32),
        jax.ShapeDtypeStruct(state_pool.shape, F32),
    ]
    return pl.pallas_call(
        functools.partial(_mix_sample_kernel, seq=seq, dqk=dqk, d_model=d_model, n_alias=n_alias),
        grid=(bs // nseq,),
        in_specs=in_specs,
        out_specs=out_specs,
        out_shape=out_shape,
        scratch_shapes=[pltpu.VMEM((nseq * CARRY_ROWS, groups * pin), F32)],
        input_output_aliases=aliases,
        compiler_params=_cparams("arbitrary"),
        name="mix_sample",
    )(*args)


def _mm_o_kernel(a_ref, w_ref, x_ref, g1_ref, ng_ref, sc_ref, sh_ref, *rest):
    x1_ref, h_ref = rest[-2:]
    bb, tt, d = x_ref.shape
    y = _dot(a_ref[...], w_ref[...]).reshape(bb, tt, d)
    x1 = x_ref[...] + g1_ref[...] * y
    x1_ref[...] = x1
    h = _rms(x1) * ng_ref[...] * (1.0 + sc_ref[...]) + sh_ref[...]
    h_ref[...] = h.reshape(bb * tt, d).astype(BF16)


def _mm_o_call(merged, x3, w_o, mod4, layer, norm2_g, m_total, row_off, prev, target=256):
    b, t, d = x3.shape
    bb, tt = _row_tiles(b, t, target)
    nt = t // tt
    rows = bb * tt
    off = row_off // rows
    in_specs = [
        pl.BlockSpec((rows, d), lambda i, j: (i * nt + j, 0)),
        _layer_spec((d, d), layer, lambda i, j: (0, 0)),
        pl.BlockSpec((bb, tt, d), lambda i, j: (i, j, 0)),
        _mod_spec(bb, d, layer, 2),
        _layer_spec((1, d), layer, lambda i, j: (0, 0)),
        _mod_spec(bb, d, layer, 4),
        _mod_spec(bb, d, layer, 3),
    ]
    args = [merged, w_o, x3, mod4, norm2_g, mod4, mod4]
    aliases = {}
    if prev is not None:
        in_specs.append(pl.BlockSpec(memory_space=pl.ANY))
        args.append(prev)
        aliases = {7: 1}
    return pl.pallas_call(
        _mm_o_kernel,
        grid=(b // bb, nt),
        in_specs=in_specs,
        out_specs=[
            pl.BlockSpec((bb, tt, d), lambda i, j: (i, j, 0)),
            pl.BlockSpec((rows, d), lambda i, j: (off + i * nt + j, 0)),
        ],
        out_shape=[jax.ShapeDtypeStruct((b, t, d), F32), jax.ShapeDtypeStruct((m_total, d), BF16)],
        input_output_aliases=aliases,
        compiler_params=_cparams("arbitrary", "arbitrary"),
        name="out_proj",
    )(*args)


def _mm_gu_kernel(a_ref, wg_ref, wu_ref, o_ref, wgb_ref, wub_ref):
    @pl.when(pl.program_id(1) == 0)
    def _():
        wgb_ref[...] = wg_ref[...].astype(BF16)
        wub_ref[...] = wu_ref[...].astype(BF16)

    a = a_ref[...]
    o_ref[...] = (_silu(_dot(a, wgb_ref[...])) * _dot(a, wub_ref[...])).astype(BF16)


def _mm_gu_call(h, w_gu, layer, tm=1024):
    m, k = h.shape
    dff = w_gu.shape[2] // 2
    tn = _largest_tile(dff, 512)
    nj = dff // tn
    return pl.pallas_call(
        _mm_gu_kernel,
        grid=(nj, m // tm),
        in_specs=[
            pl.BlockSpec((tm, k), lambda j, i: (i, 0)),
            _layer_spec((k, tn), layer, lambda j, i: (0, j)),
            _layer_spec((k, tn), layer, lambda j, i: (0, nj + j)),
        ],
        out_specs=pl.BlockSpec((tm, tn), lambda j, i: (i, j)),
        out_shape=jax.ShapeDtypeStruct((m, dff), BF16),
        scratch_shapes=[pltpu.VMEM((k, tn), BF16), pltpu.VMEM((k, tn), BF16)],
        compiler_params=_cparams("arbitrary", "arbitrary"),
        name="gate_up",
    )(h, w_gu, w_gu)


def _mm_down_kernel(a_ref, w_ref, x_ref, g2_ref, ng_ref, *rest, final):
    if final:
        y_ref, acc_ref = rest[-2:]
    else:
        sc_ref, sh_ref = rest[:2]
        x2_ref, h_ref, acc_ref = rest[-3:]
    kk = pl.program_id(2)
    bb, tt, d = x_ref.shape

    @pl.when(kk == 0)
    def _():
        acc_ref[...] = jnp.zeros_like(acc_ref)

    acc_ref[...] += _dot(a_ref[...], w_ref[...])

    @pl.when(kk == pl.num_programs(2) - 1)
    def _():
        x2 = x_ref[...] + g2_ref[...] * acc_ref[...].reshape(bb, tt, d)
        if final:
            y_ref[...] = _rms(x2) * ng_ref[...]
        else:
            x2_ref[...] = x2
            h = _rms(x2) * ng_ref[...] * (1.0 + sc_ref[...]) + sh_ref[...]
            h_ref[...] = h.reshape(bb * tt, d).astype(BF16)


def _mm_down_call(act, act_row_off, x3, w_down, mod4, layer, norm_g, norm_layer, final, m_total, row_off, prev,
                  target=512):
    b, t, d = x3.shape
    dff = w_down.shape[1]
    bk = _largest_tile(dff, 1536)
    nk = dff // bk
    bb, tt = _row_tiles(b, t, target)
    nt = t // tt
    rows = bb * tt
    a_off = act_row_off // rows
    off = row_off // rows
    in_specs = [
        pl.BlockSpec((rows, bk), lambda i, j, k: (a_off + i * nt + j, k)),
        _layer_spec((bk, d), layer, lambda i, j, k: (k, 0)),
        pl.BlockSpec((bb, tt, d), lambda i, j, k: (i, j, 0)),
        _mod_spec(bb, d, layer, 5),
        _layer_spec((1, d), norm_layer, lambda i, j, k: (0, 0)),
    ]
    args = [act, w_down, x3, mod4, norm_g]
    aliases = {}
    if final:
        out_specs = [pl.BlockSpec((bb, tt, d), lambda i, j, k: (i, j, 0))]
        out_shape = [jax.ShapeDtypeStruct((b, t, d), F32)]
    else:
        in_specs += [_mod_spec(bb, d, layer + 1, 1), _mod_spec(bb, d, layer + 1, 0)]
        args += [mod4, mod4]
        if prev is not None:
            in_specs.append(pl.BlockSpec(memory_space=pl.ANY))
            args.append(prev)
            aliases = {7: 1}
        out_specs = [
            pl.BlockSpec((bb, tt, d), lambda i, j, k: (i, j, 0)),
            pl.BlockSpec((rows, d), lambda i, j, k: (off + i * nt + j, 0)),
        ]
        out_shape = [jax.ShapeDtypeStruct((b, t, d), F32), jax.ShapeDtypeStruct((m_total, d), BF16)]
    return pl.pallas_call(
        functools.partial(_mm_down_kernel, final=final),
        grid=(b // bb, nt, nk),
        in_specs=in_specs,
        out_specs=out_specs,
        out_shape=out_shape,
        scratch_shapes=[pltpu.VMEM((rows, d), F32)],
        input_output_aliases=aliases,
        compiler_params=_cparams("arbitrary", "arbitrary", "arbitrary"),
        name="down_proj",
    )(*args)


def kernel(x_prompt, x_sample, state_gla, state_pool, c_prompt, c_sample, w_ada, b_ada, norm1_g, w_in, w_a2, b_a,
           gla_norm_g, w_pool, pool_scale, w_o, norm2_g, w_gu, w_down, final_norm_g):
    depth, d, n6 = w_ada.shape
    bp, lp, _ = x_prompt.shape
    bs, ls, _ = x_sample.shape
    mp, ms = bp * lp, bs * ls
    m_total = mp + ms
    rank, dqk = w_a2.shape[1:]
    n_main = w_in.shape[2] - rank

    c_all = jnp.concatenate([c_prompt, c_sample], axis=0)
    c_all = jnp.pad(c_all, ((0, (-c_all.shape[0]) % 8), (0, 0)))
    mod = _ada_call(c_all, w_ada, b_ada)
    mod_p = mod[:, :bp].reshape(depth, bp, 1, n6)
    mod_s = mod[:, bp:bp + bs].reshape(depth, bs, 1, n6)

    w_alr_b = jnp.pad(w_in[:, :, n_main:], ((0, 0), (0, 0), (0, LANE - rank))).astype(BF16)
    w_o_b = w_o.astype(BF16)
    w_down_b = w_down.astype(BF16)
    w_a2_b = jnp.pad(w_a2, ((0, 0), (0, LANE - rank), (0, 0))).astype(BF16)
    w_pool_b = w_pool.astype(BF16)
    n1 = norm1_g.reshape(depth, 1, d)
    n2 = norm2_g.reshape(depth, 1, d)
    nf = final_norm_g.reshape(1, 1, d)
    ba3 = b_a.reshape(depth, 1, dqk)
    gn3 = gla_norm_g.reshape(depth, 1, -1)
    ps3 = pool_scale.reshape(depth, 1, d)

    xp, xs = x_prompt, x_sample
    h = _pre_call(xp, mod_p, 0, n1, m_total, 0, None)
    h = _pre_call(xs, mod_s, 0, n1, m_total, mp, h)

    gla_p, pool_p = [], []
    gla_s = pool_s = None
    for l in range(depth):
        p = _mm_in_call(h, w_in, l, n_main)
        alr = _mm_alr_call(h, w_alr_b, l)
        merged_p, s_p, b_p = _mix_prompt_call(p, alr, bp, lp, l, w_a2_b, ba3, gn3, w_pool_b, ps3)
        merged_s, gla_s, pool_s = _mix_sample_call(p, alr, mp, ls, l, state_gla, state_pool, w_a2_b, ba3, gn3,
                                                   w_pool_b, ps3, gla_s, pool_s)
        gla_p.append(s_p)
        pool_p.append(b_p)

        xp, h2 = _mm_o_call(merged_p, xp, w_o_b, mod_p, l, n2, m_total, 0, None)
        xs, h2 = _mm_o_call(merged_s, xs, w_o_b, mod_s, l, n2, m_total, mp, h2)
        act = _mm_gu_call(h2, w_gu, l)
        if l + 1 < depth:
            xp, h = _mm_down_call(act, 0, xp, w_down_b, mod_p, l, n1, l + 1, False, m_total, 0, None)
            xs, h = _mm_down_call(act, mp, xs, w_down_b, mod_s, l, n1, l + 1, False, m_total, mp, h)
        else:
            (yp,) = _mm_down_call(act, 0, xp, w_down_b, mod_p, l, nf, 0, True, m_total, 0, None)
            (ys,) = _mm_down_call(act, mp, xs, w_down_b, mod_s, l, nf, 0, True, m_total, mp, None)
    return yp, ys, jnp.stack(gla_p), jnp.stack(pool_p), gla_s, pool_s
```

```python
import functools

import jax
import jax.numpy as jnp
from jax import lax
from jax.experimental import pallas as pl
from jax.experimental.pallas import tpu as pltpu

F32 = jnp.float32
BF16 = jnp.bfloat16

N_HEADS = 4
POOL_WINDOWS = (2, 4, 8, 16)
POOL_BUF = 15
CARRY_ROWS = POOL_BUF + 1
GATE_NORMALIZER = 16.0
PAST_LEN = 16384
EPS = 1e-6
STABLE_BLOCK = 16
LOG_DECAY_TERMS = 2
CARRY_TERMS = 3
LANE = 128
VMEM_LIMIT = 56 * 1024 * 1024


def _cparams(*sem):
    return pltpu.CompilerParams(dimension_semantics=sem, vmem_limit_bytes=VMEM_LIMIT)


def _largest_tile(n, cap):
    best = None
    for t in range(LANE, cap + 1, LANE):
        if n % t == 0:
            best = t
    assert best is not None, (n, cap)
    return best


def _dot(a, b):
    return jnp.dot(a, b, preferred_element_type=F32)


def _dot_nt(a, b):
    return lax.dot_general(a, b, (((1,), (1,)), ((), ())), preferred_element_type=F32)


def _dot_tn(a, b):
    return lax.dot_general(a, b, (((0,), (0,)), ((), ())), preferred_element_type=F32)


def _split(x, terms):
    parts = []
    for i in range(terms):
        p = x.astype(BF16)
        parts.append(p)
        if i + 1 < terms:
            x = x - p.astype(F32)
    return tuple(parts)


def _dot_parts(m, parts):
    out = _dot(m, parts[0])
    for p in parts[1:]:
        out = out + _dot(m, p)
    return out


def _column_tile(row):
    return jnp.broadcast_to(row, (LANE, row.shape[1])).T


def _sigmoid(x):
    return 1.0 / (1.0 + jnp.exp(-x))


def _silu(x):
    return x * _sigmoid(x)


def _log_sigmoid(x):
    return jnp.minimum(x, 0.0) - jnp.log(1.0 + jnp.exp(-jnp.abs(x)))


def _rms(x):
    return x * lax.rsqrt(jnp.mean(x * x, axis=-1, keepdims=True) + EPS)


def _rows_bcast(x, row, n):
    return jnp.broadcast_to(x[row:row + 1, :], (n, x.shape[1]))


def _cat(parts, axis):
    return parts[0] if len(parts) == 1 else jnp.concatenate(parts, axis=axis)


def _iota2(shape, dim):
    return lax.broadcasted_iota(jnp.int32, shape, dim)


def _blk(x, n):
    assert n & (n - 1) == 0
    return lax.shift_right_logical(x, jnp.int32(n.bit_length() - 1))


def _rem(x, n):
    assert n & (n - 1) == 0
    return x & jnp.int32(n - 1)


def _layer_spec(block, layer, tail):
    return pl.BlockSpec((None,) + block, lambda *g: (layer,) + tail(*g))


def _resident_spec(block, layer):
    zeros = (0,) * len(block)
    return pl.BlockSpec((None,) + block, lambda *g: (layer,) + zeros, pipeline_mode=pl.Buffered(1))


def _row_tiles(b, t, target):
    if t >= target:
        return 1, target
    return target // t, t


def _mod_spec(bb, d, layer, j):
    return pl.BlockSpec((None, bb, 1, d), lambda *g: (layer, g[0], 0, j))


def _ada_kernel(c_ref, w_ref, b_ref, o_ref):
    a = _silu(c_ref[...]).astype(BF16)
    o_ref[...] = _dot(a, w_ref[...].astype(BF16)) + b_ref[...]


def _ada_call(c_all, w_ada, b_ada):
    depth, d, n6 = w_ada.shape
    bp = c_all.shape[0]
    tn = _largest_tile(n6, 1024)
    return pl.pallas_call(
        _ada_kernel,
        grid=(depth, n6 // tn),
        in_specs=[
            pl.BlockSpec((bp, d), lambda l, j: (0, 0)),
            pl.BlockSpec((None, d, tn), lambda l, j: (l, 0, j)),
            pl.BlockSpec((None, 1, tn), lambda l, j: (l, 0, j)),
        ],
        out_specs=pl.BlockSpec((None, bp, tn), lambda l, j: (l, 0, j)),
        out_shape=jax.ShapeDtypeStruct((depth, bp, n6), F32),
        compiler_params=_cparams("arbitrary", "arbitrary"),
        name="ada_mod",
    )(c_all, w_ada, b_ada.reshape(depth, 1, n6))


def _pre_kernel(x_ref, g_ref, sc_ref, sh_ref, *rest):
    o_ref = rest[-1]
    x = x_ref[...]
    bb, tt, d = x.shape
    h = _rms(x) * g_ref[...] * (1.0 + sc_ref[...]) + sh_ref[...]
    o_ref[...] = h.reshape(bb * tt, d).astype(BF16)


def _pre_call(x3, mod4, layer, norm_g, m_total, row_off, prev, target=512):
    b, t, d = x3.shape
    bb, tt = _row_tiles(b, t, target)
    nt = t // tt
    off = row_off // (bb * tt)
    in_specs = [
        pl.BlockSpec((bb, tt, d), lambda i, j: (i, j, 0)),
        _layer_spec((1, d), layer, lambda i, j: (0, 0)),
        _mod_spec(bb, d, layer, 1),
        _mod_spec(bb, d, layer, 0),
    ]
    args = [x3, norm_g, mod4, mod4]
    aliases = {}
    if prev is not None:
        in_specs.append(pl.BlockSpec(memory_space=pl.ANY))
        args.append(prev)
        aliases = {4: 0}
    return pl.pallas_call(
        _pre_kernel,
        grid=(b // bb, nt),
        in_specs=in_specs,
        out_specs=pl.BlockSpec((bb * tt, d), lambda i, j: (off + i * nt + j, 0)),
        out_shape=jax.ShapeDtypeStruct((m_total, d), BF16),
        input_output_aliases=aliases,
        compiler_params=_cparams("arbitrary", "arbitrary"),
        name="prologue",
    )(*args)


def _mm_in_kernel(a_ref, w_ref, p_ref, wb_ref, *, acts, scale):
    j = pl.program_id(0)

    @pl.when(pl.program_id(1) == 0)
    def _():
        wb_ref[...] = w_ref[...].astype(BF16)

    fns = {"id": lambda y: y, "scale": lambda y: y * scale, "silu": _silu, "sigmoid": _sigmoid}
    for name in sorted(set(acts)):
        cond = functools.reduce(jnp.logical_or, [j == jj for jj, a in enumerate(acts) if a == name])

        @pl.when(cond)
        def _(fn=fns[name]):
            p_ref[...] = fn(_dot_nt(a_ref[...], wb_ref[...])).astype(BF16)


def _mm_in_call(h, w_in_t, layer, dqk, d_model, d_pool, tm=1024):
    m, k = h.shape
    starts = _p_columns(dqk, d_model, d_pool)
    n_main = starts[-1] + d_model
    tn = LANE
    for cand in range(LANE, 1024 + 1, LANE):
        if all(s % cand == 0 for s in starts + (n_main,)):
            tn = cand
    kinds = ("scale", "id", "id", "silu", "id", "sigmoid", "sigmoid")
    bounds = starts[1:] + (n_main,)
    acts = []
    for jj in range(n_main // tn):
        acts.append(kinds[next(i for i, e in enumerate(bounds) if jj * tn < e)])
    dk = dqk // N_HEADS
    return pl.pallas_call(
        functools.partial(_mm_in_kernel, acts=tuple(acts), scale=dk ** -0.5),
        grid=(n_main // tn, m // tm),
        in_specs=[
            pl.BlockSpec((tm, k), lambda j, i: (i, 0)),
            _layer_spec((tn, k), layer, lambda j, i: (j, 0)),
        ],
        out_specs=pl.BlockSpec((tm, tn), lambda j, i: (i, j)),
        out_shape=jax.ShapeDtypeStruct((m, n_main), BF16),
        scratch_shapes=[pltpu.VMEM((tn, k), BF16)],
        compiler_params=_cparams("arbitrary", "arbitrary"),
        name="in_proj",
    )(h, w_in_t)


def _mm_alr_kernel(a_ref, w_ref, o_ref):
    o_ref[...] = _dot_nt(a_ref[...], w_ref[...])


def _mm_alr_call(h, w_alr_t, layer, tm=1024):
    m, k = h.shape
    return pl.pallas_call(
        _mm_alr_kernel,
        grid=(m // tm,),
        in_specs=[
            pl.BlockSpec((tm, k), lambda i: (i, 0)),
            _layer_spec((LANE, k), layer, lambda i: (0, 0)),
        ],
        out_specs=pl.BlockSpec((tm, LANE), lambda i: (i, 0)),
        out_shape=jax.ShapeDtypeStruct((m, LANE), F32),
        compiler_params=_cparams("arbitrary"),
        name="gate_lowrank",
    )(h, w_alr_t)


def _gla_consts(t, seq):
    row = _iota2((t, t), 0)
    col = _iota2((t, t), 1)
    rowv = _iota2((t, 1), 0)
    nb = min(STABLE_BLOCK, seq)
    c = {"t": t, "seq": seq, "nb": nb}
    c["tri"] = ((col <= row) & (_blk(row, seq) == _blk(col, seq))).astype(BF16)
    c["mask0"] = (_blk(row, nb) == _blk(col, nb)) & (col <= row)
    levels = []
    n = nb * 2
    while n <= seq:
        levels.append((n, _rem(rowv, n) >= n // 2, _blk(row, n) == _blk(col, n)))
        n *= 2
    c["levels"] = levels
    return c


def _gla_factors(q, k, b, c):
    t, seq, nb = c["t"], c["seq"], c["nb"]
    dk = q.shape[1]
    parts = []
    for j in range(t // nb):
        start = j * nb
        parts.append(jnp.zeros((nb, dk), F32) if start % seq == 0 else _rows_bcast(b, start - 1, nb))
    e0 = b - _cat(parts, 0)
    qe0 = (q * jnp.exp(e0)).astype(BF16)
    ke0 = (k * jnp.exp(-e0)).astype(BF16)
    att = jnp.where(c["mask0"], _dot_nt(qe0, ke0), 0.0)
    for n, second, same in c["levels"]:
        half = n // 2
        x = b - _cat([_rows_bcast(b, j * n + half - 1, n) for j in range(t // n)], 0)
        f = jnp.exp(-jnp.abs(x))
        qk = jnp.where(second, q, k) * f
        ql = jnp.where(second, qk, 0.0).astype(BF16)
        kl = jnp.where(second, 0.0, qk).astype(BF16)
        att = att + jnp.where(same, _dot_nt(ql, kl), 0.0)
    b_last = _cat([_rows_bcast(b, (j + 1) * seq - 1, seq) for j in range(t // seq)], 0)
    qe = q * jnp.exp(b)
    kd = k * jnp.exp(b_last - b)
    return att, qe, kd


def _lane_tile(x, width):
    return _cat([x] * (width // x.shape[1]), 1)


def _pool_delta(u, u_parts, carry_parts, w, pos0, seq):
    t = u.shape[0]
    npiece = t // seq
    row = _iota2((t, t), 0)
    col = _iota2((t, t), 1)
    band = ((_blk(row, seq) == _blk(col, seq)) & (col <= row) & (col > row - w)).astype(BF16)
    rc = _iota2((t, npiece * CARRY_ROWS), 0)
    cc = _iota2((t, npiece * CARRY_ROWS), 1)
    cband = ((_blk(rc, seq) == _blk(cc, CARRY_ROWS))
             & (_rem(cc, CARRY_ROWS) > _rem(rc, seq) + CARRY_ROWS - w)).astype(BF16)
    s = _dot_parts(band, u_parts) + _dot_parts(cband, carry_parts)
    cnt = jnp.minimum(pos0 + 1, w).astype(F32)
    return s / cnt - u


def _mix_epilogue(o, silu_og, sig_ga, sig_gb, d, gn, wp, ps):
    y_a = _rms(o) * gn * silu_og.astype(F32)
    y_b = _dot(d.astype(BF16), wp) * ps
    return sig_ga.astype(F32) * y_a + sig_gb.astype(F32) * y_b


def _gate_log_decay(alr, wa2, ba):
    gk = _dot(alr.astype(BF16), wa2) + ba
    return _log_sigmoid(gk) * (1.0 / GATE_NORMALIZER)


def _p_columns(dqk, d_model, d_pool):
    q0, k0, v0 = 0, dqk, 2 * dqk
    og0 = v0 + d_model
    u0 = og0 + d_model
    ga0 = u0 + d_pool
    gb0 = ga0 + d_model
    return q0, k0, v0, og0, u0, ga0, gb0


def _mix_prompt_kernel(p_ref, alr_ref, wa2_ref, ba_ref, gn_ref, wp_ref, ps_ref,
                       m_ref, s_ref, pool_out_ref, car_ref, *, dqk, d_model):
    c = pl.program_id(1)
    t = p_ref.shape[0]
    groups, pin, dv = wp_ref.shape
    dk = dqk // N_HEADS
    q0, k0, v0, og0, u0, ga0, gb0 = _p_columns(dqk, d_model, groups * pin)

    @pl.when(c == 0)
    def _():
        s_ref[...] = jnp.zeros_like(s_ref)
        car_ref[...] = jnp.zeros_like(car_ref)

    consts = _gla_consts(t, t)
    loga = _gate_log_decay(alr_ref[...], wa2_ref[...], ba_ref[...])
    b_all = _dot_parts(consts["tri"], _split(loga, LOG_DECAY_TERMS))
    decay_all = jnp.exp(_column_tile(b_all[t - 1:t, :]))
    pos = c * t + _iota2((t, 1), 0)
    gn = gn_ref[...]

    for h in range(N_HEADS):
        q = p_ref[:, q0 + h * dk:q0 + (h + 1) * dk].astype(F32)
        k = p_ref[:, k0 + h * dk:k0 + (h + 1) * dk].astype(F32)
        vb = p_ref[:, v0 + h * dv:v0 + (h + 1) * dv]
        att, qe, kd = _gla_factors(q, k, b_all[:, h * dk:(h + 1) * dk], consts)
        s = s_ref[h]
        o = _dot(att.astype(BF16), vb) + _dot(qe.astype(BF16), s.astype(BF16))
        decay = decay_all[h * dk:(h + 1) * dk]
        s_ref[h] = _lane_tile(decay, dv) * s + _dot_tn(kd.astype(BF16), vb)

        ub = p_ref[:, u0 + h * pin:u0 + (h + 1) * pin]
        d = _pool_delta(ub.astype(F32), (ub,), (car_ref[:, h * pin:(h + 1) * pin],), POOL_WINDOWS[h], pos, t)
        merged = _mix_epilogue(o, p_ref[:, og0 + h * dv:og0 + (h + 1) * dv],
                               p_ref[:, ga0 + h * dv:ga0 + (h + 1) * dv],
                               p_ref[:, gb0 + h * dv:gb0 + (h + 1) * dv], d,
                               gn, wp_ref[h], ps_ref[:, h * dv:(h + 1) * dv])
        m_ref[:, h * dv:(h + 1) * dv] = merged.astype(BF16)

    u_tail = p_ref[t - CARRY_ROWS:, u0:u0 + groups * pin]
    car_ref[...] = u_tail

    @pl.when(c == pl.num_programs(1) - 1)
    def _():
        pool_out_ref[...] = u_tail.astype(F32)[CARRY_ROWS - POOL_BUF:]


def _mix_prompt_call(p, alr, batch, seq_len, layer, w_a2, b_a, gla_norm_g, w_pool, pool_scale, chunk=128):
    dqk = w_a2.shape[2]
    dk = dqk // N_HEADS
    groups, pin, dv = w_pool.shape[1:]
    d_model = pool_scale.shape[2]
    n_p = p.shape[1]
    nc = seq_len // chunk
    const = lambda b, c: (0, 0)
    in_specs = [
        pl.BlockSpec((chunk, n_p), lambda b, c: (b * nc + c, 0)),
        pl.BlockSpec((chunk, LANE), lambda b, c: (b * nc + c, 0)),
        _layer_spec((LANE, dqk), layer, const),
        _layer_spec((1, dqk), layer, const),
        _layer_spec((1, dv), layer, const),
        _layer_spec((groups, pin, dv), layer, lambda b, c: (0, 0, 0)),
        _layer_spec((1, d_model), layer, const),
    ]
    out_specs = [
        pl.BlockSpec((chunk, d_model), lambda b, c: (b * nc + c, 0)),
        pl.BlockSpec((None, N_HEADS, dk, dv), lambda b, c: (b, 0, 0, 0)),
        pl.BlockSpec((None, POOL_BUF, groups * pin), lambda b, c: (b, 0, 0)),
    ]
    out_shape = [
        jax.ShapeDtypeStruct((batch * seq_len, d_model), BF16),
        jax.ShapeDtypeStruct((batch, N_HEADS, dk, dv), F32),
        jax.ShapeDtypeStruct((batch, POOL_BUF, groups * pin), F32),
    ]
    return pl.pallas_call(
        functools.partial(_mix_prompt_kernel, dqk=dqk, d_model=d_model),
        grid=(batch, nc),
        in_specs=in_specs,
        out_specs=out_specs,
        out_shape=out_shape,
        scratch_shapes=[pltpu.VMEM((CARRY_ROWS, groups * pin), BF16)],
        compiler_params=_cparams("arbitrary", "arbitrary"),
        name="mix_prompt",
    )(p, alr, w_a2, b_a, gla_norm_g, w_pool, pool_scale)


def _mix_sample_kernel(p_ref, alr_ref, s0_ref, buf_ref, wa2_ref, ba_ref, gn_ref, wp_ref, ps_ref, *rest,
                       seq, dqk, d_model, n_alias):
    m_ref, s_out_ref, pool_out_ref, car_ref = rest[n_alias:]
    nseq = s0_ref.shape[0]
    t = nseq * seq
    groups, pin, dv = wp_ref.shape
    dk = dqk // N_HEADS
    q0, k0, v0, og0, u0, ga0, gb0 = _p_columns(dqk, d_model, groups * pin)

    car_ref[...] = jnp.zeros_like(car_ref)
    for j in range(nseq):
        car_ref[j * CARRY_ROWS + 1:(j + 1) * CARRY_ROWS, :] = buf_ref[j]

    consts = _gla_consts(t, seq)
    loga = _gate_log_decay(alr_ref[...], wa2_ref[...], ba_ref[...])
    b_all = _dot_parts(consts["tri"], _split(loga, LOG_DECAY_TERMS))
    decay_all = [jnp.exp(_column_tile(b_all[(j + 1) * seq - 1:(j + 1) * seq, :])) for j in range(nseq)]
    pos = PAST_LEN + _rem(_iota2((t, 1), 0), seq)
    keep = POOL_BUF - seq
    gn = gn_ref[...]

    u_all = p_ref[:, u0:u0 + groups * pin].astype(F32)
    for h in range(N_HEADS):
        q = p_ref[:, q0 + h * dk:q0 + (h + 1) * dk].astype(F32)
        k = p_ref[:, k0 + h * dk:k0 + (h + 1) * dk].astype(F32)
        vb = p_ref[:, v0 + h * dv:v0 + (h + 1) * dv]
        att, qe, kd = _gla_factors(q, k, b_all[:, h * dk:(h + 1) * dk], consts)
        kdb = kd.astype(BF16)
        qeb = qe.astype(BF16)
        o_intra = _dot(att.astype(BF16), vb)
        o_parts = []
        for j in range(nseq):
            s0 = s0_ref[j, h]
            lo, hi = j * seq, (j + 1) * seq
            o_parts.append(o_intra[lo:hi] + _dot(qeb[lo:hi], s0.astype(BF16)))
            decay = decay_all[j][h * dk:(h + 1) * dk]
            s_out_ref[j, h] = _lane_tile(decay, dv) * s0 + _dot_tn(kdb[lo:hi], vb[lo:hi])
        o = _cat(o_parts, 0)

        d = _pool_delta(u_all[:, h * pin:(h + 1) * pin], (p_ref[:, u0 + h * pin:u0 + (h + 1) * pin],),
                        _split(car_ref[:, h * pin:(h + 1) * pin], CARRY_TERMS), POOL_WINDOWS[h], pos, seq)
        merged = _mix_epilogue(o, p_ref[:, og0 + h * dv:og0 + (h + 1) * dv],
                               p_ref[:, ga0 + h * dv:ga0 + (h + 1) * dv],
                               p_ref[:, gb0 + h * dv:gb0 + (h + 1) * dv], d,
                               gn, wp_ref[h], ps_ref[:, h * dv:(h + 1) * dv])
        m_ref[:, h * dv:(h + 1) * dv] = merged.astype(BF16)

    for j in range(nseq):
        pool_out_ref[j, 0:keep, :] = buf_ref[j, seq:POOL_BUF, :]
        pool_out_ref[j, keep:POOL_BUF, :] = u_all[j * seq:(j + 1) * seq]


def _mix_sample_call(p, alr, row_off, seq, layer, state_gla, state_pool, w_a2, b_a, gla_norm_g, w_pool,
                     pool_scale, prev_gla, prev_pool, nseq=2):
    depth, bs, nh, dk, dv = state_gla.shape
    dqk = w_a2.shape[2]
    groups, pin, _ = w_pool.shape[1:]
    d_model = pool_scale.shape[2]
    n_p = p.shape[1]
    assert seq <= POOL_BUF and seq % 8 == 0
    t = nseq * seq
    off = row_off // t
    const = lambda i: (0, 0)
    in_specs = [
        pl.BlockSpec((t, n_p), lambda i: (off + i, 0)),
        pl.BlockSpec((t, LANE), lambda i: (off + i, 0)),
        pl.BlockSpec((None, nseq, nh, dk, dv), lambda i: (layer, i, 0, 0, 0)),
        pl.BlockSpec((None, nseq, POOL_BUF, groups * pin), lambda i: (layer, i, 0, 0)),
        _layer_spec((LANE, dqk), layer, const),
        _layer_spec((1, dqk), layer, const),
        _layer_spec((1, dv), layer, const),
        _layer_spec((groups, pin, dv), layer, lambda i: (0, 0, 0)),
        _layer_spec((1, d_model), layer, const),
    ]
    args = [p, alr, state_gla, state_pool, w_a2, b_a, gla_norm_g, w_pool, pool_scale]
    aliases = {}
    n_alias = 0
    if prev_gla is not None:
        in_specs += [pl.BlockSpec(memory_space=pl.ANY), pl.BlockSpec(memory_space=pl.ANY)]
        args += [prev_gla, prev_pool]
        aliases = {9: 1, 10: 2}
        n_alias = 2
    out_specs = [
        pl.BlockSpec((t, d_model), lambda i: (i, 0)),
        pl.BlockSpec((None, nseq, nh, dk, dv), lambda i: (layer, i, 0, 0, 0)),
        pl.BlockSpec((None, nseq, POOL_BUF, groups * pin), lambda i: (layer, i, 0, 0)),
    ]
    out_shape = [
        jax.ShapeDtypeStruct((bs * seq, d_model), BF16),
        jax.ShapeDtypeStruct(state_gla.shape, F32),
        jax.ShapeDtypeStruct(state_pool.shape, F32),
    ]
    return pl.pallas_call(
        functools.partial(_mix_sample_kernel, seq=seq, dqk=dqk, d_model=d_model, n_alias=n_alias),
        grid=(bs // nseq,),
        in_specs=in_specs,
        out_specs=out_specs,
        out_shape=out_shape,
        scratch_shapes=[pltpu.VMEM((nseq * CARRY_ROWS, groups * pin), F32)],
        input_output_aliases=aliases,
        compiler_params=_cparams("arbitrary"),
        name="mix_sample",
    )(*args)


def _mm_o_kernel(a_ref, w_ref, x_ref, g1_ref, ng_ref, sc_ref, sh_ref, *rest):
    x1_ref, h_ref = rest[-2:]
    bb, tt, d = x_ref.shape
    y = _dot(a_ref[...], w_ref[...]).reshape(bb, tt, d)
    x1 = x_ref[...] + g1_ref[...] * y
    x1_ref[...] = x1
    h = _rms(x1) * ng_ref[...] * (1.0 + sc_ref[...]) + sh_ref[...]
    h_ref[...] = h.reshape(bb * tt, d).astype(BF16)


def _mm_o_call(merged, x3, w_o, mod4, layer, norm2_g, m_total, row_off, prev, target=512):
    b, t, d = x3.shape
    bb, tt = _row_tiles(b, t, target)
    nt = t // tt
    rows = bb * tt
    off = row_off // rows
    in_specs = [
        pl.BlockSpec((rows, d), lambda i, j: (i * nt + j, 0)),
        _resident_spec((d, d), layer),
        pl.BlockSpec((bb, tt, d), lambda i, j: (i, j, 0)),
        _mod_spec(bb, d, layer, 2),
        _layer_spec((1, d), layer, lambda i, j: (0, 0)),
        _mod_spec(bb, d, layer, 4),
        _mod_spec(bb, d, layer, 3),
    ]
    args = [merged, w_o, x3, mod4, norm2_g, mod4, mod4]
    aliases = {}
    if prev is not None:
        in_specs.append(pl.BlockSpec(memory_space=pl.ANY))
        args.append(prev)
        aliases = {7: 1}
    return pl.pallas_call(
        _mm_o_kernel,
        grid=(b // bb, nt),
        in_specs=in_specs,
        out_specs=[
            pl.BlockSpec((bb, tt, d), lambda i, j: (i, j, 0)),
            pl.BlockSpec((rows, d), lambda i, j: (off + i * nt + j, 0)),
        ],
        out_shape=[jax.ShapeDtypeStruct((b, t, d), F32), jax.ShapeDtypeStruct((m_total, d), BF16)],
        input_output_aliases=aliases,
        compiler_params=_cparams("arbitrary", "arbitrary"),
        name="out_proj",
    )(*args)


def _mm_gu_kernel(a_ref, wg_ref, wu_ref, o_ref, wgb_ref, wub_ref):
    @pl.when(pl.program_id(1) == 0)
    def _():
        wgb_ref[...] = wg_ref[...].astype(BF16)
        wub_ref[...] = wu_ref[...].astype(BF16)

    a = a_ref[...]
    o_ref[...] = (_silu(_dot(a, wgb_ref[...])) * _dot(a, wub_ref[...])).astype(BF16)


def _mm_gu_call(h, w_gu, layer, tm=1024):
    m, k = h.shape
    dff = w_gu.shape[2] // 2
    tn = _largest_tile(dff, 512)
    nj = dff // tn
    return pl.pallas_call(
        _mm_gu_kernel,
        grid=(nj, m // tm),
        in_specs=[
            pl.BlockSpec((tm, k), lambda j, i: (i, 0)),
            _layer_spec((k, tn), layer, lambda j, i: (0, j)),
            _layer_spec((k, tn), layer, lambda j, i: (0, nj + j)),
        ],
        out_specs=pl.BlockSpec((tm, tn), lambda j, i: (i, j)),
        out_shape=jax.ShapeDtypeStruct((m, dff), BF16),
        scratch_shapes=[pltpu.VMEM((k, tn), BF16), pltpu.VMEM((k, tn), BF16)],
        compiler_params=_cparams("arbitrary", "arbitrary"),
        name="gate_up",
    )(h, w_gu, w_gu)


def _mm_down_kernel(a_ref, w_ref, x_ref, g2_ref, ng_ref, *rest, final):
    bb, tt, d = x_ref.shape
    x2 = x_ref[...] + g2_ref[...] * _dot(a_ref[...], w_ref[...]).reshape(bb, tt, d)
    if final:
        (y_ref,) = rest[-1:]
        y_ref[...] = _rms(x2) * ng_ref[...]
    else:
        sc_ref, sh_ref = rest[:2]
        x2_ref, h_ref = rest[-2:]
        x2_ref[...] = x2
        h = _rms(x2) * ng_ref[...] * (1.0 + sc_ref[...]) + sh_ref[...]
        h_ref[...] = h.reshape(bb * tt, d).astype(BF16)


def _mm_down_call(act, act_row_off, x3, w_down, mod4, layer, norm_g, norm_layer, final, m_total, row_off, prev,
                  target=256):
    b, t, d = x3.shape
    dff = w_down.shape[1]
    bb, tt = _row_tiles(b, t, target)
    nt = t // tt
    rows = bb * tt
    a_off = act_row_off // rows
    off = row_off // rows
    in_specs = [
        pl.BlockSpec((rows, dff), lambda i, j: (a_off + i * nt + j, 0)),
        _resident_spec((dff, d), layer),
        pl.BlockSpec((bb, tt, d), lambda i, j: (i, j, 0)),
        _mod_spec(bb, d, layer, 5),
        _layer_spec((1, d), norm_layer, lambda i, j: (0, 0)),
    ]
    args = [act, w_down, x3, mod4, norm_g]
    aliases = {}
    if final:
        out_specs = [pl.BlockSpec((bb, tt, d), lambda i, j: (i, j, 0))]
        out_shape = [jax.ShapeDtypeStruct((b, t, d), F32)]
    else:
        in_specs += [_mod_spec(bb, d, layer + 1, 1), _mod_spec(bb, d, layer + 1, 0)]
        args += [mod4, mod4]
        if prev is not None:
            in_specs.append(pl.BlockSpec(memory_space=pl.ANY))
            args.append(prev)
            aliases = {7: 1}
        out_specs = [
            pl.BlockSpec((bb, tt, d), lambda i, j: (i, j, 0)),
            pl.BlockSpec((rows, d), lambda i, j: (off + i * nt + j, 0)),
        ]
        out_shape = [jax.ShapeDtypeStruct((b, t, d), F32), jax.ShapeDtypeStruct((m_total, d), BF16)]
    return pl.pallas_call(
        functools.partial(_mm_down_kernel, final=final),
        grid=(b // bb, nt),
        in_specs=in_specs,
        out_specs=out_specs,
        out_shape=out_shape,
        input_output_aliases=aliases,
        compiler_params=_cparams("arbitrary", "arbitrary"),
        name="down_proj",
    )(*args)


def kernel(x_prompt, x_sample, state_gla, state_pool, c_prompt, c_sample, w_ada, b_ada, norm1_g, w_in, w_a2, b_a,
           gla_norm_g, w_pool, pool_scale, w_o, norm2_g, w_gu, w_down, final_norm_g):
    depth, d, n6 = w_ada.shape
    bp, lp, _ = x_prompt.shape
    bs, ls, _ = x_sample.shape
    mp, ms = bp * lp, bs * ls
    m_total = mp + ms
    rank, dqk = w_a2.shape[1:]
    n_main = w_in.shape[2] - rank

    c_all = jnp.concatenate([c_prompt, c_sample], axis=0)
    c_all = jnp.pad(c_all, ((0, (-c_all.shape[0]) % 8), (0, 0)))
    mod = _ada_call(c_all, w_ada, b_ada)
    mod_p = mod[:, :bp].reshape(depth, bp, 1, n6)
    mod_s = mod[:, bp:bp + bs].reshape(depth, bs, 1, n6)

    w_in_t = jnp.swapaxes(w_in, 1, 2)
    w_alr_b = jnp.pad(w_in_t[:, n_main:, :], ((0, 0), (0, LANE - rank), (0, 0))).astype(BF16)
    w_o_b = w_o.astype(BF16)
    w_down_b = w_down.astype(BF16)
    w_a2_b = jnp.pad(w_a2, ((0, 0), (0, LANE - rank), (0, 0))).astype(BF16)
    w_pool_b = w_pool.astype(BF16)
    n1 = norm1_g.reshape(depth, 1, d)
    n2 = norm2_g.reshape(depth, 1, d)
    nf = final_norm_g.reshape(1, 1, d)
    ba3 = b_a.reshape(depth, 1, dqk)
    gn3 = gla_norm_g.reshape(depth, 1, -1)
    ps3 = pool_scale.reshape(depth, 1, d)

    xp, xs = x_prompt, x_sample
    h = _pre_call(xp, mod_p, 0, n1, m_total, 0, None)
    h = _pre_call(xs, mod_s, 0, n1, m_total, mp, h)

    gla_p, pool_p = [], []
    gla_s = pool_s = None
    for l in range(depth):
        p = _mm_in_call(h, w_in_t, l, dqk, d, w_pool.shape[1] * w_pool.shape[2])
        alr = _mm_alr_call(h, w_alr_b, l)
        merged_p, s_p, b_p = _mix_prompt_call(p, alr, bp, lp, l, w_a2_b, ba3, gn3, w_pool_b, ps3)
        merged_s, gla_s, pool_s = _mix_sample_call(p, alr, mp, ls, l, state_gla, state_pool, w_a2_b, ba3, gn3,
                                                   w_pool_b, ps3, gla_s, pool_s)
        gla_p.append(s_p)
        pool_p.append(b_p)

        xp, h2 = _mm_o_call(merged_p, xp, w_o_b, mod_p, l, n2, m_total, 0, None)
        xs, h2 = _mm_o_call(merged_s, xs, w_o_b, mod_s, l, n2, m_total, mp, h2)
        act = _mm_gu_call(h2, w_gu, l)
        if l + 1 < depth:
            xp, h = _mm_down_call(act, 0, xp, w_down_b, mod_p, l, n1, l + 1, False, m_total, 0, None)
            xs, h = _mm_down_call(act, mp, xs, w_down_b, mod_s, l, n1, l + 1, False, m_total, mp, h)
        else:
            (yp,) = _mm_down_call(act, 0, xp, w_down_b, mod_p, l, nf, 0, True, m_total, 0, None)
            (ys,) = _mm_down_call(act, mp, xs, w_down_b, mod_s, l, nf, 0, True, m_total, mp, None)
    return yp, ys, jnp.stack(gla_p), jnp.stack(pool_p), gla_s, pool_s
```

```python
import functools

import jax
import jax.numpy as jnp
from jax import lax
from jax.experimental import pallas as pl
from jax.experimental.pallas import tpu as pltpu

F32 = jnp.float32
BF16 = jnp.bfloat16

N_HEADS = 4
POOL_WINDOWS = (2, 4, 8, 16)
POOL_BUF = 15
CARRY_ROWS = POOL_BUF + 1
GATE_NORMALIZER = 16.0
PAST_LEN = 16384
EPS = 1e-6
STABLE_BLOCK = 16
LOG_DECAY_TERMS = 2
CARRY_TERMS = 3
LANE = 128
MXU_DIM = 256
ROW_TILE_CAP = 1536
VMEM_LIMIT = 56 * 1024 * 1024


def _cparams(*sem):
    return pltpu.CompilerParams(dimension_semantics=sem, vmem_limit_bytes=VMEM_LIMIT)


def _largest_tile(n, cap):
    best = None
    for t in range(LANE, cap + 1, LANE):
        if n % t == 0:
            best = t
    assert best is not None, (n, cap)
    return best


def _largest_row_tile(m, cap):
    best = None
    for t in range(MXU_DIM, cap + 1, MXU_DIM):
        if m % t == 0:
            best = t
    assert best is not None, (m, cap)
    return best


def _dot(a, b):
    return jnp.dot(a, b, preferred_element_type=F32)


def _dot_nt(a, b):
    return lax.dot_general(a, b, (((1,), (1,)), ((), ())), preferred_element_type=F32)


def _dot_tn(a, b):
    return lax.dot_general(a, b, (((0,), (0,)), ((), ())), preferred_element_type=F32)


def _split(x, terms):
    parts = []
    for i in range(terms):
        p = x.astype(BF16)
        parts.append(p)
        if i + 1 < terms:
            x = x - p.astype(F32)
    return tuple(parts)


def _dot_parts(m, parts):
    out = _dot(m, parts[0])
    for p in parts[1:]:
        out = out + _dot(m, p)
    return out


def _column_tile(row):
    return jnp.broadcast_to(row, (LANE, row.shape[1])).T


def _sigmoid(x):
    return 1.0 / (1.0 + jnp.exp(-x))


def _silu(x):
    return x * _sigmoid(x)


def _log_sigmoid(x):
    return jnp.minimum(x, 0.0) - jnp.log(1.0 + jnp.exp(-jnp.abs(x)))


def _rms(x):
    return x * lax.rsqrt(jnp.mean(x * x, axis=-1, keepdims=True) + EPS)


def _rows_bcast(x, row, n):
    return jnp.broadcast_to(x[row:row + 1, :], (n, x.shape[1]))


def _cat(parts, axis):
    return parts[0] if len(parts) == 1 else jnp.concatenate(parts, axis=axis)


def _iota2(shape, dim):
    return lax.broadcasted_iota(jnp.int32, shape, dim)


def _blk(x, n):
    assert n & (n - 1) == 0
    return lax.shift_right_logical(x, jnp.int32(n.bit_length() - 1))


def _rem(x, n):
    assert n & (n - 1) == 0
    return x & jnp.int32(n - 1)


def _layer_spec(block, layer, tail):
    return pl.BlockSpec((None,) + block, lambda *g: (layer,) + tail(*g))


def _resident_spec(block, layer):
    zeros = (0,) * len(block)
    return pl.BlockSpec((None,) + block, lambda *g: (layer,) + zeros, pipeline_mode=pl.Buffered(1))


def _row_tiles(b, t, target):
    if t >= target:
        return 1, target
    return target // t, t


def _mod_spec(bb, d, layer, j):
    return pl.BlockSpec((None, bb, 1, d), lambda *g: (layer, g[0], 0, j))


def _ada_kernel(c_ref, w_ref, b_ref, o_ref):
    a = _silu(c_ref[...]).astype(BF16)
    o_ref[...] = _dot(a, w_ref[...].astype(BF16)) + b_ref[...]


def _ada_call(c_all, w_ada, b_ada):
    depth, d, n6 = w_ada.shape
    bp = c_all.shape[0]
    tn = _largest_tile(n6, 1024)
    return pl.pallas_call(
        _ada_kernel,
        grid=(depth, n6 // tn),
        in_specs=[
            pl.BlockSpec((bp, d), lambda l, j: (0, 0)),
            pl.BlockSpec((None, d, tn), lambda l, j: (l, 0, j)),
            pl.BlockSpec((None, 1, tn), lambda l, j: (l, 0, j)),
        ],
        out_specs=pl.BlockSpec((None, bp, tn), lambda l, j: (l, 0, j)),
        out_shape=jax.ShapeDtypeStruct((depth, bp, n6), F32),
        compiler_params=_cparams("arbitrary", "arbitrary"),
        name="ada_mod",
    )(c_all, w_ada, b_ada.reshape(depth, 1, n6))


def _pre_kernel(x_ref, g_ref, sc_ref, sh_ref, *rest):
    o_ref = rest[-1]
    x = x_ref[...]
    bb, tt, d = x.shape
    h = _rms(x) * g_ref[...] * (1.0 + sc_ref[...]) + sh_ref[...]
    o_ref[...] = h.reshape(bb * tt, d).astype(BF16)


def _pre_call(x3, mod4, layer, norm_g, m_total, row_off, prev, target=512):
    b, t, d = x3.shape
    bb, tt = _row_tiles(b, t, target)
    nt = t // tt
    off = row_off // (bb * tt)
    in_specs = [
        pl.BlockSpec((bb, tt, d), lambda i, j: (i, j, 0)),
        _layer_spec((1, d), layer, lambda i, j: (0, 0)),
        _mod_spec(bb, d, layer, 1),
        _mod_spec(bb, d, layer, 0),
    ]
    args = [x3, norm_g, mod4, mod4]
    aliases = {}
    if prev is not None:
        in_specs.append(pl.BlockSpec(memory_space=pl.ANY))
        args.append(prev)
        aliases = {4: 0}
    return pl.pallas_call(
        _pre_kernel,
        grid=(b // bb, nt),
        in_specs=in_specs,
        out_specs=pl.BlockSpec((bb * tt, d), lambda i, j: (off + i * nt + j, 0)),
        out_shape=jax.ShapeDtypeStruct((m_total, d), BF16),
        input_output_aliases=aliases,
        compiler_params=_cparams("arbitrary", "arbitrary"),
        name="prologue",
    )(*args)


def _mm_in_kernel(a_ref, w_ref, p_ref, wb_ref, *, acts, scale):
    j = pl.program_id(0)

    @pl.when(pl.program_id(1) == 0)
    def _():
        wb_ref[...] = w_ref[...].astype(BF16)

    fns = {"id": lambda y: y, "scale": lambda y: y * scale, "silu": _silu, "sigmoid": _sigmoid}
    for name in sorted(set(acts)):
        cond = functools.reduce(jnp.logical_or, [j == jj for jj, a in enumerate(acts) if a == name])

        @pl.when(cond)
        def _(fn=fns[name]):
            p_ref[...] = fn(_dot_nt(a_ref[...], wb_ref[...])).astype(BF16)


def _mm_in_call(h, w_in_t, layer, dqk, d_model, d_pool):
    m, k = h.shape
    tm = _largest_row_tile(m, ROW_TILE_CAP)
    starts = _p_columns(dqk, d_model, d_pool)
    n_main = starts[-1] + d_model
    tn = LANE
    for cand in range(LANE, 1024 + 1, LANE):
        if all(s % cand == 0 for s in starts + (n_main,)):
            tn = cand
    kinds = ("scale", "id", "id", "silu", "id", "sigmoid", "sigmoid")
    bounds = starts[1:] + (n_main,)
    acts = []
    for jj in range(n_main // tn):
        acts.append(kinds[next(i for i, e in enumerate(bounds) if jj * tn < e)])
    dk = dqk // N_HEADS
    return pl.pallas_call(
        functools.partial(_mm_in_kernel, acts=tuple(acts), scale=dk ** -0.5),
        grid=(n_main // tn, m // tm),
        in_specs=[
            pl.BlockSpec((tm, k), lambda j, i: (i, 0)),
            _layer_spec((tn, k), layer, lambda j, i: (j, 0)),
        ],
        out_specs=pl.BlockSpec((tm, tn), lambda j, i: (i, j)),
        out_shape=jax.ShapeDtypeStruct((m, n_main), BF16),
        scratch_shapes=[pltpu.VMEM((tn, k), BF16)],
        compiler_params=_cparams("arbitrary", "arbitrary"),
        name="in_proj",
    )(h, w_in_t)


def _mm_alr_kernel(a_ref, w_ref, o_ref):
    o_ref[...] = _dot_nt(a_ref[...], w_ref[...])


def _mm_alr_call(h, w_alr_t, layer, tm=1024):
    m, k = h.shape
    return pl.pallas_call(
        _mm_alr_kernel,
        grid=(m // tm,),
        in_specs=[
            pl.BlockSpec((tm, k), lambda i: (i, 0)),
            _layer_spec((LANE, k), layer, lambda i: (0, 0)),
        ],
        out_specs=pl.BlockSpec((tm, LANE), lambda i: (i, 0)),
        out_shape=jax.ShapeDtypeStruct((m, LANE), F32),
        compiler_params=_cparams("arbitrary"),
        name="gate_lowrank",
    )(h, w_alr_t)


def _gla_consts(t, seq):
    nb = min(STABLE_BLOCK, seq)
    nblk = t // nb
    levels = []
    n = nb * 2
    while n <= seq:
        levels.append(n)
        n *= 2
    nkinds = 3 + len(levels)
    rows = -(-nkinds * nblk // 16) * 16
    row = _iota2((t, t), 0)
    col = _iota2((t, t), 1)
    rowv = _iota2((t, 1), 0)
    same0 = _blk(row, nb) == _blk(col, nb)
    r = _iota2((rows, t), 0)
    u = _iota2((rows, t), 1)
    kind = _blk(r, nblk)
    j = _rem(r, nblk)
    start = j * nb
    seq_start = start & jnp.int32(-seq)
    sel = (kind == 0) & (u >= seq_start) & (u < start)
    sel |= (kind == 1) & (u >= start) & (u < seq_start + seq)
    for li, n in enumerate(levels):
        mid = (start & jnp.int32(-n)) + n // 2
        between = ((u >= mid) & (u < start)) | ((u >= start) & (u < mid))
        sel |= (kind == 2 + li) & between
    sel |= (kind == nkinds - 1) & (_blk(u, seq) == j)
    return {
        "t": t, "seq": seq, "nb": nb, "nblk": nblk, "total_kind": nkinds - 1,
        "tri_local": (same0 & (col <= row)).astype(BF16),
        "sel": jnp.where(sel, 1.0, 0.0).astype(BF16),
        "mask0": same0 & (col <= row),
        "levels": [(_rem(rowv, n) >= n // 2, _blk(row, n) == _blk(col, n)) for n in levels],
    }


def _gla_operands(q, k, e0, g, c):
    nb, nblk = c["nb"], c["nblk"]

    def per_block(kind):
        return _cat([_rows_bcast(g, kind * nblk + j, nb) for j in range(nblk)], 0)

    qe0 = q * jnp.exp(e0)
    ke0 = k * jnp.exp(-e0)
    pairs = [(qe0.astype(BF16), ke0.astype(BF16))]
    for li, (second, _) in enumerate(c["levels"]):
        f = per_block(2 + li)
        pairs.append((jnp.where(second, qe0 * f, 0.0).astype(BF16), jnp.where(second, 0.0, ke0 * f).astype(BF16)))
    qe = qe0 if nb == c["seq"] else qe0 * per_block(0)
    kd = ke0 * per_block(1)
    return pairs, qe.astype(BF16), kd.astype(BF16)


def _gla_scores(pairs, c):
    att = jnp.where(c["mask0"], _dot_nt(*pairs[0]), 0.0)
    for (ql, kl), (_, same) in zip(pairs[1:], c["levels"]):
        att = att + jnp.where(same, _dot_nt(ql, kl), 0.0)
    return att


def _lane_tile(x, width):
    return _cat([x] * (width // x.shape[1]), 1)


def _pool_delta(u, u_parts, carry_parts, w, pos0, seq):
    t = u.shape[0]
    npiece = t // seq
    row = _iota2((t, t), 0)
    col = _iota2((t, t), 1)
    band = ((_blk(row, seq) == _blk(col, seq)) & (col <= row) & (col > row - w)).astype(BF16)
    rc = _iota2((t, npiece * CARRY_ROWS), 0)
    cc = _iota2((t, npiece * CARRY_ROWS), 1)
    cband = ((_blk(rc, seq) == _blk(cc, CARRY_ROWS))
             & (_rem(cc, CARRY_ROWS) > _rem(rc, seq) + CARRY_ROWS - w)).astype(BF16)
    s = _dot_parts(band, u_parts) + _dot_parts(cband, carry_parts)
    cnt = jnp.minimum(pos0 + 1, w).astype(F32)
    return s / cnt - u


def _mix_epilogue(o, silu_og, sig_ga, sig_gb, d, gn, wp, ps):
    y_a = _rms(o) * gn * silu_og.astype(F32)
    y_b = _dot(d.astype(BF16), wp) * ps
    return sig_ga.astype(F32) * y_a + sig_gb.astype(F32) * y_b


def _gate_log_decay(alr, wa2, ba):
    gk = _dot(alr.astype(BF16), wa2) + ba
    return _log_sigmoid(gk) * (1.0 / GATE_NORMALIZER)


def _p_columns(dqk, d_model, d_pool):
    q0, k0, v0 = 0, dqk, 2 * dqk
    og0 = v0 + d_model
    u0 = og0 + d_model
    ga0 = u0 + d_pool
    gb0 = ga0 + d_model
    return q0, k0, v0, og0, u0, ga0, gb0


def _mix_prompt_kernel(p_ref, alr_ref, wa2_ref, ba_ref, gn_ref, wp_ref, ps_ref,
                       m_ref, s_ref, pool_out_ref, car_ref, *, dqk, d_model):
    c = pl.program_id(1)
    t = p_ref.shape[0]
    groups, pin, dv = wp_ref.shape
    dk = dqk // N_HEADS
    q0, k0, v0, og0, u0, ga0, gb0 = _p_columns(dqk, d_model, groups * pin)

    @pl.when(c == 0)
    def _():
        s_ref[...] = jnp.zeros_like(s_ref)
        car_ref[...] = jnp.zeros_like(car_ref)

    consts = _gla_consts(t, t)
    loga = _split(_gate_log_decay(alr_ref[...], wa2_ref[...], ba_ref[...]), LOG_DECAY_TERMS)
    e0_all = _dot_parts(consts["tri_local"], loga)
    g_all = jnp.exp(_dot_parts(consts["sel"], loga))
    tot = consts["total_kind"] * consts["nblk"]
    decay_all = _column_tile(g_all[tot:tot + 1, :])
    pos = c * t + _iota2((t, 1), 0)
    gn = gn_ref[...]

    heads = range(N_HEADS)
    ops = []
    for h in heads:
        q = p_ref[:, q0 + h * dk:q0 + (h + 1) * dk].astype(F32)
        k = p_ref[:, k0 + h * dk:k0 + (h + 1) * dk].astype(F32)
        ops.append(_gla_operands(q, k, e0_all[:, h * dk:(h + 1) * dk], g_all[:, h * dk:(h + 1) * dk], consts))
    atts = [_gla_scores(ops[h][0], consts).astype(BF16) for h in heads]
    outs = []
    for h in heads:
        _, qe, kd = ops[h]
        vb = p_ref[:, v0 + h * dv:v0 + (h + 1) * dv]
        s = s_ref[h]
        outs.append(_dot(atts[h], vb) + _dot(qe, s.astype(BF16)))
        decay = decay_all[h * dk:(h + 1) * dk]
        s_ref[h] = _lane_tile(decay, dv) * s + _dot_tn(kd, vb)

    for h in heads:
        o = outs[h]
        ub = p_ref[:, u0 + h * pin:u0 + (h + 1) * pin]
        d = _pool_delta(ub.astype(F32), (ub,), (car_ref[:, h * pin:(h + 1) * pin],), POOL_WINDOWS[h], pos, t)
        merged = _mix_epilogue(o, p_ref[:, og0 + h * dv:og0 + (h + 1) * dv],
                               p_ref[:, ga0 + h * dv:ga0 + (h + 1) * dv],
                               p_ref[:, gb0 + h * dv:gb0 + (h + 1) * dv], d,
                               gn, wp_ref[h], ps_ref[:, h * dv:(h + 1) * dv])
        m_ref[:, h * dv:(h + 1) * dv] = merged.astype(BF16)

    u_tail = p_ref[t - CARRY_ROWS:, u0:u0 + groups * pin]
    car_ref[...] = u_tail

    @pl.when(c == pl.num_programs(1) - 1)
    def _():
        pool_out_ref[...] = u_tail.astype(F32)[CARRY_ROWS - POOL_BUF:]


def _mix_prompt_call(p, alr, batch, seq_len, layer, w_a2, b_a, gla_norm_g, w_pool, pool_scale, chunk=128):
    dqk = w_a2.shape[2]
    dk = dqk // N_HEADS
    groups, pin, dv = w_pool.shape[1:]
    d_model = pool_scale.shape[2]
    n_p = p.shape[1]
    nc = seq_len // chunk
    const = lambda b, c: (0, 0)
    in_specs = [
        pl.BlockSpec((chunk, n_p), lambda b, c: (b * nc + c, 0)),
        pl.BlockSpec((chunk, LANE), lambda b, c: (b * nc + c, 0)),
        _layer_spec((LANE, dqk), layer, const),
        _layer_spec((1, dqk), layer, const),
        _layer_spec((1, dv), layer, const),
        _layer_spec((groups, pin, dv), layer, lambda b, c: (0, 0, 0)),
        _layer_spec((1, d_model), layer, const),
    ]
    out_specs = [
        pl.BlockSpec((chunk, d_model), lambda b, c: (b * nc + c, 0)),
        pl.BlockSpec((None, N_HEADS, dk, dv), lambda b, c: (b, 0, 0, 0)),
        pl.BlockSpec((None, POOL_BUF, groups * pin), lambda b, c: (b, 0, 0)),
    ]
    out_shape = [
        jax.ShapeDtypeStruct((batch * seq_len, d_model), BF16),
        jax.ShapeDtypeStruct((batch, N_HEADS, dk, dv), F32),
        jax.ShapeDtypeStruct((batch, POOL_BUF, groups * pin), F32),
    ]
    return pl.pallas_call(
        functools.partial(_mix_prompt_kernel, dqk=dqk, d_model=d_model),
        grid=(batch, nc),
        in_specs=in_specs,
        out_specs=out_specs,
        out_shape=out_shape,
        scratch_shapes=[pltpu.VMEM((CARRY_ROWS, groups * pin), BF16)],
        compiler_params=_cparams("arbitrary", "arbitrary"),
        name="mix_prompt",
    )(p, alr, w_a2, b_a, gla_norm_g, w_pool, pool_scale)


def _mix_sample_kernel(p_ref, alr_ref, s0_ref, buf_ref, wa2_ref, ba_ref, gn_ref, wp_ref, ps_ref, *rest,
                       seq, dqk, d_model, n_alias):
    m_ref, s_out_ref, pool_out_ref, car_ref = rest[n_alias:]
    nseq = s0_ref.shape[0]
    t = nseq * seq
    groups, pin, dv = wp_ref.shape
    dk = dqk // N_HEADS
    q0, k0, v0, og0, u0, ga0, gb0 = _p_columns(dqk, d_model, groups * pin)

    car_ref[...] = jnp.zeros_like(car_ref)
    for j in range(nseq):
        car_ref[j * CARRY_ROWS + 1:(j + 1) * CARRY_ROWS, :] = buf_ref[j]

    consts = _gla_consts(t, seq)
    loga = _split(_gate_log_decay(alr_ref[...], wa2_ref[...], ba_ref[...]), LOG_DECAY_TERMS)
    e0_all = _dot_parts(consts["tri_local"], loga)
    g_all = jnp.exp(_dot_parts(consts["sel"], loga))
    tot = consts["total_kind"] * consts["nblk"]
    decay_all = [_column_tile(g_all[tot + j:tot + j + 1, :]) for j in range(nseq)]
    pos = PAST_LEN + _rem(_iota2((t, 1), 0), seq)
    keep = POOL_BUF - seq
    gn = gn_ref[...]

    u_all = p_ref[:, u0:u0 + groups * pin].astype(F32)
    heads = range(N_HEADS)
    ops = []
    for h in heads:
        q = p_ref[:, q0 + h * dk:q0 + (h + 1) * dk].astype(F32)
        k = p_ref[:, k0 + h * dk:k0 + (h + 1) * dk].astype(F32)
        ops.append(_gla_operands(q, k, e0_all[:, h * dk:(h + 1) * dk], g_all[:, h * dk:(h + 1) * dk], consts))
    atts = [_gla_scores(ops[h][0], consts).astype(BF16) for h in heads]
    outs = []
    for h in heads:
        _, qeb, kdb = ops[h]
        vb = p_ref[:, v0 + h * dv:v0 + (h + 1) * dv]
        o_intra = _dot(atts[h], vb)
        o_parts = []
        for j in range(nseq):
            s0 = s0_ref[j, h]
            lo, hi = j * seq, (j + 1) * seq
            o_parts.append(o_intra[lo:hi] + _dot(qeb[lo:hi], s0.astype(BF16)))
            decay = decay_all[j][h * dk:(h + 1) * dk]
            s_out_ref[j, h] = _lane_tile(decay, dv) * s0 + _dot_tn(kdb[lo:hi], vb[lo:hi])
        outs.append(_cat(o_parts, 0))

    for h in heads:
        o = outs[h]
        d = _pool_delta(u_all[:, h * pin:(h + 1) * pin], (p_ref[:, u0 + h * pin:u0 + (h + 1) * pin],),
                        _split(car_ref[:, h * pin:(h + 1) * pin], CARRY_TERMS), POOL_WINDOWS[h], pos, seq)
        merged = _mix_epilogue(o, p_ref[:, og0 + h * dv:og0 + (h + 1) * dv],
                               p_ref[:, ga0 + h * dv:ga0 + (h + 1) * dv],
                               p_ref[:, gb0 + h * dv:gb0 + (h + 1) * dv], d,
                               gn, wp_ref[h], ps_ref[:, h * dv:(h + 1) * dv])
        m_ref[:, h * dv:(h + 1) * dv] = merged.astype(BF16)

    for j in range(nseq):
        pool_out_ref[j, 0:keep, :] = buf_ref[j, seq:POOL_BUF, :]
        pool_out_ref[j, keep:POOL_BUF, :] = u_all[j * seq:(j + 1) * seq]


def _mix_sample_call(p, alr, row_off, seq, layer, state_gla, state_pool, w_a2, b_a, gla_norm_g, w_pool,
                     pool_scale, prev_gla, prev_pool, nseq=4):
    depth, bs, nh, dk, dv = state_gla.shape
    dqk = w_a2.shape[2]
    groups, pin, _ = w_pool.shape[1:]
    d_model = pool_scale.shape[2]
    n_p = p.shape[1]
    assert seq <= POOL_BUF and seq % 8 == 0
    t = nseq * seq
    off = row_off // t
    const = lambda i: (0, 0)
    in_specs = [
        pl.BlockSpec((t, n_p), lambda i: (off + i, 0)),
        pl.BlockSpec((t, LANE), lambda i: (off + i, 0)),
        pl.BlockSpec((None, nseq, nh, dk, dv), lambda i: (layer, i, 0, 0, 0)),
        pl.BlockSpec((None, nseq, POOL_BUF, groups * pin), lambda i: (layer, i, 0, 0)),
        _layer_spec((LANE, dqk), layer, const),
        _layer_spec((1, dqk), layer, const),
        _layer_spec((1, dv), layer, const),
        _layer_spec((groups, pin, dv), layer, lambda i: (0, 0, 0)),
        _layer_spec((1, d_model), layer, const),
    ]
    args = [p, alr, state_gla, state_pool, w_a2, b_a, gla_norm_g, w_pool, pool_scale]
    aliases = {}
    n_alias = 0
    if prev_gla is not None:
        in_specs += [pl.BlockSpec(memory_space=pl.ANY), pl.BlockSpec(memory_space=pl.ANY)]
        args += [prev_gla, prev_pool]
        aliases = {9: 1, 10: 2}
        n_alias = 2
    out_specs = [
        pl.BlockSpec((t, d_model), lambda i: (i, 0)),
        pl.BlockSpec((None, nseq, nh, dk, dv), lambda i: (layer, i, 0, 0, 0)),
        pl.BlockSpec((None, nseq, POOL_BUF, groups * pin), lambda i: (layer, i, 0, 0)),
    ]
    out_shape = [
        jax.ShapeDtypeStruct((bs * seq, d_model), BF16),
        jax.ShapeDtypeStruct(state_gla.shape, F32),
        jax.ShapeDtypeStruct(state_pool.shape, F32),
    ]
    return pl.pallas_call(
        functools.partial(_mix_sample_kernel, seq=seq, dqk=dqk, d_model=d_model, n_alias=n_alias),
        grid=(bs // nseq,),
        in_specs=in_specs,
        out_specs=out_specs,
        out_shape=out_shape,
        scratch_shapes=[pltpu.VMEM((nseq * CARRY_ROWS, groups * pin), F32)],
        input_output_aliases=aliases,
        compiler_params=_cparams("arbitrary"),
        name="mix_sample",
    )(*args)


def _mm_o_kernel(a_ref, w_ref, x_ref, g1_ref, ng_ref, sc_ref, sh_ref, *rest):
    x1_ref, h_ref = rest[-2:]
    bb, tt, d = x_ref.shape
    y = _dot(a_ref[...], w_ref[...]).reshape(bb, tt, d)
    x1 = x_ref[...] + g1_ref[...] * y
    x1_ref[...] = x1
    h = _rms(x1) * ng_ref[...] * (1.0 + sc_ref[...]) + sh_ref[...]
    h_ref[...] = h.reshape(bb * tt, d).astype(BF16)


def _mm_o_call(merged, x3, w_o, mod4, layer, norm2_g, m_total, row_off, prev, target=512):
    b, t, d = x3.shape
    bb, tt = _row_tiles(b, t, target)
    nt = t // tt
    rows = bb * tt
    off = row_off // rows
    in_specs = [
        pl.BlockSpec((rows, d), lambda i, j: (i * nt + j, 0)),
        _resident_spec((d, d), layer),
        pl.BlockSpec((bb, tt, d), lambda i, j: (i, j, 0)),
        _mod_spec(bb, d, layer, 2),
        _layer_spec((1, d), layer, lambda i, j: (0, 0)),
        _mod_spec(bb, d, layer, 4),
        _mod_spec(bb, d, layer, 3),
    ]
    args = [merged, w_o, x3, mod4, norm2_g, mod4, mod4]
    aliases = {}
    if prev is not None:
        in_specs.append(pl.BlockSpec(memory_space=pl.ANY))
        args.append(prev)
        aliases = {7: 1}
    return pl.pallas_call(
        _mm_o_kernel,
        grid=(b // bb, nt),
        in_specs=in_specs,
        out_specs=[
            pl.BlockSpec((bb, tt, d), lambda i, j: (i, j, 0)),
            pl.BlockSpec((rows, d), lambda i, j: (off + i * nt + j, 0)),
        ],
        out_shape=[jax.ShapeDtypeStruct((b, t, d), F32), jax.ShapeDtypeStruct((m_total, d), BF16)],
        input_output_aliases=aliases,
        compiler_params=_cparams("arbitrary", "arbitrary"),
        name="out_proj",
    )(*args)


def _mm_gu_kernel(a_ref, wg_ref, wu_ref, o_ref, wgb_ref, wub_ref):
    @pl.when(pl.program_id(1) == 0)
    def _():
        wgb_ref[...] = wg_ref[...].astype(BF16)
        wub_ref[...] = wu_ref[...].astype(BF16)

    a = a_ref[...]
    o_ref[...] = (_silu(_dot(a, wgb_ref[...])) * _dot(a, wub_ref[...])).astype(BF16)


def _mm_gu_call(h, w_gu, layer):
    m, k = h.shape
    tm = _largest_row_tile(m, ROW_TILE_CAP)
    dff = w_gu.shape[2] // 2
    tn = _largest_tile(dff, 512)
    nj = dff // tn
    return pl.pallas_call(
        _mm_gu_kernel,
        grid=(nj, m // tm),
        in_specs=[
            pl.BlockSpec((tm, k), lambda j, i: (i, 0)),
            _layer_spec((k, tn), layer, lambda j, i: (0, j)),
            _layer_spec((k, tn), layer, lambda j, i: (0, nj + j)),
        ],
        out_specs=pl.BlockSpec((tm, tn), lambda j, i: (i, j)),
        out_shape=jax.ShapeDtypeStruct((m, dff), BF16),
        scratch_shapes=[pltpu.VMEM((k, tn), BF16), pltpu.VMEM((k, tn), BF16)],
        compiler_params=_cparams("arbitrary", "arbitrary"),
        name="gate_up",
    )(h, w_gu, w_gu)


def _mm_down_kernel(a_ref, w_ref, x_ref, g2_ref, ng_ref, *rest, final):
    bb, tt, d = x_ref.shape
    x2 = x_ref[...] + g2_ref[...] * _dot(a_ref[...], w_ref[...]).reshape(bb, tt, d)
    if final:
        (y_ref,) = rest[-1:]
        y_ref[...] = _rms(x2) * ng_ref[...]
    else:
        sc_ref, sh_ref = rest[:2]
        x2_ref, h_ref = rest[-2:]
        x2_ref[...] = x2
        h = _rms(x2) * ng_ref[...] * (1.0 + sc_ref[...]) + sh_ref[...]
        h_ref[...] = h.reshape(bb * tt, d).astype(BF16)


def _mm_down_call(act, act_row_off, x3, w_down, mod4, layer, norm_g, norm_layer, final, m_total, row_off, prev,
                  target=256):
    b, t, d = x3.shape
    dff = w_down.shape[1]
    bb, tt = _row_tiles(b, t, target)
    nt = t // tt
    rows = bb * tt
    a_off = act_row_off // rows
    off = row_off // rows
    in_specs = [
        pl.BlockSpec((rows, dff), lambda i, j: (a_off + i * nt + j, 0)),
        _resident_spec((dff, d), layer),
        pl.BlockSpec((bb, tt, d), lambda i, j: (i, j, 0)),
        _mod_spec(bb, d, layer, 5),
        _layer_spec((1, d), norm_layer, lambda i, j: (0, 0)),
    ]
    args = [act, w_down, x3, mod4, norm_g]
    aliases = {}
    if final:
        out_specs = [pl.BlockSpec((bb, tt, d), lambda i, j: (i, j, 0))]
        out_shape = [jax.ShapeDtypeStruct((b, t, d), F32)]
    else:
        in_specs += [_mod_spec(bb, d, layer + 1, 1), _mod_spec(bb, d, layer + 1, 0)]
        args += [mod4, mod4]
        if prev is not None:
            in_specs.append(pl.BlockSpec(memory_space=pl.ANY))
            args.append(prev)
            aliases = {7: 1}
        out_specs = [
            pl.BlockSpec((bb, tt, d), lambda i, j: (i, j, 0)),
            pl.BlockSpec((rows, d), lambda i, j: (off + i * nt + j, 0)),
        ]
        out_shape = [jax.ShapeDtypeStruct((b, t, d), F32), jax.ShapeDtypeStruct((m_total, d), BF16)]
    return pl.pallas_call(
        functools.partial(_mm_down_kernel, final=final),
        grid=(b // bb, nt),
        in_specs=in_specs,
        out_specs=out_specs,
        out_shape=out_shape,
        input_output_aliases=aliases,
        compiler_params=_cparams("arbitrary", "arbitrary"),
        name="down_proj",
    )(*args)


def kernel(x_prompt, x_sample, state_gla, state_pool, c_prompt, c_sample, w_ada, b_ada, norm1_g, w_in, w_a2, b_a,
           gla_norm_g, w_pool, pool_scale, w_o, norm2_g, w_gu, w_down, final_norm_g):
    depth, d, n6 = w_ada.shape
    bp, lp, _ = x_prompt.shape
    bs, ls, _ = x_sample.shape
    mp, ms = bp * lp, bs * ls
    m_total = mp + ms
    rank, dqk = w_a2.shape[1:]
    n_main = w_in.shape[2] - rank

    c_all = jnp.concatenate([c_prompt, c_sample], axis=0)
    c_all = jnp.pad(c_all, ((0, (-c_all.shape[0]) % 8), (0, 0)))
    mod = _ada_call(c_all, w_ada, b_ada)
    mod_p = mod[:, :bp].reshape(depth, bp, 1, n6)
    mod_s = mod[:, bp:bp + bs].reshape(depth, bs, 1, n6)

    w_in_t = jnp.swapaxes(w_in, 1, 2)
    w_alr_b = jnp.pad(w_in_t[:, n_main:, :], ((0, 0), (0, LANE - rank), (0, 0))).astype(BF16)
    w_o_b = w_o.astype(BF16)
    w_down_b = w_down.astype(BF16)
    w_a2_b = jnp.pad(w_a2, ((0, 0), (0, LANE - rank), (0, 0))).astype(BF16)
    w_pool_b = w_pool.astype(BF16)
    n1 = norm1_g.reshape(depth, 1, d)
    n2 = norm2_g.reshape(depth, 1, d)
    nf = final_norm_g.reshape(1, 1, d)
    ba3 = b_a.reshape(depth, 1, dqk)
    gn3 = gla_norm_g.reshape(depth, 1, -1)
    ps3 = pool_scale.reshape(depth, 1, d)

    xp, xs = x_prompt, x_sample
    h = _pre_call(xp, mod_p, 0, n1, m_total, 0, None)
    h = _pre_call(xs, mod_s, 0, n1, m_total, mp, h)

    gla_p, pool_p = [], []
    gla_s = pool_s = None
    for l in range(depth):
        p = _mm_in_call(h, w_in_t, l, dqk, d, w_pool.shape[1] * w_pool.shape[2])
        alr = _mm_alr_call(h, w_alr_b, l)
        merged_p, s_p, b_p = _mix_prompt_call(p, alr, bp, lp, l, w_a2_b, ba3, gn3, w_pool_b, ps3)
        merged_s, gla_s, pool_s = _mix_sample_call(p, alr, mp, ls, l, state_gla, state_pool, w_a2_b, ba3, gn3,
                                                   w_pool_b, ps3, gla_s, pool_s)
        gla_p.append(s_p)
        pool_p.append(b_p)

        xp, h2 = _mm_o_call(merged_p, xp, w_o_b, mod_p, l, n2, m_total, 0, None)
        xs, h2 = _mm_o_call(merged_s, xs, w_o_b, mod_s, l, n2, m_total, mp, h2)
        act = _mm_gu_call(h2, w_gu, l)
        if l + 1 < depth:
            xp, h = _mm_down_call(act, 0, xp, w_down_b, mod_p, l, n1, l + 1, False, m_total, 0, None)
            xs, h = _mm_down_call(act, mp, xs, w_down_b, mod_s, l, n1, l + 1, False, m_total, mp, h)
        else:
            (yp,) = _mm_down_call(act, 0, xp, w_down_b, mod_p, l, nf, 0, True, m_total, 0, None)
            (ys,) = _mm_down_call(act, mp, xs, w_down_b, mod_s, l, nf, 0, True, m_total, mp, None)
    return yp, ys, jnp.stack(gla_p), jnp.stack(pool_p), gla_s, pool_s
```

```python
import functools

import jax
import jax.numpy as jnp
from jax import lax
from jax.experimental import pallas as pl
from jax.experimental.pallas import tpu as pltpu

F32 = jnp.float32
BF16 = jnp.bfloat16

N_HEADS = 4
POOL_WINDOWS = (2, 4, 8, 16)
POOL_BUF = 15
CARRY_ROWS = POOL_BUF + 1
GATE_NORMALIZER = 16.0
PAST_LEN = 16384
EPS = 1e-6
STABLE_BLOCK = 16
LOG_DECAY_TERMS = 2
CARRY_TERMS = 3
LANE = 128
MXU_DIM = 256
ROW_TILE_CAP = 1024
VMEM_LIMIT = 56 * 1024 * 1024


def _cparams(*sem):
    return pltpu.CompilerParams(dimension_semantics=sem, vmem_limit_bytes=VMEM_LIMIT)


def _largest_tile(n, cap):
    best = None
    for t in range(LANE, cap + 1, LANE):
        if n % t == 0:
            best = t
    assert best is not None, (n, cap)
    return best


def _largest_row_tile(m, cap):
    best = None
    for t in range(MXU_DIM, cap + 1, MXU_DIM):
        if m % t == 0:
            best = t
    assert best is not None, (m, cap)
    return best


def _dot(a, b):
    return jnp.dot(a, b, preferred_element_type=F32)


def _dot_nt(a, b):
    return lax.dot_general(a, b, (((1,), (1,)), ((), ())), preferred_element_type=F32)


def _dot_tn(a, b):
    return lax.dot_general(a, b, (((0,), (0,)), ((), ())), preferred_element_type=F32)


def _split(x, terms):
    parts = []
    for i in range(terms):
        p = x.astype(BF16)
        parts.append(p)
        if i + 1 < terms:
            x = x - p.astype(F32)
    return tuple(parts)


def _dot_parts(m, parts):
    out = _dot(m, parts[0])
    for p in parts[1:]:
        out = out + _dot(m, p)
    return out


def _column_tile(row):
    return jnp.broadcast_to(row, (LANE, row.shape[1])).T


def _sigmoid(x):
    return 1.0 / (1.0 + jnp.exp(-x))


def _silu(x):
    return x * _sigmoid(x)


def _log_sigmoid(x):
    return jnp.minimum(x, 0.0) - jnp.log(1.0 + jnp.exp(-jnp.abs(x)))


def _rms(x):
    return x * lax.rsqrt(jnp.mean(x * x, axis=-1, keepdims=True) + EPS)


def _rows_bcast(x, row, n):
    return jnp.broadcast_to(x[row:row + 1, :], (n, x.shape[1]))


def _cat(parts, axis):
    return parts[0] if len(parts) == 1 else jnp.concatenate(parts, axis=axis)


def _iota2(shape, dim):
    return lax.broadcasted_iota(jnp.int32, shape, dim)


def _blk(x, n):
    assert n & (n - 1) == 0
    return lax.shift_right_logical(x, jnp.int32(n.bit_length() - 1))


def _rem(x, n):
    assert n & (n - 1) == 0
    return x & jnp.int32(n - 1)


def _layer_spec(block, layer, tail):
    return pl.BlockSpec((None,) + block, lambda *g: (layer,) + tail(*g))


def _resident_spec(block, layer):
    zeros = (0,) * len(block)
    return pl.BlockSpec((None,) + block, lambda *g: (layer,) + zeros, pipeline_mode=pl.Buffered(1))


class _RowGrid:
    def __init__(self, xp_shape, xs_shape, rows):
        bp, lp, d = xp_shape
        bs, ls, _ = xs_shape
        assert lp % rows == 0 and rows % ls == 0 and bs % (rows // ls) == 0
        self.rows, self.d = rows, d
        self.tiles_per_seq = lp // rows
        self.n_p = bp * self.tiles_per_seq
        self.seqs_per_tile = rows // ls
        self.n_s = bs // self.seqs_per_tile
        self.ls = ls
        self.grid = (self.n_p + self.n_s,)

    def _ip(self, i):
        return jnp.minimum(i, self.n_p - 1)

    def _is(self, i):
        return jnp.maximum(i - self.n_p, 0)

    def x_specs(self):
        tps = self.tiles_per_seq
        return (pl.BlockSpec((1, self.rows, self.d), lambda i: (self._ip(i) // tps, self._ip(i) % tps, 0)),
                pl.BlockSpec((self.seqs_per_tile, self.ls, self.d), lambda i: (self._is(i), 0, 0)))

    def mod_specs(self, layer, j):
        tps = self.tiles_per_seq
        return (pl.BlockSpec((None, None, 1, self.d), lambda i: (layer, self._ip(i) // tps, 0, j)),
                pl.BlockSpec((None, self.seqs_per_tile, self.d), lambda i: (layer, self._is(i), j)))

    def rows_specs(self, width):
        return (pl.BlockSpec((self.rows, width), lambda i: (self._ip(i), 0)),
                pl.BlockSpec((self.rows, width), lambda i: (self._is(i), 0)))

    def joint_spec(self, width):
        return pl.BlockSpec((self.rows, width), lambda i: (i, 0))

    def run(self, body, prompt_refs, sample_refs):
        i = pl.program_id(0)

        @pl.when(i < self.n_p)
        def _():
            body(*prompt_refs)

        @pl.when(i >= self.n_p)
        def _():
            body(*sample_refs)


def _rows2d(x_ref):
    bb, tt, d = x_ref.shape
    return x_ref[...].reshape(bb * tt, d)


def _per_row(mod_ref, rows):
    m = mod_ref[...]
    n = m.shape[0]
    if n == 1:
        return m
    return _cat([_rows_bcast(m, s, rows // n) for s in range(n)], 0)


def _modulated_norm(x, g_ref, sc_ref, sh_ref):
    rows = x.shape[0]
    h = _rms(x) * g_ref[...] * (1.0 + _per_row(sc_ref, rows)) + _per_row(sh_ref, rows)
    return h.astype(BF16)


def _ada_kernel(c_ref, w_ref, b_ref, o_ref):
    a = _silu(c_ref[...]).astype(BF16)
    o_ref[...] = _dot(a, w_ref[...].astype(BF16)) + b_ref[...]


def _ada_call(c_all, w_ada, b_ada):
    depth, d, n6 = w_ada.shape
    bp = c_all.shape[0]
    tn = _largest_tile(n6, 1024)
    return pl.pallas_call(
        _ada_kernel,
        grid=(depth, n6 // tn),
        in_specs=[
            pl.BlockSpec((bp, d), lambda l, j: (0, 0)),
            pl.BlockSpec((None, d, tn), lambda l, j: (l, 0, j)),
            pl.BlockSpec((None, 1, tn), lambda l, j: (l, 0, j)),
        ],
        out_specs=pl.BlockSpec((None, bp, tn), lambda l, j: (l, 0, j)),
        out_shape=jax.ShapeDtypeStruct((depth, bp, n6), F32),
        compiler_params=_cparams("arbitrary", "arbitrary"),
        name="ada_mod",
    )(c_all, w_ada, b_ada.reshape(depth, 1, n6))


def _pre_kernel(xp_ref, xs_ref, scp_ref, scs_ref, shp_ref, shs_ref, g_ref, wl_ref, h_ref, alr_ref, *, rg):
    def body(x_ref, sc_ref, sh_ref):
        h = _modulated_norm(_rows2d(x_ref), g_ref, sc_ref, sh_ref)
        h_ref[...] = h
        alr_ref[...] = _dot_nt(h, wl_ref[...])

    rg.run(body, (xp_ref, scp_ref, shp_ref), (xs_ref, scs_ref, shs_ref))


def _pre_call(xp, xs, mod_p, mod_s, layer, norm_g, w_alr_t, rows=512):
    rg = _RowGrid(xp.shape, xs.shape, rows)
    d = rg.d
    m_total = xp.shape[0] * xp.shape[1] + xs.shape[0] * xs.shape[1]
    const = lambda i: (0, 0)
    return pl.pallas_call(
        functools.partial(_pre_kernel, rg=rg),
        grid=rg.grid,
        in_specs=[*rg.x_specs(), *rg.mod_specs(layer, 1), *rg.mod_specs(layer, 0),
                  _layer_spec((1, d), layer, const), _layer_spec((LANE, d), layer, const)],
        out_specs=[rg.joint_spec(d), rg.joint_spec(LANE)],
        out_shape=[jax.ShapeDtypeStruct((m_total, d), BF16), jax.ShapeDtypeStruct((m_total, LANE), F32)],
        compiler_params=_cparams("arbitrary"),
        name="prologue",
    )(xp, xs, mod_p, mod_s, mod_p, mod_s, norm_g, w_alr_t)


def _mm_in_kernel(a_ref, w_ref, p_ref, wb_ref, *, acts, scale):
    j = pl.program_id(0)

    @pl.when(pl.program_id(1) == 0)
    def _():
        wb_ref[...] = w_ref[...].astype(BF16)

    fns = {"id": lambda y: y, "scale": lambda y: y * scale, "silu": _silu, "sigmoid": _sigmoid}
    for name in sorted(set(acts)):
        cond = functools.reduce(jnp.logical_or, [j == jj for jj, a in enumerate(acts) if a == name])

        @pl.when(cond)
        def _(fn=fns[name]):
            p_ref[...] = fn(_dot_nt(a_ref[...], wb_ref[...])).astype(BF16)


def _mm_in_call(h, w_in_t, layer, dqk, d_model, d_pool):
    m, k = h.shape
    tm = _largest_row_tile(m, ROW_TILE_CAP)
    starts = _p_columns(dqk, d_model, d_pool)
    n_main = starts[-1] + d_model
    tn = LANE
    for cand in range(LANE, 1024 + 1, LANE):
        if all(s % cand == 0 for s in starts + (n_main,)):
            tn = cand
    kinds = ("scale", "id", "id", "silu", "id", "sigmoid", "sigmoid")
    bounds = starts[1:] + (n_main,)
    acts = []
    for jj in range(n_main // tn):
        acts.append(kinds[next(i for i, e in enumerate(bounds) if jj * tn < e)])
    dk = dqk // N_HEADS
    return pl.pallas_call(
        functools.partial(_mm_in_kernel, acts=tuple(acts), scale=dk ** -0.5),
        grid=(n_main // tn, m // tm),
        in_specs=[
            pl.BlockSpec((tm, k), lambda j, i: (i, 0)),
            _layer_spec((tn, k), layer, lambda j, i: (j, 0)),
        ],
        out_specs=pl.BlockSpec((tm, tn), lambda j, i: (i, j)),
        out_shape=jax.ShapeDtypeStruct((m, n_main), BF16),
        scratch_shapes=[pltpu.VMEM((tn, k), BF16)],
        compiler_params=_cparams("arbitrary", "arbitrary"),
        name="in_proj",
    )(h, w_in_t)


def _gla_consts(t, seq):
    nb = min(STABLE_BLOCK, seq)
    nblk = t // nb
    levels = []
    n = nb * 2
    while n <= seq:
        levels.append(n)
        n *= 2
    nkinds = 3 + len(levels)
    rows = -(-nkinds * nblk // 16) * 16
    row = _iota2((t, t), 0)
    col = _iota2((t, t), 1)
    rowv = _iota2((t, 1), 0)
    same0 = _blk(row, nb) == _blk(col, nb)
    r = _iota2((rows, t), 0)
    u = _iota2((rows, t), 1)
    kind = _blk(r, nblk)
    j = _rem(r, nblk)
    start = j * nb
    seq_start = start & jnp.int32(-seq)
    sel = (kind == 0) & (u >= seq_start) & (u < start)
    sel |= (kind == 1) & (u >= start) & (u < seq_start + seq)
    for li, n in enumerate(levels):
        mid = (start & jnp.int32(-n)) + n // 2
        between = ((u >= mid) & (u < start)) | ((u >= start) & (u < mid))
        sel |= (kind == 2 + li) & between
    sel |= (kind == nkinds - 1) & (_blk(u, seq) == j)
    return {
        "t": t, "seq": seq, "nb": nb, "nblk": nblk, "total_kind": nkinds - 1,
        "tri_local": (same0 & (col <= row)).astype(BF16),
        "sel": jnp.where(sel, 1.0, 0.0).astype(BF16),
        "mask0": same0 & (col <= row),
        "levels": [(_rem(rowv, n) >= n // 2, _blk(row, n) == _blk(col, n)) for n in levels],
    }


def _gla_operands(q, k, e0, g, c):
    nb, nblk = c["nb"], c["nblk"]

    def per_block(kind):
        return _cat([_rows_bcast(g, kind * nblk + j, nb) for j in range(nblk)], 0)

    qe0 = q * jnp.exp(e0)
    ke0 = k * jnp.exp(-e0)
    pairs = [(qe0.astype(BF16), ke0.astype(BF16))]
    for li, (second, _) in enumerate(c["levels"]):
        f = per_block(2 + li)
        pairs.append((jnp.where(second, qe0 * f, 0.0).astype(BF16), jnp.where(second, 0.0, ke0 * f).astype(BF16)))
    qe = qe0 if nb == c["seq"] else qe0 * per_block(0)
    kd = ke0 * per_block(1)
    return pairs, qe.astype(BF16), kd.astype(BF16)


def _gla_scores(pairs, c):
    att = jnp.where(c["mask0"], _dot_nt(*pairs[0]), 0.0)
    for (ql, kl), (_, same) in zip(pairs[1:], c["levels"]):
        att = att + jnp.where(same, _dot_nt(ql, kl), 0.0)
    return att


def _lane_tile(x, width):
    return _cat([x] * (width // x.shape[1]), 1)


def _pool_delta(u, u_parts, carry_parts, w, pos0, seq):
    t = u.shape[0]
    npiece = t // seq
    row = _iota2((t, t), 0)
    col = _iota2((t, t), 1)
    band = ((_blk(row, seq) == _blk(col, seq)) & (col <= row) & (col > row - w)).astype(BF16)
    rc = _iota2((t, npiece * CARRY_ROWS), 0)
    cc = _iota2((t, npiece * CARRY_ROWS), 1)
    cband = ((_blk(rc, seq) == _blk(cc, CARRY_ROWS))
             & (_rem(cc, CARRY_ROWS) > _rem(rc, seq) + CARRY_ROWS - w)).astype(BF16)
    s = _dot_parts(band, u_parts) + _dot_parts(cband, carry_parts)
    cnt = jnp.minimum(pos0 + 1, w).astype(F32)
    return s / cnt - u


def _mix_epilogue(o, silu_og, sig_ga, sig_gb, d, gn, wp, ps):
    y_a = _rms(o) * gn * silu_og.astype(F32)
    y_b = _dot(d.astype(BF16), wp) * ps
    return sig_ga.astype(F32) * y_a + sig_gb.astype(F32) * y_b


def _gate_log_decay(alr, wa2, ba):
    gk = _dot(alr.astype(BF16), wa2) + ba
    return _log_sigmoid(gk) * (1.0 / GATE_NORMALIZER)


def _p_columns(dqk, d_model, d_pool):
    q0, k0, v0 = 0, dqk, 2 * dqk
    og0 = v0 + d_model
    u0 = og0 + d_model
    ga0 = u0 + d_pool
    gb0 = ga0 + d_model
    return q0, k0, v0, og0, u0, ga0, gb0


def _mix_prompt_kernel(p_ref, alr_ref, wa2_ref, ba_ref, gn_ref, wp_ref, ps_ref,
                       m_ref, s_ref, pool_out_ref, car_ref, *, dqk, d_model):
    c = pl.program_id(1)
    t = p_ref.shape[0]
    groups, pin, dv = wp_ref.shape
    dk = dqk // N_HEADS
    q0, k0, v0, og0, u0, ga0, gb0 = _p_columns(dqk, d_model, groups * pin)

    @pl.when(c == 0)
    def _():
        s_ref[...] = jnp.zeros_like(s_ref)
        car_ref[...] = jnp.zeros_like(car_ref)

    consts = _gla_consts(t, t)
    loga = _split(_gate_log_decay(alr_ref[...], wa2_ref[...], ba_ref[...]), LOG_DECAY_TERMS)
    e0_all = _dot_parts(consts["tri_local"], loga)
    g_all = jnp.exp(_dot_parts(consts["sel"], loga))
    tot = consts["total_kind"] * consts["nblk"]
    decay_all = _column_tile(g_all[tot:tot + 1, :])
    pos = c * t + _iota2((t, 1), 0)
    gn = gn_ref[...]

    heads = range(N_HEADS)
    ops = []
    for h in heads:
        q = p_ref[:, q0 + h * dk:q0 + (h + 1) * dk].astype(F32)
        k = p_ref[:, k0 + h * dk:k0 + (h + 1) * dk].astype(F32)
        ops.append(_gla_operands(q, k, e0_all[:, h * dk:(h + 1) * dk], g_all[:, h * dk:(h + 1) * dk], consts))
    atts = [_gla_scores(ops[h][0], consts).astype(BF16) for h in heads]
    outs = []
    for h in heads:
        _, qe, kd = ops[h]
        vb = p_ref[:, v0 + h * dv:v0 + (h + 1) * dv]
        s = s_ref[h]
        outs.append(_dot(atts[h], vb) + _dot(qe, s.astype(BF16)))
        decay = decay_all[h * dk:(h + 1) * dk]
        s_ref[h] = _lane_tile(decay, dv) * s + _dot_tn(kd, vb)

    for h in heads:
        o = outs[h]
        ub = p_ref[:, u0 + h * pin:u0 + (h + 1) * pin]
        d = _pool_delta(ub.astype(F32), (ub,), (car_ref[:, h * pin:(h + 1) * pin],), POOL_WINDOWS[h], pos, t)
        merged = _mix_epilogue(o, p_ref[:, og0 + h * dv:og0 + (h + 1) * dv],
                               p_ref[:, ga0 + h * dv:ga0 + (h + 1) * dv],
                               p_ref[:, gb0 + h * dv:gb0 + (h + 1) * dv], d,
                               gn, wp_ref[h], ps_ref[:, h * dv:(h + 1) * dv])
        m_ref[:, h * dv:(h + 1) * dv] = merged.astype(BF16)

    u_tail = p_ref[t - CARRY_ROWS:, u0:u0 + groups * pin]
    car_ref[...] = u_tail

    @pl.when(c == pl.num_programs(1) - 1)
    def _():
        pool_out_ref[...] = u_tail.astype(F32)[CARRY_ROWS - POOL_BUF:]


def _mix_prompt_call(p, alr, batch, seq_len, layer, w_a2, b_a, gla_norm_g, w_pool, pool_scale, chunk=128):
    dqk = w_a2.shape[2]
    dk = dqk // N_HEADS
    groups, pin, dv = w_pool.shape[1:]
    d_model = pool_scale.shape[2]
    n_p = p.shape[1]
    nc = seq_len // chunk
    const = lambda b, c: (0, 0)
    in_specs = [
        pl.BlockSpec((chunk, n_p), lambda b, c: (b * nc + c, 0)),
        pl.BlockSpec((chunk, LANE), lambda b, c: (b * nc + c, 0)),
        _layer_spec((LANE, dqk), layer, const),
        _layer_spec((1, dqk), layer, const),
        _layer_spec((1, dv), layer, const),
        _layer_spec((groups, pin, dv), layer, lambda b, c: (0, 0, 0)),
        _layer_spec((1, d_model), layer, const),
    ]
    out_specs = [
        pl.BlockSpec((chunk, d_model), lambda b, c: (b * nc + c, 0)),
        pl.BlockSpec((None, N_HEADS, dk, dv), lambda b, c: (b, 0, 0, 0)),
        pl.BlockSpec((None, POOL_BUF, groups * pin), lambda b, c: (b, 0, 0)),
    ]
    out_shape = [
        jax.ShapeDtypeStruct((batch * seq_len, d_model), BF16),
        jax.ShapeDtypeStruct((batch, N_HEADS, dk, dv), F32),
        jax.ShapeDtypeStruct((batch, POOL_BUF, groups * pin), F32),
    ]
    return pl.pallas_call(
        functools.partial(_mix_prompt_kernel, dqk=dqk, d_model=d_model),
        grid=(batch, nc),
        in_specs=in_specs,
        out_specs=out_specs,
        out_shape=out_shape,
        scratch_shapes=[pltpu.VMEM((CARRY_ROWS, groups * pin), BF16)],
        compiler_params=_cparams("arbitrary", "arbitrary"),
        name="mix_prompt",
    )(p, alr, w_a2, b_a, gla_norm_g, w_pool, pool_scale)


def _mix_sample_kernel(p_ref, alr_ref, s0_ref, buf_ref, wa2_ref, ba_ref, gn_ref, wp_ref, ps_ref, *rest,
                       seq, dqk, d_model, n_alias):
    m_ref, s_out_ref, pool_out_ref, car_ref = rest[n_alias:]
    nseq = s0_ref.shape[0]
    t = nseq * seq
    groups, pin, dv = wp_ref.shape
    dk = dqk // N_HEADS
    q0, k0, v0, og0, u0, ga0, gb0 = _p_columns(dqk, d_model, groups * pin)

    car_ref[...] = jnp.zeros_like(car_ref)
    for j in range(nseq):
        car_ref[j * CARRY_ROWS + 1:(j + 1) * CARRY_ROWS, :] = buf_ref[j]

    consts = _gla_consts(t, seq)
    loga = _split(_gate_log_decay(alr_ref[...], wa2_ref[...], ba_ref[...]), LOG_DECAY_TERMS)
    e0_all = _dot_parts(consts["tri_local"], loga)
    g_all = jnp.exp(_dot_parts(consts["sel"], loga))
    tot = consts["total_kind"] * consts["nblk"]
    decay_all = [_column_tile(g_all[tot + j:tot + j + 1, :]) for j in range(nseq)]
    pos = PAST_LEN + _rem(_iota2((t, 1), 0), seq)
    keep = POOL_BUF - seq
    gn = gn_ref[...]

    u_all = p_ref[:, u0:u0 + groups * pin].astype(F32)
    heads = range(N_HEADS)
    ops = []
    for h in heads:
        q = p_ref[:, q0 + h * dk:q0 + (h + 1) * dk].astype(F32)
        k = p_ref[:, k0 + h * dk:k0 + (h + 1) * dk].astype(F32)
        ops.append(_gla_operands(q, k, e0_all[:, h * dk:(h + 1) * dk], g_all[:, h * dk:(h + 1) * dk], consts))
    atts = [_gla_scores(ops[h][0], consts).astype(BF16) for h in heads]
    outs = []
    for h in heads:
        _, qeb, kdb = ops[h]
        vb = p_ref[:, v0 + h * dv:v0 + (h + 1) * dv]
        o_intra = _dot(atts[h], vb)
        o_parts = []
        for j in range(nseq):
            s0 = s0_ref[j, h]
            lo, hi = j * seq, (j + 1) * seq
            o_parts.append(o_intra[lo:hi] + _dot(qeb[lo:hi], s0.astype(BF16)))
            decay = decay_all[j][h * dk:(h + 1) * dk]
            s_out_ref[j, h] = _lane_tile(decay, dv) * s0 + _dot_tn(kdb[lo:hi], vb[lo:hi])
        outs.append(_cat(o_parts, 0))

    for h in heads:
        o = outs[h]
        d = _pool_delta(u_all[:, h * pin:(h + 1) * pin], (p_ref[:, u0 + h * pin:u0 + (h + 1) * pin],),
                        _split(car_ref[:, h * pin:(h + 1) * pin], CARRY_TERMS), POOL_WINDOWS[h], pos, seq)
        merged = _mix_epilogue(o, p_ref[:, og0 + h * dv:og0 + (h + 1) * dv],
                               p_ref[:, ga0 + h * dv:ga0 + (h + 1) * dv],
                               p_ref[:, gb0 + h * dv:gb0 + (h + 1) * dv], d,
                               gn, wp_ref[h], ps_ref[:, h * dv:(h + 1) * dv])
        m_ref[:, h * dv:(h + 1) * dv] = merged.astype(BF16)

    for j in range(nseq):
        pool_out_ref[j, 0:keep, :] = buf_ref[j, seq:POOL_BUF, :]
        pool_out_ref[j, keep:POOL_BUF, :] = u_all[j * seq:(j + 1) * seq]


def _mix_sample_call(p, alr, row_off, seq, layer, state_gla, state_pool, w_a2, b_a, gla_norm_g, w_pool,
                     pool_scale, prev_gla, prev_pool, nseq=4):
    depth, bs, nh, dk, dv = state_gla.shape
    dqk = w_a2.shape[2]
    groups, pin, _ = w_pool.shape[1:]
    d_model = pool_scale.shape[2]
    n_p = p.shape[1]
    assert seq <= POOL_BUF and seq % 8 == 0
    t = nseq * seq
    off = row_off // t
    const = lambda i: (0, 0)
    in_specs = [
        pl.BlockSpec((t, n_p), lambda i: (off + i, 0)),
        pl.BlockSpec((t, LANE), lambda i: (off + i, 0)),
        pl.BlockSpec((None, nseq, nh, dk, dv), lambda i: (layer, i, 0, 0, 0)),
        pl.BlockSpec((None, nseq, POOL_BUF, groups * pin), lambda i: (layer, i, 0, 0)),
        _layer_spec((LANE, dqk), layer, const),
        _layer_spec((1, dqk), layer, const),
        _layer_spec((1, dv), layer, const),
        _layer_spec((groups, pin, dv), layer, lambda i: (0, 0, 0)),
        _layer_spec((1, d_model), layer, const),
    ]
    args = [p, alr, state_gla, state_pool, w_a2, b_a, gla_norm_g, w_pool, pool_scale]
    aliases = {}
    n_alias = 0
    if prev_gla is not None:
        in_specs += [pl.BlockSpec(memory_space=pl.ANY), pl.BlockSpec(memory_space=pl.ANY)]
        args += [prev_gla, prev_pool]
        aliases = {9: 1, 10: 2}
        n_alias = 2
    out_specs = [
        pl.BlockSpec((t, d_model), lambda i: (i, 0)),
        pl.BlockSpec((None, nseq, nh, dk, dv), lambda i: (layer, i, 0, 0, 0)),
        pl.BlockSpec((None, nseq, POOL_BUF, groups * pin), lambda i: (layer, i, 0, 0)),
    ]
    out_shape = [
        jax.ShapeDtypeStruct((bs * seq, d_model), BF16),
        jax.ShapeDtypeStruct(state_gla.shape, F32),
        jax.ShapeDtypeStruct(state_pool.shape, F32),
    ]
    return pl.pallas_call(
        functools.partial(_mix_sample_kernel, seq=seq, dqk=dqk, d_model=d_model, n_alias=n_alias),
        grid=(bs // nseq,),
        in_specs=in_specs,
        out_specs=out_specs,
        out_shape=out_shape,
        scratch_shapes=[pltpu.VMEM((nseq * CARRY_ROWS, groups * pin), F32)],
        input_output_aliases=aliases,
        compiler_params=_cparams("arbitrary"),
        name="mix_sample",
    )(*args)


def _mm_o_kernel(ap_ref, as_ref, xp_ref, xs_ref, g1p_ref, g1s_ref, scp_ref, scs_ref, shp_ref, shs_ref,
                 w_ref, ng_ref, x1p_ref, x1s_ref, h_ref, *, rg):
    def body(a_ref, x_ref, g1_ref, sc_ref, sh_ref, x1_ref):
        x1 = _rows2d(x_ref) + _per_row(g1_ref, rg.rows) * _dot(a_ref[...], w_ref[...])
        x1_ref[...] = x1.reshape(x1_ref.shape)
        h_ref[...] = _modulated_norm(x1, ng_ref, sc_ref, sh_ref)

    rg.run(body, (ap_ref, xp_ref, g1p_ref, scp_ref, shp_ref, x1p_ref),
           (as_ref, xs_ref, g1s_ref, scs_ref, shs_ref, x1s_ref))


def _mm_o_call(merged_p, merged_s, xp, xs, w_o, mod_p, mod_s, layer, norm2_g, rows=256):
    rg = _RowGrid(xp.shape, xs.shape, rows)
    d = rg.d
    m_total = xp.shape[0] * xp.shape[1] + xs.shape[0] * xs.shape[1]
    xsp = rg.x_specs()
    return pl.pallas_call(
        functools.partial(_mm_o_kernel, rg=rg),
        grid=rg.grid,
        in_specs=[*rg.rows_specs(d), *xsp, *rg.mod_specs(layer, 2), *rg.mod_specs(layer, 4),
                  *rg.mod_specs(layer, 3), _resident_spec((d, d), layer),
                  _layer_spec((1, d), layer, lambda i: (0, 0))],
        out_specs=[*xsp, rg.joint_spec(d)],
        out_shape=[jax.ShapeDtypeStruct(xp.shape, F32), jax.ShapeDtypeStruct(xs.shape, F32),
                   jax.ShapeDtypeStruct((m_total, d), BF16)],
        compiler_params=_cparams("arbitrary"),
        name="out_proj",
    )(merged_p, merged_s, xp, xs, mod_p, mod_s, mod_p, mod_s, mod_p, mod_s, w_o, norm2_g)


def _mm_gu_kernel(a_ref, wg_ref, wu_ref, o_ref, wgb_ref, wub_ref):
    @pl.when(pl.program_id(1) == 0)
    def _():
        wgb_ref[...] = wg_ref[...].astype(BF16)
        wub_ref[...] = wu_ref[...].astype(BF16)

    a = a_ref[...]
    o_ref[...] = (_silu(_dot(a, wgb_ref[...])) * _dot(a, wub_ref[...])).astype(BF16)


def _mm_gu_call(h, w_gu, layer):
    m, k = h.shape
    tm = _largest_row_tile(m, ROW_TILE_CAP)
    dff = w_gu.shape[2] // 2
    tn = _largest_tile(dff, 512)
    nj = dff // tn
    return pl.pallas_call(
        _mm_gu_kernel,
        grid=(nj, m // tm),
        in_specs=[
            pl.BlockSpec((tm, k), lambda j, i: (i, 0)),
            _layer_spec((k, tn), layer, lambda j, i: (0, j)),
            _layer_spec((k, tn), layer, lambda j, i: (0, nj + j)),
        ],
        out_specs=pl.BlockSpec((tm, tn), lambda j, i: (i, j)),
        out_shape=jax.ShapeDtypeStruct((m, dff), BF16),
        scratch_shapes=[pltpu.VMEM((k, tn), BF16), pltpu.VMEM((k, tn), BF16)],
        compiler_params=_cparams("arbitrary", "arbitrary"),
        name="gate_up",
    )(h, w_gu, w_gu)


def _mm_down_kernel(a_ref, xp_ref, xs_ref, g2p_ref, g2s_ref, w_ref, ng_ref, *rest, rg, final):
    if final:
        yp_ref, ys_ref = rest

        def body(x_ref, g2_ref, y_ref):
            x2 = _rows2d(x_ref) + _per_row(g2_ref, rg.rows) * _dot(a_ref[...], w_ref[...])
            y_ref[...] = (_rms(x2) * ng_ref[...]).reshape(y_ref.shape)

        rg.run(body, (xp_ref, g2p_ref, yp_ref), (xs_ref, g2s_ref, ys_ref))
    else:
        scp_ref, scs_ref, shp_ref, shs_ref, wl_ref, x2p_ref, x2s_ref, h_ref, alr_ref = rest

        def body(x_ref, g2_ref, sc_ref, sh_ref, x2_ref):
            x2 = _rows2d(x_ref) + _per_row(g2_ref, rg.rows) * _dot(a_ref[...], w_ref[...])
            x2_ref[...] = x2.reshape(x2_ref.shape)
            h = _modulated_norm(x2, ng_ref, sc_ref, sh_ref)
            h_ref[...] = h
            alr_ref[...] = _dot_nt(h, wl_ref[...])

        rg.run(body, (xp_ref, g2p_ref, scp_ref, shp_ref, x2p_ref), (xs_ref, g2s_ref, scs_ref, shs_ref, x2s_ref))


def _mm_down_call(act, xp, xs, w_down, mod_p, mod_s, layer, norm_g, norm_layer, final, w_alr_t, rows=256):
    rg = _RowGrid(xp.shape, xs.shape, rows)
    d = rg.d
    dff = w_down.shape[1]
    m_total = act.shape[0]
    const = lambda i: (0, 0)
    xsp = rg.x_specs()
    in_specs = [rg.joint_spec(dff), *xsp, *rg.mod_specs(layer, 5), _resident_spec((dff, d), layer),
                _layer_spec((1, d), norm_layer, const)]
    args = [act, xp, xs, mod_p, mod_s, w_down, norm_g]
    x_shapes = [jax.ShapeDtypeStruct(xp.shape, F32), jax.ShapeDtypeStruct(xs.shape, F32)]
    if final:
        out_specs, out_shape = [*xsp], x_shapes
    else:
        in_specs += [*rg.mod_specs(layer + 1, 1), *rg.mod_specs(layer + 1, 0),
                     _layer_spec((LANE, d), layer + 1, const)]
        args += [mod_p, mod_s, mod_p, mod_s, w_alr_t]
        out_specs = [*xsp, rg.joint_spec(d), rg.joint_spec(LANE)]
        out_shape = x_shapes + [jax.ShapeDtypeStruct((m_total, d), BF16),
                                jax.ShapeDtypeStruct((m_total, LANE), F32)]
    return pl.pallas_call(
        functools.partial(_mm_down_kernel, rg=rg, final=final),
        grid=rg.grid,
        in_specs=in_specs,
        out_specs=out_specs,
        out_shape=out_shape,
        compiler_params=_cparams("arbitrary"),
        name="down_proj",
    )(*args)


def kernel(x_prompt, x_sample, state_gla, state_pool, c_prompt, c_sample, w_ada, b_ada, norm1_g, w_in, w_a2, b_a,
           gla_norm_g, w_pool, pool_scale, w_o, norm2_g, w_gu, w_down, final_norm_g):
    depth, d, n6 = w_ada.shape
    bp, lp, _ = x_prompt.shape
    bs, ls, _ = x_sample.shape
    mp, ms = bp * lp, bs * ls
    m_total = mp + ms
    rank, dqk = w_a2.shape[1:]
    n_main = w_in.shape[2] - rank

    c_all = jnp.concatenate([c_sample, c_prompt], axis=0)
    c_all = jnp.pad(c_all, ((0, (-c_all.shape[0]) % 8), (0, 0)))
    mod_s = _ada_call(c_all, w_ada, b_ada)
    mod_p = mod_s[:, bs:bs + bp].reshape(depth, bp, 1, n6)

    w_in_t = jnp.swapaxes(w_in, 1, 2)
    w_alr_b = jnp.pad(w_in_t[:, n_main:, :], ((0, 0), (0, LANE - rank), (0, 0))).astype(BF16)
    w_o_b = w_o.astype(BF16)
    w_down_b = w_down.astype(BF16)
    w_a2_b = jnp.pad(w_a2, ((0, 0), (0, LANE - rank), (0, 0))).astype(BF16)
    w_pool_b = w_pool.astype(BF16)
    n1 = norm1_g.reshape(depth, 1, d)
    n2 = norm2_g.reshape(depth, 1, d)
    nf = final_norm_g.reshape(1, 1, d)
    ba3 = b_a.reshape(depth, 1, dqk)
    gn3 = gla_norm_g.reshape(depth, 1, -1)
    ps3 = pool_scale.reshape(depth, 1, d)

    xp, xs = x_prompt, x_sample
    h, alr = _pre_call(xp, xs, mod_p, mod_s, 0, n1, w_alr_b)

    gla_p, pool_p = [], []
    gla_s = pool_s = None
    for l in range(depth):
        p = _mm_in_call(h, w_in_t, l, dqk, d, w_pool.shape[1] * w_pool.shape[2])
        merged_p, s_p, b_p = _mix_prompt_call(p, alr, bp, lp, l, w_a2_b, ba3, gn3, w_pool_b, ps3)
        merged_s, gla_s, pool_s = _mix_sample_call(p, alr, mp, ls, l, state_gla, state_pool, w_a2_b, ba3, gn3,
                                                   w_pool_b, ps3, gla_s, pool_s)
        gla_p.append(s_p)
        pool_p.append(b_p)

        xp, xs, h2 = _mm_o_call(merged_p, merged_s, xp, xs, w_o_b, mod_p, mod_s, l, n2)
        act = _mm_gu_call(h2, w_gu, l)
        if l + 1 < depth:
            xp, xs, h, alr = _mm_down_call(act, xp, xs, w_down_b, mod_p, mod_s, l, n1, l + 1, False, w_alr_b)
        else:
            yp, ys = _mm_down_call(act, xp, xs, w_down_b, mod_p, mod_s, l, nf, 0, True, None)
    return yp, ys, jnp.stack(gla_p), jnp.stack(pool_p), gla_s, pool_s
```

```python
import functools

import jax
import jax.numpy as jnp
from jax import lax
from jax.experimental import pallas as pl
from jax.experimental.pallas import tpu as pltpu

F32 = jnp.float32
BF16 = jnp.bfloat16

N_HEADS = 4
POOL_WINDOWS = (2, 4, 8, 16)
POOL_BUF = 15
CARRY_ROWS = POOL_BUF + 1
GATE_NORMALIZER = 16.0
PAST_LEN = 16384
EPS = 1e-6
STABLE_BLOCK = 16
LOG_DECAY_TERMS = 2
CARRY_TERMS = 3
LANE = 128
MXU_DIM = 256
ROW_TILE_CAP = 1024
VMEM_LIMIT = 56 * 1024 * 1024


def _cparams(*sem):
    return pltpu.CompilerParams(dimension_semantics=sem, vmem_limit_bytes=VMEM_LIMIT)


def _largest_tile(n, cap):
    best = None
    for t in range(LANE, cap + 1, LANE):
        if n % t == 0:
            best = t
    assert best is not None, (n, cap)
    return best


def _largest_row_tile(m, cap):
    best = None
    for t in range(MXU_DIM, cap + 1, MXU_DIM):
        if m % t == 0:
            best = t
    assert best is not None, (m, cap)
    return best


def _dot(a, b):
    return jnp.dot(a, b, preferred_element_type=F32)


def _dot_nt(a, b):
    return lax.dot_general(a, b, (((1,), (1,)), ((), ())), preferred_element_type=F32)


def _dot_tn(a, b):
    return lax.dot_general(a, b, (((0,), (0,)), ((), ())), preferred_element_type=F32)


def _split(x, terms):
    parts = []
    for i in range(terms):
        p = x.astype(BF16)
        parts.append(p)
        if i + 1 < terms:
            x = x - p.astype(F32)
    return tuple(parts)


def _dot_parts(m, parts):
    out = _dot(m, parts[0])
    for p in parts[1:]:
        out = out + _dot(m, p)
    return out


def _column_tile(row):
    return jnp.broadcast_to(row, (LANE, row.shape[1])).T


def _sigmoid(x):
    return 1.0 / (1.0 + jnp.exp(-x))


def _silu(x):
    return x * _sigmoid(x)


def _log_sigmoid(x):
    return jnp.minimum(x, 0.0) - jnp.log(1.0 + jnp.exp(-jnp.abs(x)))


def _rms(x):
    return x * lax.rsqrt(jnp.mean(x * x, axis=-1, keepdims=True) + EPS)


def _rows_bcast(x, row, n):
    return jnp.broadcast_to(x[row:row + 1, :], (n, x.shape[1]))


def _cat(parts, axis):
    return parts[0] if len(parts) == 1 else jnp.concatenate(parts, axis=axis)


def _iota2(shape, dim):
    return lax.broadcasted_iota(jnp.int32, shape, dim)


def _blk(x, n):
    assert n & (n - 1) == 0
    return lax.shift_right_logical(x, jnp.int32(n.bit_length() - 1))


def _rem(x, n):
    assert n & (n - 1) == 0
    return x & jnp.int32(n - 1)


def _layer_spec(block, layer, tail):
    return pl.BlockSpec((None,) + block, lambda *g: (layer,) + tail(*g))


def _resident_spec(block, layer):
    zeros = (0,) * len(block)
    return pl.BlockSpec((None,) + block, lambda *g: (layer,) + zeros, pipeline_mode=pl.Buffered(1))


class _RowGrid:
    def __init__(self, xp_shape, xs_shape, rows):
        bp, lp, d = xp_shape
        bs, ls, _ = xs_shape
        assert lp % rows == 0 and rows % ls == 0 and bs % (rows // ls) == 0
        self.rows, self.d = rows, d
        self.tiles_per_seq = lp // rows
        self.n_p = bp * self.tiles_per_seq
        self.seqs_per_tile = rows // ls
        self.n_s = bs // self.seqs_per_tile
        self.ls = ls
        self.grid = (self.n_p + self.n_s,)

    def _ip(self, i):
        return jnp.minimum(i, self.n_p - 1)

    def _is(self, i):
        return jnp.maximum(i - self.n_p, 0)

    def x_specs(self):
        tps = self.tiles_per_seq
        return (pl.BlockSpec((1, self.rows, self.d), lambda i: (self._ip(i) // tps, self._ip(i) % tps, 0)),
                pl.BlockSpec((self.seqs_per_tile, self.ls, self.d), lambda i: (self._is(i), 0, 0)))

    def mod_specs(self, layer, j):
        tps = self.tiles_per_seq
        return (pl.BlockSpec((None, None, 1, self.d), lambda i: (layer, self._ip(i) // tps, 0, j)),
                pl.BlockSpec((None, self.seqs_per_tile, self.d), lambda i: (layer, self._is(i), j)))

    def rows_specs(self, width):
        return (pl.BlockSpec((self.rows, width), lambda i: (self._ip(i), 0)),
                pl.BlockSpec((self.rows, width), lambda i: (self._is(i), 0)))

    def joint_spec(self, width):
        return pl.BlockSpec((self.rows, width), lambda i: (i, 0))

    def run(self, body, prompt_refs, sample_refs):
        i = pl.program_id(0)

        @pl.when(i < self.n_p)
        def _():
            body(*prompt_refs)

        @pl.when(i >= self.n_p)
        def _():
            body(*sample_refs)


def _rows2d(x_ref):
    bb, tt, d = x_ref.shape
    return x_ref[...].reshape(bb * tt, d)


def _per_row(mod_ref, rows):
    m = mod_ref[...]
    n = m.shape[0]
    if n == 1:
        return m
    return _cat([_rows_bcast(m, s, rows // n) for s in range(n)], 0)


def _modulated_norm(x, g_ref, sc_ref, sh_ref):
    rows = x.shape[0]
    h = _rms(x) * g_ref[...] * (1.0 + _per_row(sc_ref, rows)) + _per_row(sh_ref, rows)
    return h.astype(BF16)


def _ada_kernel(c_ref, w_ref, b_ref, o_ref):
    a = _silu(c_ref[...]).astype(BF16)
    o_ref[...] = _dot(a, w_ref[...].astype(BF16)) + b_ref[...]


def _ada_call(c_all, w_ada, b_ada):
    depth, d, n6 = w_ada.shape
    bp = c_all.shape[0]
    tn = _largest_tile(n6, 1024)
    return pl.pallas_call(
        _ada_kernel,
        grid=(depth, n6 // tn),
        in_specs=[
            pl.BlockSpec((bp, d), lambda l, j: (0, 0)),
            pl.BlockSpec((None, d, tn), lambda l, j: (l, 0, j)),
            pl.BlockSpec((None, 1, tn), lambda l, j: (l, 0, j)),
        ],
        out_specs=pl.BlockSpec((None, bp, tn), lambda l, j: (l, 0, j)),
        out_shape=jax.ShapeDtypeStruct((depth, bp, n6), F32),
        compiler_params=_cparams("arbitrary", "arbitrary"),
        name="ada_mod",
    )(c_all, w_ada, b_ada.reshape(depth, 1, n6))


def _log_decay_terms(h, wl_ref, wa2_ref, ba_ref):
    alr = _dot_nt(h, wl_ref[...])
    gk = _dot(alr.astype(BF16), wa2_ref[...]) + ba_ref[...]
    return _split(_log_sigmoid(gk) * (1.0 / GATE_NORMALIZER), LOG_DECAY_TERMS)


def _gate_specs(rg, layer, d, dqk):
    const = lambda i: (0, 0)
    ins = [_layer_spec((LANE, d), layer, const), _layer_spec((LANE, dqk), layer, const),
           _layer_spec((1, dqk), layer, const)]
    outs = [rg.joint_spec(dqk)] * LOG_DECAY_TERMS
    return ins, outs


def _pre_kernel(xp_ref, xs_ref, scp_ref, scs_ref, shp_ref, shs_ref, g_ref, wl_ref, wa2_ref, ba_ref,
                h_ref, *lg_refs, rg):
    def body(x_ref, sc_ref, sh_ref):
        h = _modulated_norm(_rows2d(x_ref), g_ref, sc_ref, sh_ref)
        h_ref[...] = h
        for ref, term in zip(lg_refs, _log_decay_terms(h, wl_ref, wa2_ref, ba_ref)):
            ref[...] = term

    rg.run(body, (xp_ref, scp_ref, shp_ref), (xs_ref, scs_ref, shs_ref))


def _pre_call(xp, xs, mod_p, mod_s, layer, norm_g, w_alr_t, w_a2, b_a, rows=512):
    rg = _RowGrid(xp.shape, xs.shape, rows)
    d = rg.d
    dqk = w_a2.shape[2]
    m_total = xp.shape[0] * xp.shape[1] + xs.shape[0] * xs.shape[1]
    gate_in, gate_out = _gate_specs(rg, layer, d, dqk)
    return pl.pallas_call(
        functools.partial(_pre_kernel, rg=rg),
        grid=rg.grid,
        in_specs=[*rg.x_specs(), *rg.mod_specs(layer, 1), *rg.mod_specs(layer, 0),
                  _layer_spec((1, d), layer, lambda i: (0, 0)), *gate_in],
        out_specs=[rg.joint_spec(d), *gate_out],
        out_shape=[jax.ShapeDtypeStruct((m_total, d), BF16)]
        + [jax.ShapeDtypeStruct((m_total, dqk), BF16)] * LOG_DECAY_TERMS,
        compiler_params=_cparams("arbitrary"),
        name="prologue",
    )(xp, xs, mod_p, mod_s, mod_p, mod_s, norm_g, w_alr_t, w_a2, b_a)


def _mm_in_kernel(a_ref, w_ref, p_ref, wb_ref, *, acts, scale):
    j = pl.program_id(0)

    @pl.when(pl.program_id(1) == 0)
    def _():
        wb_ref[...] = w_ref[...].astype(BF16)

    fns = {"id": lambda y: y, "scale": lambda y: y * scale, "silu": _silu, "sigmoid": _sigmoid}
    for name in sorted(set(acts)):
        cond = functools.reduce(jnp.logical_or, [j == jj for jj, a in enumerate(acts) if a == name])

        @pl.when(cond)
        def _(fn=fns[name]):
            p_ref[...] = fn(_dot_nt(a_ref[...], wb_ref[...])).astype(BF16)


def _mm_in_call(h, w_in_t, layer, dqk, d_model, d_pool):
    m, k = h.shape
    tm = _largest_row_tile(m, ROW_TILE_CAP)
    starts = _p_columns(dqk, d_model, d_pool)
    n_main = starts[-1] + d_model
    tn = LANE
    for cand in range(LANE, 1024 + 1, LANE):
        if all(s % cand == 0 for s in starts + (n_main,)):
            tn = cand
    kinds = ("scale", "id", "id", "silu", "id", "sigmoid", "sigmoid")
    bounds = starts[1:] + (n_main,)
    acts = []
    for jj in range(n_main // tn):
        acts.append(kinds[next(i for i, e in enumerate(bounds) if jj * tn < e)])
    dk = dqk // N_HEADS
    return pl.pallas_call(
        functools.partial(_mm_in_kernel, acts=tuple(acts), scale=dk ** -0.5),
        grid=(n_main // tn, m // tm),
        in_specs=[
            pl.BlockSpec((tm, k), lambda j, i: (i, 0)),
            _layer_spec((tn, k), layer, lambda j, i: (j, 0)),
        ],
        out_specs=pl.BlockSpec((tm, tn), lambda j, i: (i, j)),
        out_shape=jax.ShapeDtypeStruct((m, n_main), BF16),
        scratch_shapes=[pltpu.VMEM((tn, k), BF16)],
        compiler_params=_cparams("arbitrary", "arbitrary"),
        name="in_proj",
    )(h, w_in_t)


def _gla_consts(t, seq):
    nb = min(STABLE_BLOCK, seq)
    nblk = t // nb
    levels = []
    n = nb * 2
    while n <= seq:
        levels.append(n)
        n *= 2
    nkinds = 3 + len(levels)
    rows = -(-nkinds * nblk // 16) * 16
    row = _iota2((t, t), 0)
    col = _iota2((t, t), 1)
    rowv = _iota2((t, 1), 0)
    same0 = _blk(row, nb) == _blk(col, nb)
    r = _iota2((rows, t), 0)
    u = _iota2((rows, t), 1)
    kind = _blk(r, nblk)
    j = _rem(r, nblk)
    start = j * nb
    seq_start = start & jnp.int32(-seq)
    sel = (kind == 0) & (u >= seq_start) & (u < start)
    sel |= (kind == 1) & (u >= start) & (u < seq_start + seq)
    for li, n in enumerate(levels):
        mid = (start & jnp.int32(-n)) + n // 2
        between = ((u >= mid) & (u < start)) | ((u >= start) & (u < mid))
        sel |= (kind == 2 + li) & between
    sel |= (kind == nkinds - 1) & (_blk(u, seq) == j)
    return {
        "t": t, "seq": seq, "nb": nb, "nblk": nblk, "total_kind": nkinds - 1,
        "tri_local": (same0 & (col <= row)).astype(BF16),
        "sel": jnp.where(sel, 1.0, 0.0).astype(BF16),
        "mask0": same0 & (col <= row),
        "levels": [(_rem(rowv, n) >= n // 2, _blk(row, n) == _blk(col, n)) for n in levels],
    }


def _gla_operands(q, k, e0, g, c):
    nb, nblk = c["nb"], c["nblk"]

    def per_block(kind):
        return _cat([_rows_bcast(g, kind * nblk + j, nb) for j in range(nblk)], 0)

    qe0 = q * jnp.exp(e0)
    ke0 = k * jnp.exp(-e0)
    pairs = [(qe0.astype(BF16), ke0.astype(BF16))]
    for li, (second, _) in enumerate(c["levels"]):
        f = per_block(2 + li)
        pairs.append((jnp.where(second, qe0 * f, 0.0).astype(BF16), jnp.where(second, 0.0, ke0 * f).astype(BF16)))
    qe = qe0 if nb == c["seq"] else qe0 * per_block(0)
    kd = ke0 * per_block(1)
    return pairs, qe.astype(BF16), kd.astype(BF16)


def _gla_scores(pairs, c):
    att = jnp.where(c["mask0"], _dot_nt(*pairs[0]), 0.0)
    for (ql, kl), (_, same) in zip(pairs[1:], c["levels"]):
        att = att + jnp.where(same, _dot_nt(ql, kl), 0.0)
    return att


def _lane_tile(x, width):
    return _cat([x] * (width // x.shape[1]), 1)


def _pool_delta(u, u_parts, carry_parts, w, pos0, seq):
    t = u.shape[0]
    npiece = t // seq
    row = _iota2((t, t), 0)
    col = _iota2((t, t), 1)
    band = ((_blk(row, seq) == _blk(col, seq)) & (col <= row) & (col > row - w)).astype(BF16)
    rc = _iota2((t, npiece * CARRY_ROWS), 0)
    cc = _iota2((t, npiece * CARRY_ROWS), 1)
    cband = ((_blk(rc, seq) == _blk(cc, CARRY_ROWS))
             & (_rem(cc, CARRY_ROWS) > _rem(rc, seq) + CARRY_ROWS - w)).astype(BF16)
    s = _dot_parts(band, u_parts) + _dot_parts(cband, carry_parts)
    cnt = jnp.minimum(pos0 + 1, w).astype(F32)
    return s / cnt - u


def _mix_merge(o, silu_og, sig_ga, sig_gb, pooled, gn, ps):
    y_a = _rms(o) * gn * silu_og.astype(F32)
    return sig_ga.astype(F32) * y_a + sig_gb.astype(F32) * (pooled * ps)


def _mix_epilogue(o, silu_og, sig_ga, sig_gb, d, gn, wp, ps):
    return _mix_merge(o, silu_og, sig_ga, sig_gb, _dot(d.astype(BF16), wp), gn, ps)


def _p_columns(dqk, d_model, d_pool):
    q0, k0, v0 = 0, dqk, 2 * dqk
    og0 = v0 + d_model
    u0 = og0 + d_model
    ga0 = u0 + d_pool
    gb0 = ga0 + d_model
    return q0, k0, v0, og0, u0, ga0, gb0


def _mix_prompt_kernel(p_ref, *refs, dqk, d_model):
    lg_refs = refs[:LOG_DECAY_TERMS]
    gn_ref, wp_ref, ps_ref, m_ref, s_ref, pool_out_ref, car_ref = refs[LOG_DECAY_TERMS:]
    c = pl.program_id(1)
    t = p_ref.shape[0]
    groups, pin, dv = wp_ref.shape
    dk = dqk // N_HEADS
    q0, k0, v0, og0, u0, ga0, gb0 = _p_columns(dqk, d_model, groups * pin)

    @pl.when(c == 0)
    def _():
        s_ref[...] = jnp.zeros_like(s_ref)
        car_ref[...] = jnp.zeros_like(car_ref)

    consts = _gla_consts(t, t)
    loga = tuple(ref[...] for ref in lg_refs)
    e0_all = _dot_parts(consts["tri_local"], loga)
    g_all = jnp.exp(_dot_parts(consts["sel"], loga))
    tot = consts["total_kind"] * consts["nblk"]
    decay_all = _column_tile(g_all[tot:tot + 1, :])
    pos = c * t + _iota2((t, 1), 0)
    gn = gn_ref[...]

    heads = range(N_HEADS)
    deltas = []
    for h in heads:
        ub = p_ref[:, u0 + h * pin:u0 + (h + 1) * pin]
        deltas.append(_pool_delta(ub.astype(F32), (ub,), (car_ref[:, h * pin:(h + 1) * pin],), POOL_WINDOWS[h],
                                  pos, t).astype(BF16))
    ops = []
    for h in heads:
        q = p_ref[:, q0 + h * dk:q0 + (h + 1) * dk].astype(F32)
        k = p_ref[:, k0 + h * dk:k0 + (h + 1) * dk].astype(F32)
        ops.append(_gla_operands(q, k, e0_all[:, h * dk:(h + 1) * dk], g_all[:, h * dk:(h + 1) * dk], consts))
    pooled = [_dot(deltas[h], wp_ref[h]) for h in heads]
    atts = [_gla_scores(ops[h][0], consts).astype(BF16) for h in heads]
    outs = []
    for h in heads:
        _, qe, kd = ops[h]
        vb = p_ref[:, v0 + h * dv:v0 + (h + 1) * dv]
        s = s_ref[h]
        outs.append(_dot(atts[h], vb) + _dot(qe, s.astype(BF16)))
        decay = decay_all[h * dk:(h + 1) * dk]
        s_ref[h] = _lane_tile(decay, dv) * s + _dot_tn(kd, vb)

    for h in heads:
        merged = _mix_merge(outs[h], p_ref[:, og0 + h * dv:og0 + (h + 1) * dv],
                            p_ref[:, ga0 + h * dv:ga0 + (h + 1) * dv], p_ref[:, gb0 + h * dv:gb0 + (h + 1) * dv],
                            pooled[h], gn, ps_ref[:, h * dv:(h + 1) * dv])
        m_ref[:, h * dv:(h + 1) * dv] = merged.astype(BF16)

    u_tail = p_ref[t - CARRY_ROWS:, u0:u0 + groups * pin]
    car_ref[...] = u_tail

    @pl.when(c == pl.num_programs(1) - 1)
    def _():
        pool_out_ref[...] = u_tail.astype(F32)[CARRY_ROWS - POOL_BUF:]


def _mix_prompt_call(p, lg, batch, seq_len, layer, gla_norm_g, w_pool, pool_scale, chunk=128):
    dqk = lg[0].shape[1]
    dk = dqk // N_HEADS
    groups, pin, dv = w_pool.shape[1:]
    d_model = pool_scale.shape[2]
    n_p = p.shape[1]
    nc = seq_len // chunk
    const = lambda b, c: (0, 0)
    in_specs = [
        pl.BlockSpec((chunk, n_p), lambda b, c: (b * nc + c, 0)),
        *[pl.BlockSpec((chunk, dqk), lambda b, c: (b * nc + c, 0)) for _ in lg],
        _layer_spec((1, dv), layer, const),
        _layer_spec((groups, pin, dv), layer, lambda b, c: (0, 0, 0)),
        _layer_spec((1, d_model), layer, const),
    ]
    out_specs = [
        pl.BlockSpec((chunk, d_model), lambda b, c: (b * nc + c, 0)),
        pl.BlockSpec((None, N_HEADS, dk, dv), lambda b, c: (b, 0, 0, 0)),
        pl.BlockSpec((None, POOL_BUF, groups * pin), lambda b, c: (b, 0, 0)),
    ]
    out_shape = [
        jax.ShapeDtypeStruct((batch * seq_len, d_model), BF16),
        jax.ShapeDtypeStruct((batch, N_HEADS, dk, dv), F32),
        jax.ShapeDtypeStruct((batch, POOL_BUF, groups * pin), F32),
    ]
    return pl.pallas_call(
        functools.partial(_mix_prompt_kernel, dqk=dqk, d_model=d_model),
        grid=(batch, nc),
        in_specs=in_specs,
        out_specs=out_specs,
        out_shape=out_shape,
        scratch_shapes=[pltpu.VMEM((CARRY_ROWS, groups * pin), BF16)],
        compiler_params=_cparams("arbitrary", "arbitrary"),
        name="mix_prompt",
    )(p, *lg, gla_norm_g, w_pool, pool_scale)


def _mix_sample_kernel(p_ref, *refs, seq, dqk, d_model, n_alias):
    lg_refs = refs[:LOG_DECAY_TERMS]
    s0_ref, buf_ref, gn_ref, wp_ref, ps_ref = refs[LOG_DECAY_TERMS:LOG_DECAY_TERMS + 5]
    m_ref, s_out_ref, pool_out_ref, car_ref = refs[LOG_DECAY_TERMS + 5 + n_alias:]
    nseq = s0_ref.shape[0]
    t = nseq * seq
    groups, pin, dv = wp_ref.shape
    dk = dqk // N_HEADS
    q0, k0, v0, og0, u0, ga0, gb0 = _p_columns(dqk, d_model, groups * pin)

    car_ref[...] = jnp.zeros_like(car_ref)
    for j in range(nseq):
        car_ref[j * CARRY_ROWS + 1:(j + 1) * CARRY_ROWS, :] = buf_ref[j]

    consts = _gla_consts(t, seq)
    loga = tuple(ref[...] for ref in lg_refs)
    e0_all = _dot_parts(consts["tri_local"], loga)
    g_all = jnp.exp(_dot_parts(consts["sel"], loga))
    tot = consts["total_kind"] * consts["nblk"]
    decay_all = [_column_tile(g_all[tot + j:tot + j + 1, :]) for j in range(nseq)]
    pos = PAST_LEN + _rem(_iota2((t, 1), 0), seq)
    keep = POOL_BUF - seq
    gn = gn_ref[...]

    u_all = p_ref[:, u0:u0 + groups * pin].astype(F32)
    heads = range(N_HEADS)
    ops = []
    for h in heads:
        q = p_ref[:, q0 + h * dk:q0 + (h + 1) * dk].astype(F32)
        k = p_ref[:, k0 + h * dk:k0 + (h + 1) * dk].astype(F32)
        ops.append(_gla_operands(q, k, e0_all[:, h * dk:(h + 1) * dk], g_all[:, h * dk:(h + 1) * dk], consts))
    atts = [_gla_scores(ops[h][0], consts).astype(BF16) for h in heads]
    outs = []
    for h in heads:
        _, qeb, kdb = ops[h]
        vb = p_ref[:, v0 + h * dv:v0 + (h + 1) * dv]
        o_intra = _dot(atts[h], vb)
        o_parts = []
        for j in range(nseq):
            s0 = s0_ref[j, h]
            lo, hi = j * seq, (j + 1) * seq
            o_parts.append(o_intra[lo:hi] + _dot(qeb[lo:hi], s0.astype(BF16)))
            decay = decay_all[j][h * dk:(h + 1) * dk]
            s_out_ref[j, h] = _lane_tile(decay, dv) * s0 + _dot_tn(kdb[lo:hi], vb[lo:hi])
        outs.append(_cat(o_parts, 0))

    for h in heads:
        o = outs[h]
        d = _pool_delta(u_all[:, h * pin:(h + 1) * pin], (p_ref[:, u0 + h * pin:u0 + (h + 1) * pin],),
                        _split(car_ref[:, h * pin:(h + 1) * pin], CARRY_TERMS), POOL_WINDOWS[h], pos, seq)
        merged = _mix_epilogue(o, p_ref[:, og0 + h * dv:og0 + (h + 1) * dv],
                               p_ref[:, ga0 + h * dv:ga0 + (h + 1) * dv],
                               p_ref[:, gb0 + h * dv:gb0 + (h + 1) * dv], d,
                               gn, wp_ref[h], ps_ref[:, h * dv:(h + 1) * dv])
        m_ref[:, h * dv:(h + 1) * dv] = merged.astype(BF16)

    for j in range(nseq):
        pool_out_ref[j, 0:keep, :] = buf_ref[j, seq:POOL_BUF, :]
        pool_out_ref[j, keep:POOL_BUF, :] = u_all[j * seq:(j + 1) * seq]


def _mix_sample_call(p, lg, row_off, seq, layer, state_gla, state_pool, gla_norm_g, w_pool,
                     pool_scale, prev_gla, prev_pool, nseq=4):
    depth, bs, nh, dk, dv = state_gla.shape
    dqk = lg[0].shape[1]
    groups, pin, _ = w_pool.shape[1:]
    d_model = pool_scale.shape[2]
    n_p = p.shape[1]
    assert seq <= POOL_BUF and seq % 8 == 0
    t = nseq * seq
    off = row_off // t
    const = lambda i: (0, 0)
    in_specs = [
        pl.BlockSpec((t, n_p), lambda i: (off + i, 0)),
        *[pl.BlockSpec((t, dqk), lambda i: (off + i, 0)) for _ in lg],
        pl.BlockSpec((None, nseq, nh, dk, dv), lambda i: (layer, i, 0, 0, 0)),
        pl.BlockSpec((None, nseq, POOL_BUF, groups * pin), lambda i: (layer, i, 0, 0)),
        _layer_spec((1, dv), layer, const),
        _layer_spec((groups, pin, dv), layer, lambda i: (0, 0, 0)),
        _layer_spec((1, d_model), layer, const),
    ]
    args = [p, *lg, state_gla, state_pool, gla_norm_g, w_pool, pool_scale]
    aliases = {}
    n_alias = 0
    if prev_gla is not None:
        in_specs += [pl.BlockSpec(memory_space=pl.ANY), pl.BlockSpec(memory_space=pl.ANY)]
        aliases = {len(args): 1, len(args) + 1: 2}
        args += [prev_gla, prev_pool]
        n_alias = 2
    out_specs = [
        pl.BlockSpec((t, d_model), lambda i: (i, 0)),
        pl.BlockSpec((None, nseq, nh, dk, dv), lambda i: (layer, i, 0, 0, 0)),
        pl.BlockSpec((None, nseq, POOL_BUF, groups * pin), lambda i: (layer, i, 0, 0)),
    ]
    out_shape = [
        jax.ShapeDtypeStruct((bs * seq, d_model), BF16),
        jax.ShapeDtypeStruct(state_gla.shape, F32),
        jax.ShapeDtypeStruct(state_pool.shape, F32),
    ]
    return pl.pallas_call(
        functools.partial(_mix_sample_kernel, seq=seq, dqk=dqk, d_model=d_model, n_alias=n_alias),
        grid=(bs // nseq,),
        in_specs=in_specs,
        out_specs=out_specs,
        out_shape=out_shape,
        scratch_shapes=[pltpu.VMEM((nseq * CARRY_ROWS, groups * pin), F32)],
        input_output_aliases=aliases,
        compiler_params=_cparams("arbitrary"),
        name="mix_sample",
    )(*args)


def _mm_o_kernel(ap_ref, as_ref, xp_ref, xs_ref, g1p_ref, g1s_ref, scp_ref, scs_ref, shp_ref, shs_ref,
                 w_ref, ng_ref, x1p_ref, x1s_ref, h_ref, *, rg):
    def body(a_ref, x_ref, g1_ref, sc_ref, sh_ref, x1_ref):
        x1 = _rows2d(x_ref) + _per_row(g1_ref, rg.rows) * _dot(a_ref[...], w_ref[...])
        x1_ref[...] = x1.reshape(x1_ref.shape)
        h_ref[...] = _modulated_norm(x1, ng_ref, sc_ref, sh_ref)

    rg.run(body, (ap_ref, xp_ref, g1p_ref, scp_ref, shp_ref, x1p_ref),
           (as_ref, xs_ref, g1s_ref, scs_ref, shs_ref, x1s_ref))


def _mm_o_call(merged_p, merged_s, xp, xs, w_o, mod_p, mod_s, layer, norm2_g, rows=256):
    rg = _RowGrid(xp.shape, xs.shape, rows)
    d = rg.d
    m_total = xp.shape[0] * xp.shape[1] + xs.shape[0] * xs.shape[1]
    xsp = rg.x_specs()
    return pl.pallas_call(
        functools.partial(_mm_o_kernel, rg=rg),
        grid=rg.grid,
        in_specs=[*rg.rows_specs(d), *xsp, *rg.mod_specs(layer, 2), *rg.mod_specs(layer, 4),
                  *rg.mod_specs(layer, 3), _resident_spec((d, d), layer),
                  _layer_spec((1, d), layer, lambda i: (0, 0))],
        out_specs=[*xsp, rg.joint_spec(d)],
        out_shape=[jax.ShapeDtypeStruct(xp.shape, F32), jax.ShapeDtypeStruct(xs.shape, F32),
                   jax.ShapeDtypeStruct((m_total, d), BF16)],
        compiler_params=_cparams("arbitrary"),
        name="out_proj",
    )(merged_p, merged_s, xp, xs, mod_p, mod_s, mod_p, mod_s, mod_p, mod_s, w_o, norm2_g)


def _mm_gu_kernel(a_ref, wg_ref, wu_ref, o_ref, wgb_ref, wub_ref):
    @pl.when(pl.program_id(1) == 0)
    def _():
        wgb_ref[...] = wg_ref[...].astype(BF16)
        wub_ref[...] = wu_ref[...].astype(BF16)

    a = a_ref[...]
    o_ref[...] = (_silu(_dot(a, wgb_ref[...])) * _dot(a, wub_ref[...])).astype(BF16)


def _mm_gu_call(h, w_gu, layer):
    m, k = h.shape
    tm = _largest_row_tile(m, ROW_TILE_CAP)
    dff = w_gu.shape[2] // 2
    tn = _largest_tile(dff, 512)
    nj = dff // tn
    return pl.pallas_call(
        _mm_gu_kernel,
        grid=(nj, m // tm),
        in_specs=[
            pl.BlockSpec((tm, k), lambda j, i: (i, 0)),
            _layer_spec((k, tn), layer, lambda j, i: (0, j)),
            _layer_spec((k, tn), layer, lambda j, i: (0, nj + j)),
        ],
        out_specs=pl.BlockSpec((tm, tn), lambda j, i: (i, j)),
        out_shape=jax.ShapeDtypeStruct((m, dff), BF16),
        scratch_shapes=[pltpu.VMEM((k, tn), BF16), pltpu.VMEM((k, tn), BF16)],
        compiler_params=_cparams("arbitrary", "arbitrary"),
        name="gate_up",
    )(h, w_gu, w_gu)


def _mm_down_kernel(a_ref, xp_ref, xs_ref, g2p_ref, g2s_ref, w_ref, ng_ref, *rest, rg, final):
    if final:
        yp_ref, ys_ref = rest

        def body(x_ref, g2_ref, y_ref):
            x2 = _rows2d(x_ref) + _per_row(g2_ref, rg.rows) * _dot(a_ref[...], w_ref[...])
            y_ref[...] = (_rms(x2) * ng_ref[...]).reshape(y_ref.shape)

        rg.run(body, (xp_ref, g2p_ref, yp_ref), (xs_ref, g2s_ref, ys_ref))
    else:
        scp_ref, scs_ref, shp_ref, shs_ref, wl_ref, wa2_ref, ba_ref, x2p_ref, x2s_ref, h_ref, *lg_refs = rest

        def body(x_ref, g2_ref, sc_ref, sh_ref, x2_ref):
            x2 = _rows2d(x_ref) + _per_row(g2_ref, rg.rows) * _dot(a_ref[...], w_ref[...])
            x2_ref[...] = x2.reshape(x2_ref.shape)
            h = _modulated_norm(x2, ng_ref, sc_ref, sh_ref)
            h_ref[...] = h
            for ref, term in zip(lg_refs, _log_decay_terms(h, wl_ref, wa2_ref, ba_ref)):
                ref[...] = term

        rg.run(body, (xp_ref, g2p_ref, scp_ref, shp_ref, x2p_ref), (xs_ref, g2s_ref, scs_ref, shs_ref, x2s_ref))


def _mm_down_call(act, xp, xs, w_down, mod_p, mod_s, layer, norm_g, norm_layer, final, w_alr_t, w_a2, b_a,
                  rows=256):
    rg = _RowGrid(xp.shape, xs.shape, rows)
    d = rg.d
    dff = w_down.shape[1]
    m_total = act.shape[0]
    xsp = rg.x_specs()
    in_specs = [rg.joint_spec(dff), *xsp, *rg.mod_specs(layer, 5), _resident_spec((dff, d), layer),
                _layer_spec((1, d), norm_layer, lambda i: (0, 0))]
    args = [act, xp, xs, mod_p, mod_s, w_down, norm_g]
    x_shapes = [jax.ShapeDtypeStruct(xp.shape, F32), jax.ShapeDtypeStruct(xs.shape, F32)]
    if final:
        out_specs, out_shape = [*xsp], x_shapes
    else:
        dqk = w_a2.shape[2]
        gate_in, gate_out = _gate_specs(rg, layer + 1, d, dqk)
        in_specs += [*rg.mod_specs(layer + 1, 1), *rg.mod_specs(layer + 1, 0), *gate_in]
        args += [mod_p, mod_s, mod_p, mod_s, w_alr_t, w_a2, b_a]
        out_specs = [*xsp, rg.joint_spec(d), *gate_out]
        out_shape = x_shapes + [jax.ShapeDtypeStruct((m_total, d), BF16)] \
            + [jax.ShapeDtypeStruct((m_total, dqk), BF16)] * LOG_DECAY_TERMS
    return pl.pallas_call(
        functools.partial(_mm_down_kernel, rg=rg, final=final),
        grid=rg.grid,
        in_specs=in_specs,
        out_specs=out_specs,
        out_shape=out_shape,
        compiler_params=_cparams("arbitrary"),
        name="down_proj",
    )(*args)


def kernel(x_prompt, x_sample, state_gla, state_pool, c_prompt, c_sample, w_ada, b_ada, norm1_g, w_in, w_a2, b_a,
           gla_norm_g, w_pool, pool_scale, w_o, norm2_g, w_gu, w_down, final_norm_g):
    depth, d, n6 = w_ada.shape
    bp, lp, _ = x_prompt.shape
    bs, ls, _ = x_sample.shape
    mp, ms = bp * lp, bs * ls
    m_total = mp + ms
    rank, dqk = w_a2.shape[1:]
    n_main = w_in.shape[2] - rank

    c_all = jnp.concatenate([c_sample, c_prompt], axis=0)
    c_all = jnp.pad(c_all, ((0, (-c_all.shape[0]) % 8), (0, 0)))
    mod_s = _ada_call(c_all, w_ada, b_ada)
    mod_p = mod_s[:, bs:bs + bp].reshape(depth, bp, 1, n6)

    w_in_t = jnp.swapaxes(w_in, 1, 2)
    w_alr_b = jnp.pad(w_in_t[:, n_main:, :], ((0, 0), (0, LANE - rank), (0, 0))).astype(BF16)
    w_o_b = w_o.astype(BF16)
    w_down_b = w_down.astype(BF16)
    w_a2_b = jnp.pad(w_a2, ((0, 0), (0, LANE - rank), (0, 0))).astype(BF16)
    w_pool_b = w_pool.astype(BF16)
    n1 = norm1_g.reshape(depth, 1, d)
    n2 = norm2_g.reshape(depth, 1, d)
    nf = final_norm_g.reshape(1, 1, d)
    ba3 = b_a.reshape(depth, 1, dqk)
    gn3 = gla_norm_g.reshape(depth, 1, -1)
    ps3 = pool_scale.reshape(depth, 1, d)

    xp, xs = x_prompt, x_sample
    h, *lg = _pre_call(xp, xs, mod_p, mod_s, 0, n1, w_alr_b, w_a2_b, ba3)

    gla_p, pool_p = [], []
    gla_s = pool_s = None
    for l in range(depth):
        p = _mm_in_call(h, w_in_t, l, dqk, d, w_pool.shape[1] * w_pool.shape[2])
        merged_p, s_p, b_p = _mix_prompt_call(p, lg, bp, lp, l, gn3, w_pool_b, ps3)
        merged_s, gla_s, pool_s = _mix_sample_call(p, lg, mp, ls, l, state_gla, state_pool, gn3, w_pool_b, ps3,
                                                   gla_s, pool_s)
        gla_p.append(s_p)
        pool_p.append(b_p)

        xp, xs, h2 = _mm_o_call(merged_p, merged_s, xp, xs, w_o_b, mod_p, mod_s, l, n2)
        act = _mm_gu_call(h2, w_gu, l)
        if l + 1 < depth:
            xp, xs, h, *lg = _mm_down_call(act, xp, xs, w_down_b, mod_p, mod_s, l, n1, l + 1, False, w_alr_b,
                                           w_a2_b, ba3)
        else:
            yp, ys = _mm_down_call(act, xp, xs, w_down_b, mod_p, mod_s, l, nf, 0, True, None, None, None)
    return yp, ys, jnp.stack(gla_p), jnp.stack(pool_p), gla_s, pool_s
```

```python
import functools

import jax
import jax.numpy as jnp
from jax import lax
from jax.experimental import pallas as pl
from jax.experimental.pallas import tpu as pltpu

F32 = jnp.float32
BF16 = jnp.bfloat16

N_HEADS = 4
POOL_WINDOWS = (2, 4, 8, 16)
POOL_BUF = 15
CARRY_ROWS = POOL_BUF + 1
GATE_NORMALIZER = 16.0
PAST_LEN = 16384
EPS = 1e-6
STABLE_BLOCK = 16
LOG_DECAY_TERMS = 2
CARRY_TERMS = 3
LANE = 128
MXU_DIM = 256
ROW_TILE_CAP = 1024
VMEM_LIMIT = 56 * 1024 * 1024


def _cparams(*sem):
    return pltpu.CompilerParams(dimension_semantics=sem, vmem_limit_bytes=VMEM_LIMIT)


def _largest_tile(n, cap):
    best = None
    for t in range(LANE, cap + 1, LANE):
        if n % t == 0:
            best = t
    assert best is not None, (n, cap)
    return best


def _largest_row_tile(m, cap):
    best = None
    for t in range(MXU_DIM, cap + 1, MXU_DIM):
        if m % t == 0:
            best = t
    assert best is not None, (m, cap)
    return best


def _dot(a, b):
    return jnp.dot(a, b, preferred_element_type=F32)


def _dot_nt(a, b):
    return lax.dot_general(a, b, (((1,), (1,)), ((), ())), preferred_element_type=F32)


def _dot_tn(a, b):
    return lax.dot_general(a, b, (((0,), (0,)), ((), ())), preferred_element_type=F32)


def _split(x, terms):
    parts = []
    for i in range(terms):
        p = x.astype(BF16)
        parts.append(p)
        if i + 1 < terms:
            x = x - p.astype(F32)
    return tuple(parts)


def _dot_parts(m, parts):
    out = _dot(m, parts[0])
    for p in parts[1:]:
        out = out + _dot(m, p)
    return out


def _column_tile(row):
    return jnp.broadcast_to(row, (LANE, row.shape[1])).T


def _sigmoid(x):
    return 1.0 / (1.0 + jnp.exp(-x))


def _silu(x):
    return x * _sigmoid(x)


def _log_sigmoid(x):
    return jnp.minimum(x, 0.0) - jnp.log(1.0 + jnp.exp(-jnp.abs(x)))


def _rms(x):
    return x * lax.rsqrt(jnp.mean(x * x, axis=-1, keepdims=True) + EPS)


def _rows_bcast(x, row, n):
    return jnp.broadcast_to(x[row:row + 1, :], (n, x.shape[1]))


def _cat(parts, axis):
    return parts[0] if len(parts) == 1 else jnp.concatenate(parts, axis=axis)


def _iota2(shape, dim):
    return lax.broadcasted_iota(jnp.int32, shape, dim)


def _blk(x, n):
    assert n & (n - 1) == 0
    return lax.shift_right_logical(x, jnp.int32(n.bit_length() - 1))


def _rem(x, n):
    assert n & (n - 1) == 0
    return x & jnp.int32(n - 1)


def _layer_spec(block, layer, tail):
    return pl.BlockSpec((None,) + block, lambda *g: (layer,) + tail(*g))


def _resident_spec(block, layer):
    zeros = (0,) * len(block)
    return pl.BlockSpec((None,) + block, lambda *g: (layer,) + zeros, pipeline_mode=pl.Buffered(1))


class _RowGrid:
    def __init__(self, xp_shape, xs_shape, rows):
        bp, lp, d = xp_shape
        bs, ls, _ = xs_shape
        assert lp % rows == 0 and rows % ls == 0 and bs % (rows // ls) == 0
        self.rows, self.d = rows, d
        self.tiles_per_seq = lp // rows
        self.n_p = bp * self.tiles_per_seq
        self.seqs_per_tile = rows // ls
        self.n_s = bs // self.seqs_per_tile
        self.ls = ls
        self.grid = (self.n_p + self.n_s,)

    def _ip(self, i):
        return jnp.minimum(i, self.n_p - 1)

    def _is(self, i):
        return jnp.maximum(i - self.n_p, 0)

    def x_specs(self):
        tps = self.tiles_per_seq
        return (pl.BlockSpec((1, self.rows, self.d), lambda i: (self._ip(i) // tps, self._ip(i) % tps, 0)),
                pl.BlockSpec((self.seqs_per_tile, self.ls, self.d), lambda i: (self._is(i), 0, 0)))

    def mod_specs(self, layer, j):
        tps = self.tiles_per_seq
        return (pl.BlockSpec((None, None, 1, self.d), lambda i: (layer, self._ip(i) // tps, 0, j)),
                pl.BlockSpec((None, self.seqs_per_tile, self.d), lambda i: (layer, self._is(i), j)))

    def rows_specs(self, width):
        return (pl.BlockSpec((self.rows, width), lambda i: (self._ip(i), 0)),
                pl.BlockSpec((self.rows, width), lambda i: (self._is(i), 0)))

    def joint_spec(self, width):
        return pl.BlockSpec((self.rows, width), lambda i: (i, 0))

    def run(self, body, prompt_refs, sample_refs):
        i = pl.program_id(0)

        @pl.when(i < self.n_p)
        def _():
            body(*prompt_refs)

        @pl.when(i >= self.n_p)
        def _():
            body(*sample_refs)


def _rows2d(x_ref):
    bb, tt, d = x_ref.shape
    return x_ref[...].reshape(bb * tt, d)


def _per_row(mod_ref, rows):
    m = mod_ref[...]
    n = m.shape[0]
    if n == 1:
        return m
    return _cat([_rows_bcast(m, s, rows // n) for s in range(n)], 0)


def _modulated_norm(x, g, sc, sh):
    return (_rms(x) * g * (1.0 + sc) + sh).astype(BF16)


def _ada_kernel(c_ref, w_ref, b_ref, o_ref):
    a = _silu(c_ref[...]).astype(BF16)
    o_ref[...] = _dot(a, w_ref[...].astype(BF16)) + b_ref[...]


def _ada_call(c_all, w_ada, b_ada):
    depth, d, n6 = w_ada.shape
    bp = c_all.shape[0]
    tn = _largest_tile(n6, 2048)
    return pl.pallas_call(
        _ada_kernel,
        grid=(depth, n6 // tn),
        in_specs=[
            pl.BlockSpec((bp, d), lambda l, j: (0, 0)),
            pl.BlockSpec((None, d, tn), lambda l, j: (l, 0, j)),
            pl.BlockSpec((None, 1, tn), lambda l, j: (l, 0, j)),
        ],
        out_specs=pl.BlockSpec((None, bp, tn), lambda l, j: (l, 0, j)),
        out_shape=jax.ShapeDtypeStruct((depth, bp, n6), F32),
        compiler_params=_cparams("arbitrary", "arbitrary"),
        name="ada_mod",
    )(c_all, w_ada, b_ada.reshape(depth, 1, n6))


def _log_decay_terms(h, wl_ref, wa2_ref, ba_ref):
    alr = _dot_nt(h, wl_ref[...])
    gk = _dot(alr.astype(BF16), wa2_ref[...]) + ba_ref[...]
    return _split(_log_sigmoid(gk) * (1.0 / GATE_NORMALIZER), LOG_DECAY_TERMS)


def _gate_specs(rg, layer, d, dqk):
    const = lambda i: (0, 0)
    ins = [_layer_spec((LANE, d), layer, const), _layer_spec((LANE, dqk), layer, const),
           _layer_spec((1, dqk), layer, const)]
    outs = [rg.joint_spec(dqk)] * LOG_DECAY_TERMS
    return ins, outs


def _pre_kernel(xp_ref, xs_ref, scp_ref, scs_ref, shp_ref, shs_ref, g_ref, wl_ref, wa2_ref, ba_ref,
                h_ref, *lg_refs, rg):
    def body(x_ref, sc_ref, sh_ref):
        h = _modulated_norm(_rows2d(x_ref), g_ref[...], _per_row(sc_ref, rg.rows), _per_row(sh_ref, rg.rows))
        h_ref[...] = h
        for ref, term in zip(lg_refs, _log_decay_terms(h, wl_ref, wa2_ref, ba_ref)):
            ref[...] = term

    rg.run(body, (xp_ref, scp_ref, shp_ref), (xs_ref, scs_ref, shs_ref))


def _pre_call(xp, xs, mod_p, mod_s, layer, norm_g, w_alr_t, w_a2, b_a, rows=512):
    rg = _RowGrid(xp.shape, xs.shape, rows)
    d = rg.d
    dqk = w_a2.shape[2]
    m_total = xp.shape[0] * xp.shape[1] + xs.shape[0] * xs.shape[1]
    gate_in, gate_out = _gate_specs(rg, layer, d, dqk)
    return pl.pallas_call(
        functools.partial(_pre_kernel, rg=rg),
        grid=rg.grid,
        in_specs=[*rg.x_specs(), *rg.mod_specs(layer, 1), *rg.mod_specs(layer, 0),
                  _layer_spec((1, d), layer, lambda i: (0, 0)), *gate_in],
        out_specs=[rg.joint_spec(d), *gate_out],
        out_shape=[jax.ShapeDtypeStruct((m_total, d), BF16)]
        + [jax.ShapeDtypeStruct((m_total, dqk), BF16)] * LOG_DECAY_TERMS,
        compiler_params=_cparams("arbitrary"),
        name="prologue",
    )(xp, xs, mod_p, mod_s, mod_p, mod_s, norm_g, w_alr_t, w_a2, b_a)


def _mm_in_kernel(a_ref, w_ref, p_ref, wb_ref, *, acts, scale):
    j = pl.program_id(0)

    @pl.when(pl.program_id(1) == 0)
    def _():
        wb_ref[...] = w_ref[...].astype(BF16)

    fns = {"id": lambda y: y, "scale": lambda y: y * scale, "silu": _silu, "sigmoid": _sigmoid}
    for name in sorted(set(acts)):
        cond = functools.reduce(jnp.logical_or, [j == jj for jj, a in enumerate(acts) if a == name])

        @pl.when(cond)
        def _(fn=fns[name]):
            p_ref[...] = fn(_dot_nt(a_ref[...], wb_ref[...])).astype(BF16)


def _mm_in_call(h, w_in_t, layer, dqk, d_model, d_pool):
    m, k = h.shape
    tm = _largest_row_tile(m, ROW_TILE_CAP)
    starts = _p_columns(dqk, d_model, d_pool)
    n_main = starts[-1] + d_model
    tn = LANE
    for cand in range(LANE, 1024 + 1, LANE):
        if all(s % cand == 0 for s in starts + (n_main,)):
            tn = cand
    kinds = ("scale", "id", "id", "silu", "id", "sigmoid", "sigmoid")
    bounds = starts[1:] + (n_main,)
    acts = []
    for jj in range(n_main // tn):
        acts.append(kinds[next(i for i, e in enumerate(bounds) if jj * tn < e)])
    dk = dqk // N_HEADS
    return pl.pallas_call(
        functools.partial(_mm_in_kernel, acts=tuple(acts), scale=dk ** -0.5),
        grid=(n_main // tn, m // tm),
        in_specs=[
            pl.BlockSpec((tm, k), lambda j, i: (i, 0)),
            _layer_spec((tn, k), layer, lambda j, i: (j, 0)),
        ],
        out_specs=pl.BlockSpec((tm, tn), lambda j, i: (i, j)),
        out_shape=jax.ShapeDtypeStruct((m, n_main), BF16),
        scratch_shapes=[pltpu.VMEM((tn, k), BF16)],
        compiler_params=_cparams("arbitrary", "arbitrary"),
        name="in_proj",
    )(h, w_in_t)


def _gla_consts(t, seq):
    nb = min(STABLE_BLOCK, seq)
    nblk = t // nb
    levels = []
    n = nb * 2
    while n <= seq:
        levels.append(n)
        n *= 2
    nkinds = 3 + len(levels)
    rows = -(-nkinds * nblk // 16) * 16
    row = _iota2((t, t), 0)
    col = _iota2((t, t), 1)
    rowv = _iota2((t, 1), 0)
    same0 = _blk(row, nb) == _blk(col, nb)
    r = _iota2((rows, t), 0)
    u = _iota2((rows, t), 1)
    kind = _blk(r, nblk)
    j = _rem(r, nblk)
    start = j * nb
    seq_start = start & jnp.int32(-seq)
    sel = (kind == 0) & (u >= seq_start) & (u < start)
    sel |= (kind == 1) & (u >= start) & (u < seq_start + seq)
    for li, n in enumerate(levels):
        mid = (start & jnp.int32(-n)) + n // 2
        between = ((u >= mid) & (u < start)) | ((u >= start) & (u < mid))
        sel |= (kind == 2 + li) & between
    sel |= (kind == nkinds - 1) & (_blk(u, seq) == j)
    return {
        "t": t, "seq": seq, "nb": nb, "nblk": nblk, "total_kind": nkinds - 1,
        "tri_local": (same0 & (col <= row)).astype(BF16),
        "sel": jnp.where(sel, 1.0, 0.0).astype(BF16),
        "mask0": same0 & (col <= row),
        "levels": [(_rem(rowv, n) >= n // 2, _blk(row, n) == _blk(col, n)) for n in levels],
    }


def _gla_operands(q, k, e0, g, c):
    nb, nblk = c["nb"], c["nblk"]

    def per_block(kind):
        return _cat([_rows_bcast(g, kind * nblk + j, nb) for j in range(nblk)], 0)

    qe0 = q * jnp.exp(e0)
    ke0 = k * jnp.exp(-e0)
    pairs = [(qe0.astype(BF16), ke0.astype(BF16))]
    for li, (second, _) in enumerate(c["levels"]):
        f = per_block(2 + li)
        pairs.append((jnp.where(second, qe0 * f, 0.0).astype(BF16), jnp.where(second, 0.0, ke0 * f).astype(BF16)))
    qe = qe0 if nb == c["seq"] else qe0 * per_block(0)
    kd = ke0 * per_block(1)
    return pairs, qe.astype(BF16), kd.astype(BF16)


def _gla_scores(pairs, c):
    att = jnp.where(c["mask0"], _dot_nt(*pairs[0]), 0.0)
    for (ql, kl), (_, same) in zip(pairs[1:], c["levels"]):
        att = att + jnp.where(same, _dot_nt(ql, kl), 0.0)
    return att


def _lane_tile(x, width):
    return _cat([x] * (width // x.shape[1]), 1)


def _pool_delta(u, u_parts, carry_parts, w, pos0, seq):
    t = u.shape[0]
    npiece = t // seq
    row = _iota2((t, t), 0)
    col = _iota2((t, t), 1)
    band = ((_blk(row, seq) == _blk(col, seq)) & (col <= row) & (col > row - w)).astype(BF16)
    rc = _iota2((t, npiece * CARRY_ROWS), 0)
    cc = _iota2((t, npiece * CARRY_ROWS), 1)
    cband = ((_blk(rc, seq) == _blk(cc, CARRY_ROWS))
             & (_rem(cc, CARRY_ROWS) > _rem(rc, seq) + CARRY_ROWS - w)).astype(BF16)
    s = _dot_parts(band, u_parts) + _dot_parts(cband, carry_parts)
    cnt = jnp.minimum(pos0 + 1, w).astype(F32)
    return s / cnt - u


def _mix_merge(o, silu_og, sig_ga, sig_gb, pooled, gn, ps):
    y_a = _rms(o) * gn * silu_og.astype(F32)
    return sig_ga.astype(F32) * y_a + sig_gb.astype(F32) * (pooled * ps)


def _p_columns(dqk, d_model, d_pool):
    q0, k0, v0 = 0, dqk, 2 * dqk
    og0 = v0 + d_model
    u0 = og0 + d_model
    ga0 = u0 + d_pool
    gb0 = ga0 + d_model
    return q0, k0, v0, og0, u0, ga0, gb0


def _mix_prompt_kernel(p_ref, *refs, dqk, d_model):
    lg_refs = refs[:LOG_DECAY_TERMS]
    gn_ref, wp_ref, ps_ref, m_ref, s_ref, pool_out_ref, car_ref = refs[LOG_DECAY_TERMS:]
    c = pl.program_id(1)
    t = p_ref.shape[0]
    groups, pin, dv = wp_ref.shape
    dk = dqk // N_HEADS
    q0, k0, v0, og0, u0, ga0, gb0 = _p_columns(dqk, d_model, groups * pin)

    @pl.when(c == 0)
    def _():
        s_ref[...] = jnp.zeros_like(s_ref)
        car_ref[...] = jnp.zeros_like(car_ref)

    consts = _gla_consts(t, t)
    loga = tuple(ref[...] for ref in lg_refs)
    e0_all = _dot_parts(consts["tri_local"], loga)
    g_all = jnp.exp(_dot_parts(consts["sel"], loga))
    tot = consts["total_kind"] * consts["nblk"]
    decay_all = _column_tile(g_all[tot:tot + 1, :])
    pos = c * t + _iota2((t, 1), 0)
    gn = gn_ref[...]

    heads = range(N_HEADS)
    deltas = []
    for h in heads:
        ub = p_ref[:, u0 + h * pin:u0 + (h + 1) * pin]
        deltas.append(_pool_delta(ub.astype(F32), (ub,), (car_ref[:, h * pin:(h + 1) * pin],), POOL_WINDOWS[h],
                                  pos, t).astype(BF16))
    ops = []
    for h in heads:
        q = p_ref[:, q0 + h * dk:q0 + (h + 1) * dk].astype(F32)
        k = p_ref[:, k0 + h * dk:k0 + (h + 1) * dk].astype(F32)
        ops.append(_gla_operands(q, k, e0_all[:, h * dk:(h + 1) * dk], g_all[:, h * dk:(h + 1) * dk], consts))
    pooled = [_dot(deltas[h], wp_ref[h]) for h in heads]
    atts = [_gla_scores(ops[h][0], consts).astype(BF16) for h in heads]
    outs = []
    for h in heads:
        _, qe, kd = ops[h]
        vb = p_ref[:, v0 + h * dv:v0 + (h + 1) * dv]
        s = s_ref[h]
        outs.append(_dot(atts[h], vb) + _dot(qe, s.astype(BF16)))
        decay = decay_all[h * dk:(h + 1) * dk]
        s_ref[h] = _lane_tile(decay, dv) * s + _dot_tn(kd, vb)

    for h in heads:
        merged = _mix_merge(outs[h], p_ref[:, og0 + h * dv:og0 + (h + 1) * dv],
                            p_ref[:, ga0 + h * dv:ga0 + (h + 1) * dv], p_ref[:, gb0 + h * dv:gb0 + (h + 1) * dv],
                            pooled[h], gn, ps_ref[:, h * dv:(h + 1) * dv])
        m_ref[:, h * dv:(h + 1) * dv] = merged.astype(BF16)

    u_tail = p_ref[t - CARRY_ROWS:, u0:u0 + groups * pin]
    car_ref[...] = u_tail

    @pl.when(c == pl.num_programs(1) - 1)
    def _():
        pool_out_ref[...] = u_tail.astype(F32)[CARRY_ROWS - POOL_BUF:]


def _mix_prompt_call(p, lg, batch, seq_len, layer, gla_norm_g, w_pool, pool_scale, chunk=128):
    dqk = lg[0].shape[1]
    dk = dqk // N_HEADS
    groups, pin, dv = w_pool.shape[1:]
    d_model = pool_scale.shape[2]
    n_p = p.shape[1]
    nc = seq_len // chunk
    const = lambda b, c: (0, 0)
    in_specs = [
        pl.BlockSpec((chunk, n_p), lambda b, c: (b * nc + c, 0)),
        *[pl.BlockSpec((chunk, dqk), lambda b, c: (b * nc + c, 0)) for _ in lg],
        _layer_spec((1, dv), layer, const),
        _layer_spec((groups, pin, dv), layer, lambda b, c: (0, 0, 0)),
        _layer_spec((1, d_model), layer, const),
    ]
    out_specs = [
        pl.BlockSpec((chunk, d_model), lambda b, c: (b * nc + c, 0)),
        pl.BlockSpec((None, N_HEADS, dk, dv), lambda b, c: (b, 0, 0, 0)),
        pl.BlockSpec((None, POOL_BUF, groups * pin), lambda b, c: (b, 0, 0)),
    ]
    out_shape = [
        jax.ShapeDtypeStruct((batch * seq_len, d_model), BF16),
        jax.ShapeDtypeStruct((batch, N_HEADS, dk, dv), F32),
        jax.ShapeDtypeStruct((batch, POOL_BUF, groups * pin), F32),
    ]
    return pl.pallas_call(
        functools.partial(_mix_prompt_kernel, dqk=dqk, d_model=d_model),
        grid=(batch, nc),
        in_specs=in_specs,
        out_specs=out_specs,
        out_shape=out_shape,
        scratch_shapes=[pltpu.VMEM((CARRY_ROWS, groups * pin), BF16)],
        compiler_params=_cparams("arbitrary", "arbitrary"),
        name="mix_prompt",
    )(p, *lg, gla_norm_g, w_pool, pool_scale)


def _mix_sample_kernel(p_ref, *refs, seq, dqk, d_model, n_alias):
    lg_refs = refs[:LOG_DECAY_TERMS]
    s0_ref, buf_ref, gn_ref, wp_ref, ps_ref = refs[LOG_DECAY_TERMS:LOG_DECAY_TERMS + 5]
    m_ref, s_out_ref, pool_out_ref, car_ref = refs[LOG_DECAY_TERMS + 5 + n_alias:]
    nseq = s0_ref.shape[0]
    t = nseq * seq
    groups, pin, dv = wp_ref.shape
    dk = dqk // N_HEADS
    q0, k0, v0, og0, u0, ga0, gb0 = _p_columns(dqk, d_model, groups * pin)

    car_ref[...] = jnp.zeros_like(car_ref)
    for j in range(nseq):
        car_ref[j * CARRY_ROWS + 1:(j + 1) * CARRY_ROWS, :] = buf_ref[j]

    consts = _gla_consts(t, seq)
    loga = tuple(ref[...] for ref in lg_refs)
    e0_all = _dot_parts(consts["tri_local"], loga)
    g_all = jnp.exp(_dot_parts(consts["sel"], loga))
    tot = consts["total_kind"] * consts["nblk"]
    decay_all = [_column_tile(g_all[tot + j:tot + j + 1, :]) for j in range(nseq)]
    pos = PAST_LEN + _rem(_iota2((t, 1), 0), seq)
    keep = POOL_BUF - seq
    gn = gn_ref[...]

    u_all = p_ref[:, u0:u0 + groups * pin].astype(F32)
    heads = range(N_HEADS)
    ops = []
    for h in heads:
        q = p_ref[:, q0 + h * dk:q0 + (h + 1) * dk].astype(F32)
        k = p_ref[:, k0 + h * dk:k0 + (h + 1) * dk].astype(F32)
        ops.append(_gla_operands(q, k, e0_all[:, h * dk:(h + 1) * dk], g_all[:, h * dk:(h + 1) * dk], consts))
    atts = [_gla_scores(ops[h][0], consts).astype(BF16) for h in heads]
    outs = []
    for h in heads:
        _, qeb, kdb = ops[h]
        vb = p_ref[:, v0 + h * dv:v0 + (h + 1) * dv]
        o_intra = _dot(atts[h], vb)
        o_parts = []
        for j in range(nseq):
            s0 = s0_ref[j, h]
            lo, hi = j * seq, (j + 1) * seq
            o_parts.append(o_intra[lo:hi] + _dot(qeb[lo:hi], s0.astype(BF16)))
            decay = decay_all[j][h * dk:(h + 1) * dk]
            s_out_ref[j, h] = _lane_tile(decay, dv) * s0 + _dot_tn(kdb[lo:hi], vb[lo:hi])
        outs.append(_cat(o_parts, 0))

    for h in heads:
        d = _pool_delta(u_all[:, h * pin:(h + 1) * pin], (p_ref[:, u0 + h * pin:u0 + (h + 1) * pin],),
                        _split(car_ref[:, h * pin:(h + 1) * pin], CARRY_TERMS), POOL_WINDOWS[h], pos, seq)
        merged = _mix_merge(outs[h], p_ref[:, og0 + h * dv:og0 + (h + 1) * dv],
                            p_ref[:, ga0 + h * dv:ga0 + (h + 1) * dv], p_ref[:, gb0 + h * dv:gb0 + (h + 1) * dv],
                            _dot(d.astype(BF16), wp_ref[h]), gn, ps_ref[:, h * dv:(h + 1) * dv])
        m_ref[:, h * dv:(h + 1) * dv] = merged.astype(BF16)

    for j in range(nseq):
        pool_out_ref[j, 0:keep, :] = buf_ref[j, seq:POOL_BUF, :]
        pool_out_ref[j, keep:POOL_BUF, :] = u_all[j * seq:(j + 1) * seq]


def _mix_sample_call(p, lg, row_off, seq, layer, state_gla, state_pool, gla_norm_g, w_pool,
                     pool_scale, prev_gla, prev_pool, nseq=4):
    depth, bs, nh, dk, dv = state_gla.shape
    dqk = lg[0].shape[1]
    groups, pin, _ = w_pool.shape[1:]
    d_model = pool_scale.shape[2]
    n_p = p.shape[1]
    assert seq <= POOL_BUF and seq % 8 == 0
    t = nseq * seq
    off = row_off // t
    const = lambda i: (0, 0)
    in_specs = [
        pl.BlockSpec((t, n_p), lambda i: (off + i, 0)),
        *[pl.BlockSpec((t, dqk), lambda i: (off + i, 0)) for _ in lg],
        pl.BlockSpec((None, nseq, nh, dk, dv), lambda i: (layer, i, 0, 0, 0)),
        pl.BlockSpec((None, nseq, POOL_BUF, groups * pin), lambda i: (layer, i, 0, 0)),
        _layer_spec((1, dv), layer, const),
        _layer_spec((groups, pin, dv), layer, lambda i: (0, 0, 0)),
        _layer_spec((1, d_model), layer, const),
    ]
    args = [p, *lg, state_gla, state_pool, gla_norm_g, w_pool, pool_scale]
    aliases = {}
    n_alias = 0
    if prev_gla is not None:
        in_specs += [pl.BlockSpec(memory_space=pl.ANY), pl.BlockSpec(memory_space=pl.ANY)]
        aliases = {len(args): 1, len(args) + 1: 2}
        args += [prev_gla, prev_pool]
        n_alias = 2
    out_specs = [
        pl.BlockSpec((t, d_model), lambda i: (i, 0)),
        pl.BlockSpec((None, nseq, nh, dk, dv), lambda i: (layer, i, 0, 0, 0)),
        pl.BlockSpec((None, nseq, POOL_BUF, groups * pin), lambda i: (layer, i, 0, 0)),
    ]
    out_shape = [
        jax.ShapeDtypeStruct((bs * seq, d_model), BF16),
        jax.ShapeDtypeStruct(state_gla.shape, F32),
        jax.ShapeDtypeStruct(state_pool.shape, F32),
    ]
    return pl.pallas_call(
        functools.partial(_mix_sample_kernel, seq=seq, dqk=dqk, d_model=d_model, n_alias=n_alias),
        grid=(bs // nseq,),
        in_specs=in_specs,
        out_specs=out_specs,
        out_shape=out_shape,
        scratch_shapes=[pltpu.VMEM((nseq * CARRY_ROWS, groups * pin), F32)],
        input_output_aliases=aliases,
        compiler_params=_cparams("arbitrary"),
        name="mix_sample",
    )(*args)


def _mm_o_kernel(ap_ref, as_ref, xp_ref, xs_ref, g1p_ref, g1s_ref, scp_ref, scs_ref, shp_ref, shs_ref,
                 w_ref, ng_ref, x1p_ref, x1s_ref, h_ref, wb_ref, *, rg):
    @pl.when(pl.program_id(0) == 0)
    def _():
        wb_ref[...] = w_ref[...].astype(BF16)

    def body(a_ref, x_ref, g1_ref, sc_ref, sh_ref, x1_ref):
        rows = rg.rows
        x1 = _rows2d(x_ref) + _per_row(g1_ref, rows) * _dot(a_ref[...], wb_ref[...])
        x1_ref[...] = x1.reshape(x1_ref.shape)
        h_ref[...] = _modulated_norm(x1, ng_ref[...], _per_row(sc_ref, rows), _per_row(sh_ref, rows))

    rg.run(body, (ap_ref, xp_ref, g1p_ref, scp_ref, shp_ref, x1p_ref),
           (as_ref, xs_ref, g1s_ref, scs_ref, shs_ref, x1s_ref))


def _mm_o_call(merged_p, merged_s, xp, xs, w_o, mod_p, mod_s, layer, norm2_g, rows=256):
    rg = _RowGrid(xp.shape, xs.shape, rows)
    d = rg.d
    m_total = xp.shape[0] * xp.shape[1] + xs.shape[0] * xs.shape[1]
    xsp = rg.x_specs()
    return pl.pallas_call(
        functools.partial(_mm_o_kernel, rg=rg),
        grid=rg.grid,
        in_specs=[*rg.rows_specs(d), *xsp, *rg.mod_specs(layer, 2), *rg.mod_specs(layer, 4),
                  *rg.mod_specs(layer, 3), _resident_spec((d, d), layer),
                  _layer_spec((1, d), layer, lambda i: (0, 0))],
        out_specs=[*xsp, rg.joint_spec(d)],
        out_shape=[jax.ShapeDtypeStruct(xp.shape, F32), jax.ShapeDtypeStruct(xs.shape, F32),
                   jax.ShapeDtypeStruct((m_total, d), BF16)],
        scratch_shapes=[pltpu.VMEM((d, d), BF16)],
        compiler_params=_cparams("arbitrary"),
        name="out_proj",
    )(merged_p, merged_s, xp, xs, mod_p, mod_s, mod_p, mod_s, mod_p, mod_s, w_o, norm2_g)


def _mm_gu_kernel(a_ref, wg_ref, wu_ref, o_ref, wgb_ref, wub_ref):
    @pl.when(pl.program_id(1) == 0)
    def _():
        wgb_ref[...] = wg_ref[...].astype(BF16)
        wub_ref[...] = wu_ref[...].astype(BF16)

    a = a_ref[...]
    o_ref[...] = (_silu(_dot(a, wgb_ref[...])) * _dot(a, wub_ref[...])).astype(BF16)


def _mm_gu_call(h, w_gu, layer):
    m, k = h.shape
    tm = _largest_row_tile(m, ROW_TILE_CAP)
    dff = w_gu.shape[2] // 2
    tn = _largest_tile(dff, 512)
    nj = dff // tn
    return pl.pallas_call(
        _mm_gu_kernel,
        grid=(nj, m // tm),
        in_specs=[
            pl.BlockSpec((tm, k), lambda j, i: (i, 0)),
            _layer_spec((k, tn), layer, lambda j, i: (0, j)),
            _layer_spec((k, tn), layer, lambda j, i: (0, nj + j)),
        ],
        out_specs=pl.BlockSpec((tm, tn), lambda j, i: (i, j)),
        out_shape=jax.ShapeDtypeStruct((m, dff), BF16),
        scratch_shapes=[pltpu.VMEM((k, tn), BF16), pltpu.VMEM((k, tn), BF16)],
        compiler_params=_cparams("arbitrary", "arbitrary"),
        name="gate_up",
    )(h, w_gu, w_gu)


def _mm_down_kernel(a_ref, xp_ref, xs_ref, g2p_ref, g2s_ref, w_ref, ng_ref, *rest, rg, final):
    if final:
        yp_ref, ys_ref = rest

        def body(x_ref, g2_ref, y_ref):
            x2 = _rows2d(x_ref) + _per_row(g2_ref, rg.rows) * _dot(a_ref[...], w_ref[...])
            y_ref[...] = (_rms(x2) * ng_ref[...]).reshape(y_ref.shape)

        rg.run(body, (xp_ref, g2p_ref, yp_ref), (xs_ref, g2s_ref, ys_ref))
    else:
        scp_ref, scs_ref, shp_ref, shs_ref, wl_ref, wa2_ref, ba_ref, x2p_ref, x2s_ref, h_ref, *lg_refs = rest

        def body(x_ref, g2_ref, sc_ref, sh_ref, x2_ref):
            rows = rg.rows
            x2 = _rows2d(x_ref) + _per_row(g2_ref, rows) * _dot(a_ref[...], w_ref[...])
            x2_ref[...] = x2.reshape(x2_ref.shape)
            h = _modulated_norm(x2, ng_ref[...], _per_row(sc_ref, rows), _per_row(sh_ref, rows))
            h_ref[...] = h
            for ref, term in zip(lg_refs, _log_decay_terms(h, wl_ref, wa2_ref, ba_ref)):
                ref[...] = term

        rg.run(body, (xp_ref, g2p_ref, scp_ref, shp_ref, x2p_ref), (xs_ref, g2s_ref, scs_ref, shs_ref, x2s_ref))


def _mm_down_call(act, xp, xs, w_down, mod_p, mod_s, layer, norm_g, norm_layer, final, w_alr_t, w_a2, b_a,
                  rows=256):
    rg = _RowGrid(xp.shape, xs.shape, rows)
    d = rg.d
    dff = w_down.shape[1]
    m_total = act.shape[0]
    xsp = rg.x_specs()
    in_specs = [rg.joint_spec(dff), *xsp, *rg.mod_specs(layer, 5), _resident_spec((dff, d), layer),
                _layer_spec((1, d), norm_layer, lambda i: (0, 0))]
    args = [act, xp, xs, mod_p, mod_s, w_down, norm_g]
    x_shapes = [jax.ShapeDtypeStruct(xp.shape, F32), jax.ShapeDtypeStruct(xs.shape, F32)]
    if final:
        out_specs, out_shape = [*xsp], x_shapes
    else:
        dqk = w_a2.shape[2]
        gate_in, gate_out = _gate_specs(rg, layer + 1, d, dqk)
        in_specs += [*rg.mod_specs(layer + 1, 1), *rg.mod_specs(layer + 1, 0), *gate_in]
        args += [mod_p, mod_s, mod_p, mod_s, w_alr_t, w_a2, b_a]
        out_specs = [*xsp, rg.joint_spec(d), *gate_out]
        out_shape = x_shapes + [jax.ShapeDtypeStruct((m_total, d), BF16)] \
            + [jax.ShapeDtypeStruct((m_total, dqk), BF16)] * LOG_DECAY_TERMS
    return pl.pallas_call(
        functools.partial(_mm_down_kernel, rg=rg, final=final),
        grid=rg.grid,
        in_specs=in_specs,
        out_specs=out_specs,
        out_shape=out_shape,
        compiler_params=_cparams("arbitrary"),
        name="down_proj",
    )(*args)


def kernel(x_prompt, x_sample, state_gla, state_pool, c_prompt, c_sample, w_ada, b_ada, norm1_g, w_in, w_a2, b_a,
           gla_norm_g, w_pool, pool_scale, w_o, norm2_g, w_gu, w_down, final_norm_g):
    depth, d, n6 = w_ada.shape
    bp, lp, _ = x_prompt.shape
    bs, ls, _ = x_sample.shape
    mp = bp * lp
    rank, dqk = w_a2.shape[1:]
    n_main = w_in.shape[2] - rank

    c_all = jnp.concatenate([c_sample, c_prompt], axis=0)
    c_all = jnp.pad(c_all, ((0, (-c_all.shape[0]) % 8), (0, 0)))
    mod_s = _ada_call(c_all, w_ada, b_ada)
    mod_p = mod_s[:, bs:bs + bp].reshape(depth, bp, 1, n6)

    w_in_t = jnp.swapaxes(w_in, 1, 2)
    w_alr_b = jnp.pad(w_in_t[:, n_main:, :], ((0, 0), (0, LANE - rank), (0, 0))).astype(BF16)
    w_down_b = w_down.astype(BF16)
    w_a2_b = jnp.pad(w_a2, ((0, 0), (0, LANE - rank), (0, 0))).astype(BF16)
    w_pool_b = w_pool.astype(BF16)
    n1 = norm1_g.reshape(depth, 1, d)
    n2 = norm2_g.reshape(depth, 1, d)
    nf = final_norm_g.reshape(1, 1, d)
    ba3 = b_a.reshape(depth, 1, dqk)
    gn3 = gla_norm_g.reshape(depth, 1, -1)
    ps3 = pool_scale.reshape(depth, 1, d)

    xp, xs = x_prompt, x_sample
    h, *lg = _pre_call(xp, xs, mod_p, mod_s, 0, n1, w_alr_b, w_a2_b, ba3)

    gla_p, pool_p = [], []
    gla_s = pool_s = None
    for l in range(depth):
        p = _mm_in_call(h, w_in_t, l, dqk, d, w_pool.shape[1] * w_pool.shape[2])
        merged_p, s_p, b_p = _mix_prompt_call(p, lg, bp, lp, l, gn3, w_pool_b, ps3)
        merged_s, gla_s, pool_s = _mix_sample_call(p, lg, mp, ls, l, state_gla, state_pool, gn3, w_pool_b, ps3,
                                                   gla_s, pool_s)
        gla_p.append(s_p)
        pool_p.append(b_p)

        xp, xs, h2 = _mm_o_call(merged_p, merged_s, xp, xs, w_o, mod_p, mod_s, l, n2)
        act = _mm_gu_call(h2, w_gu, l)
        if l + 1 < depth:
            xp, xs, h, *lg = _mm_down_call(act, xp, xs, w_down_b, mod_p, mod_s, l, n1, l + 1, False, w_alr_b,
                                           w_a2_b, ba3)
        else:
            yp, ys = _mm_down_call(act, xp, xs, w_down_b, mod_p, mod_s, l, nf, 0, True, None, None, None)
    return yp, ys, jnp.stack(gla_p), jnp.stack(pool_p), gla_s, pool_s
```

```python
import functools

import jax
import jax.numpy as jnp
from jax import lax
from jax.experimental import pallas as pl
from jax.experimental.pallas import tpu as pltpu

F32 = jnp.float32
BF16 = jnp.bfloat16

N_HEADS = 4
POOL_WINDOWS = (2, 4, 8, 16)
POOL_BUF = 15
CARRY_ROWS = POOL_BUF + 1
GATE_NORMALIZER = 16.0
PAST_LEN = 16384
EPS = 1e-6
STABLE_BLOCK = 16
LOG_DECAY_TERMS = 2
CARRY_TERMS = 3
LANE = 128
MXU_DIM = 256
ROW_TILE_CAP = 1024
VMEM_LIMIT = 56 * 1024 * 1024


def _cparams(*sem):
    return pltpu.CompilerParams(dimension_semantics=sem, vmem_limit_bytes=VMEM_LIMIT)


def _largest_tile(n, cap):
    best = None
    for t in range(LANE, cap + 1, LANE):
        if n % t == 0:
            best = t
    assert best is not None, (n, cap)
    return best


def _largest_row_tile(m, cap):
    best = None
    for t in range(MXU_DIM, cap + 1, MXU_DIM):
        if m % t == 0:
            best = t
    assert best is not None, (m, cap)
    return best


def _dot(a, b):
    return jnp.dot(a, b, preferred_element_type=F32)


def _dot_nt(a, b):
    return lax.dot_general(a, b, (((1,), (1,)), ((), ())), preferred_element_type=F32)


def _dot_tn(a, b):
    return lax.dot_general(a, b, (((0,), (0,)), ((), ())), preferred_element_type=F32)


def _split(x, terms):
    parts = []
    for i in range(terms):
        p = x.astype(BF16)
        parts.append(p)
        if i + 1 < terms:
            x = x - p.astype(F32)
    return tuple(parts)


def _dot_parts(m, parts):
    out = _dot(m, parts[0])
    for p in parts[1:]:
        out = out + _dot(m, p)
    return out


def _column_tile(row):
    return jnp.broadcast_to(row, (LANE, row.shape[1])).T


def _sigmoid(x):
    return 1.0 / (1.0 + jnp.exp(-x))


def _silu(x):
    return x * _sigmoid(x)


def _log_sigmoid(x):
    return jnp.minimum(x, 0.0) - jnp.log(1.0 + jnp.exp(-jnp.abs(x)))


def _rms(x):
    return x * lax.rsqrt(jnp.mean(x * x, axis=-1, keepdims=True) + EPS)


def _rows_bcast(x, row, n):
    return jnp.broadcast_to(x[row:row + 1, :], (n, x.shape[1]))


def _cat(parts, axis):
    return parts[0] if len(parts) == 1 else jnp.concatenate(parts, axis=axis)


def _iota2(shape, dim):
    return lax.broadcasted_iota(jnp.int32, shape, dim)


def _blk(x, n):
    assert n & (n - 1) == 0
    return lax.shift_right_logical(x, jnp.int32(n.bit_length() - 1))


def _rem(x, n):
    assert n & (n - 1) == 0
    return x & jnp.int32(n - 1)


def _layer_spec(block, layer, tail):
    return pl.BlockSpec((None,) + block, lambda *g: (layer,) + tail(*g))


def _resident_spec(block, layer=None):
    zeros = (0,) * len(block)
    if layer is None:
        return pl.BlockSpec(block, lambda *g: zeros, pipeline_mode=pl.Buffered(1))
    return pl.BlockSpec((None,) + block, lambda *g: (layer,) + zeros, pipeline_mode=pl.Buffered(1))


class _RowGrid:
    def __init__(self, xp_shape, xs_shape, rows):
        bp, lp, d = xp_shape
        bs, ls, _ = xs_shape
        assert lp % rows == 0 and rows % ls == 0 and bs % (rows // ls) == 0
        self.rows, self.d = rows, d
        self.tiles_per_seq = lp // rows
        self.n_p = bp * self.tiles_per_seq
        self.seqs_per_tile = rows // ls
        self.n_s = bs // self.seqs_per_tile
        self.ls = ls
        self.grid = (self.n_p + self.n_s,)

    def _ip(self, i):
        return jnp.minimum(i, self.n_p - 1)

    def _is(self, i):
        return jnp.maximum(i - self.n_p, 0)

    def x_specs(self):
        tps = self.tiles_per_seq
        return (pl.BlockSpec((1, self.rows, self.d), lambda i: (self._ip(i) // tps, self._ip(i) % tps, 0)),
                pl.BlockSpec((self.seqs_per_tile, self.ls, self.d), lambda i: (self._is(i), 0, 0)))

    def mod_specs(self, layer, j):
        tps = self.tiles_per_seq
        return (pl.BlockSpec((None, None, 1, self.d), lambda i: (layer, self._ip(i) // tps, 0, j)),
                pl.BlockSpec((None, self.seqs_per_tile, self.d), lambda i: (layer, self._is(i), j)))

    def rows_specs(self, width):
        return (pl.BlockSpec((self.rows, width), lambda i: (self._ip(i), 0)),
                pl.BlockSpec((self.rows, width), lambda i: (self._is(i), 0)))

    def joint_spec(self, width):
        return pl.BlockSpec((self.rows, width), lambda i: (i, 0))

    def run(self, body, prompt_refs, sample_refs):
        i = pl.program_id(0)

        @pl.when(i < self.n_p)
        def _():
            body(*prompt_refs)

        @pl.when(i >= self.n_p)
        def _():
            body(*sample_refs)


def _rows2d(x_ref):
    bb, tt, d = x_ref.shape
    return x_ref[...].reshape(bb * tt, d)


def _per_row(mod_ref, rows):
    m = mod_ref[...]
    n = m.shape[0]
    if n == 1:
        return m
    return _cat([_rows_bcast(m, s, rows // n) for s in range(n)], 0)


def _modulated_norm(x, g, sc, sh):
    return (_rms(x) * g * (1.0 + sc) + sh).astype(BF16)


def _ada_kernel(c_ref, w_ref, b_ref, o_ref):
    a = _silu(c_ref[...]).astype(BF16)
    o_ref[...] = _dot(a, w_ref[...].astype(BF16)) + b_ref[...]


def _ada_call(c_all, w_ada, b_ada):
    depth, d, n6 = w_ada.shape
    bp = c_all.shape[0]
    tn = _largest_tile(n6, 2048)
    return pl.pallas_call(
        _ada_kernel,
        grid=(depth, n6 // tn),
        in_specs=[
            pl.BlockSpec((bp, d), lambda l, j: (0, 0)),
            pl.BlockSpec((None, d, tn), lambda l, j: (l, 0, j)),
            pl.BlockSpec((None, 1, tn), lambda l, j: (l, 0, j)),
        ],
        out_specs=pl.BlockSpec((None, bp, tn), lambda l, j: (l, 0, j)),
        out_shape=jax.ShapeDtypeStruct((depth, bp, n6), F32),
        compiler_params=_cparams("arbitrary", "arbitrary"),
        name="ada_mod",
    )(c_all, w_ada, b_ada.reshape(depth, 1, n6))


def _log_decay_terms(h, wl_ref, wa2_ref, ba_ref):
    alr = _dot_nt(h, wl_ref[...])
    gk = _dot(alr.astype(BF16), wa2_ref[...]) + ba_ref[...]
    return _split(_log_sigmoid(gk) * (1.0 / GATE_NORMALIZER), LOG_DECAY_TERMS)


def _gate_specs(rg, layer, d, dqk):
    const = lambda i: (0, 0)
    ins = [_layer_spec((LANE, d), layer, const), _layer_spec((LANE, dqk), layer, const),
           _layer_spec((1, dqk), layer, const)]
    outs = [rg.joint_spec(dqk)] * LOG_DECAY_TERMS
    return ins, outs


def _pre_kernel(xp_ref, xs_ref, scp_ref, scs_ref, shp_ref, shs_ref, g_ref, wl_ref, wa2_ref, ba_ref,
                h_ref, *lg_refs, rg):
    def body(x_ref, sc_ref, sh_ref):
        h = _modulated_norm(_rows2d(x_ref), g_ref[...], _per_row(sc_ref, rg.rows), _per_row(sh_ref, rg.rows))
        h_ref[...] = h
        for ref, term in zip(lg_refs, _log_decay_terms(h, wl_ref, wa2_ref, ba_ref)):
            ref[...] = term

    rg.run(body, (xp_ref, scp_ref, shp_ref), (xs_ref, scs_ref, shs_ref))


def _pre_call(xp, xs, mod_p, mod_s, layer, norm_g, w_alr_t, w_a2, b_a, rows=512):
    rg = _RowGrid(xp.shape, xs.shape, rows)
    d = rg.d
    dqk = w_a2.shape[2]
    m_total = xp.shape[0] * xp.shape[1] + xs.shape[0] * xs.shape[1]
    gate_in, gate_out = _gate_specs(rg, layer, d, dqk)
    return pl.pallas_call(
        functools.partial(_pre_kernel, rg=rg),
        grid=rg.grid,
        in_specs=[*rg.x_specs(), *rg.mod_specs(layer, 1), *rg.mod_specs(layer, 0),
                  _layer_spec((1, d), layer, lambda i: (0, 0)), *gate_in],
        out_specs=[rg.joint_spec(d), *gate_out],
        out_shape=[jax.ShapeDtypeStruct((m_total, d), BF16)]
        + [jax.ShapeDtypeStruct((m_total, dqk), BF16)] * LOG_DECAY_TERMS,
        compiler_params=_cparams("arbitrary"),
        name="prologue",
    )(xp, xs, mod_p, mod_s, mod_p, mod_s, norm_g, w_alr_t, w_a2, b_a)


def _mm_in_kernel(a_ref, w_ref, p_ref, wb_ref, *, acts, scale):
    j = pl.program_id(0)

    @pl.when(pl.program_id(1) == 0)
    def _():
        wb_ref[...] = w_ref[...].astype(BF16)

    fns = {"id": lambda y: y, "scale": lambda y: y * scale, "silu": _silu, "sigmoid": _sigmoid}
    for name in sorted(set(acts)):
        cond = functools.reduce(jnp.logical_or, [j == jj for jj, a in enumerate(acts) if a == name])

        @pl.when(cond)
        def _(fn=fns[name]):
            p_ref[...] = fn(_dot_nt(a_ref[...], wb_ref[...])).astype(BF16)


def _mm_in_call(h, w_in_t, layer, dqk, d_model, d_pool):
    m, k = h.shape
    tm = _largest_row_tile(m, ROW_TILE_CAP)
    starts = _p_columns(dqk, d_model, d_pool)
    n_main = starts[-1] + d_model
    tn = LANE
    for cand in range(LANE, 1024 + 1, LANE):
        if all(s % cand == 0 for s in starts + (n_main,)):
            tn = cand
    kinds = ("scale", "id", "id", "silu", "id", "sigmoid", "sigmoid")
    bounds = starts[1:] + (n_main,)
    acts = []
    for jj in range(n_main // tn):
        acts.append(kinds[next(i for i, e in enumerate(bounds) if jj * tn < e)])
    dk = dqk // N_HEADS
    return pl.pallas_call(
        functools.partial(_mm_in_kernel, acts=tuple(acts), scale=dk ** -0.5),
        grid=(n_main // tn, m // tm),
        in_specs=[
            pl.BlockSpec((tm, k), lambda j, i: (i, 0)),
            _layer_spec((tn, k), layer, lambda j, i: (j, 0)),
        ],
        out_specs=pl.BlockSpec((tm, tn), lambda j, i: (i, j)),
        out_shape=jax.ShapeDtypeStruct((m, n_main), BF16),
        scratch_shapes=[pltpu.VMEM((tn, k), BF16)],
        compiler_params=_cparams("arbitrary", "arbitrary"),
        name="in_proj",
    )(h, w_in_t)


def _gla_consts(t, seq):
    nb = min(STABLE_BLOCK, seq)
    nblk = t // nb
    levels = []
    n = nb * 2
    while n <= seq:
        levels.append(n)
        n *= 2
    nkinds = 3 + len(levels)
    rows = -(-nkinds * nblk // 16) * 16
    row = _iota2((t, t), 0)
    col = _iota2((t, t), 1)
    rowv = _iota2((t, 1), 0)
    same0 = _blk(row, nb) == _blk(col, nb)
    r = _iota2((rows, t), 0)
    u = _iota2((rows, t), 1)
    kind = _blk(r, nblk)
    j = _rem(r, nblk)
    start = j * nb
    seq_start = start & jnp.int32(-seq)
    sel = (kind == 0) & (u >= seq_start) & (u < start)
    sel |= (kind == 1) & (u >= start) & (u < seq_start + seq)
    for li, n in enumerate(levels):
        mid = (start & jnp.int32(-n)) + n // 2
        between = ((u >= mid) & (u < start)) | ((u >= start) & (u < mid))
        sel |= (kind == 2 + li) & between
    sel |= (kind == nkinds - 1) & (_blk(u, seq) == j)
    return {
        "t": t, "seq": seq, "nb": nb, "nblk": nblk, "total_kind": nkinds - 1,
        "tri_local": (same0 & (col <= row)).astype(BF16),
        "sel": jnp.where(sel, 1.0, 0.0).astype(BF16),
        "mask0": same0 & (col <= row),
        "levels": [(_rem(rowv, n) >= n // 2, _blk(row, n) == _blk(col, n)) for n in levels],
    }


def _gla_operands(q, k, e0, g, c):
    nb, nblk = c["nb"], c["nblk"]

    def per_block(kind):
        return _cat([_rows_bcast(g, kind * nblk + j, nb) for j in range(nblk)], 0)

    qe0 = q * jnp.exp(e0)
    ke0 = k * jnp.exp(-e0)
    pairs = [(qe0.astype(BF16), ke0.astype(BF16))]
    for li, (second, _) in enumerate(c["levels"]):
        f = per_block(2 + li)
        pairs.append((jnp.where(second, qe0 * f, 0.0).astype(BF16), jnp.where(second, 0.0, ke0 * f).astype(BF16)))
    qe = qe0 if nb == c["seq"] else qe0 * per_block(0)
    kd = ke0 * per_block(1)
    return pairs, qe.astype(BF16), kd.astype(BF16)


def _gla_scores(pairs, c):
    att = jnp.where(c["mask0"], _dot_nt(*pairs[0]), 0.0)
    for (ql, kl), (_, same) in zip(pairs[1:], c["levels"]):
        att = att + jnp.where(same, _dot_nt(ql, kl), 0.0)
    return att


def _lane_tile(x, width):
    return _cat([x] * (width // x.shape[1]), 1)


def _pool_delta(u, u_parts, carry_parts, w, pos0, seq):
    t = u.shape[0]
    npiece = t // seq
    row = _iota2((t, t), 0)
    col = _iota2((t, t), 1)
    band = ((_blk(row, seq) == _blk(col, seq)) & (col <= row) & (col > row - w)).astype(BF16)
    rc = _iota2((t, npiece * CARRY_ROWS), 0)
    cc = _iota2((t, npiece * CARRY_ROWS), 1)
    cband = ((_blk(rc, seq) == _blk(cc, CARRY_ROWS))
             & (_rem(cc, CARRY_ROWS) > _rem(rc, seq) + CARRY_ROWS - w)).astype(BF16)
    s = _dot_parts(band, u_parts) + _dot_parts(cband, carry_parts)
    cnt = jnp.minimum(pos0 + 1, w).astype(F32)
    return s / cnt - u


def _mix_merge(o, silu_og, sig_ga, sig_gb, pooled, gn, ps):
    y_a = _rms(o) * gn * silu_og.astype(F32)
    return sig_ga.astype(F32) * y_a + sig_gb.astype(F32) * (pooled * ps)


def _p_columns(dqk, d_model, d_pool):
    q0, k0, v0 = 0, dqk, 2 * dqk
    og0 = v0 + d_model
    u0 = og0 + d_model
    ga0 = u0 + d_pool
    gb0 = ga0 + d_model
    return q0, k0, v0, og0, u0, ga0, gb0


def _mix_prompt_kernel(p_ref, *refs, dqk, d_model):
    lg_refs = refs[:LOG_DECAY_TERMS]
    gn_ref, wp_ref, ps_ref, m_ref, s_ref, pool_out_ref, car_ref = refs[LOG_DECAY_TERMS:]
    c = pl.program_id(1)
    t = p_ref.shape[0]
    groups, pin, dv = wp_ref.shape
    dk = dqk // N_HEADS
    q0, k0, v0, og0, u0, ga0, gb0 = _p_columns(dqk, d_model, groups * pin)

    @pl.when(c == 0)
    def _():
        s_ref[...] = jnp.zeros_like(s_ref)
        car_ref[...] = jnp.zeros_like(car_ref)

    consts = _gla_consts(t, t)
    loga = tuple(ref[...] for ref in lg_refs)
    e0_all = _dot_parts(consts["tri_local"], loga)
    g_all = jnp.exp(_dot_parts(consts["sel"], loga))
    tot = consts["total_kind"] * consts["nblk"]
    decay_all = _column_tile(g_all[tot:tot + 1, :])
    pos = c * t + _iota2((t, 1), 0)
    gn = gn_ref[...]

    heads = range(N_HEADS)
    deltas = []
    for h in heads:
        ub = p_ref[:, u0 + h * pin:u0 + (h + 1) * pin]
        deltas.append(_pool_delta(ub.astype(F32), (ub,), (car_ref[:, h * pin:(h + 1) * pin],), POOL_WINDOWS[h],
                                  pos, t).astype(BF16))
    ops = []
    for h in heads:
        q = p_ref[:, q0 + h * dk:q0 + (h + 1) * dk].astype(F32)
        k = p_ref[:, k0 + h * dk:k0 + (h + 1) * dk].astype(F32)
        ops.append(_gla_operands(q, k, e0_all[:, h * dk:(h + 1) * dk], g_all[:, h * dk:(h + 1) * dk], consts))
    pooled = [_dot(deltas[h], wp_ref[h]) for h in heads]
    atts = [_gla_scores(ops[h][0], consts).astype(BF16) for h in heads]
    outs = []
    for h in heads:
        _, qe, kd = ops[h]
        vb = p_ref[:, v0 + h * dv:v0 + (h + 1) * dv]
        s = s_ref[h]
        outs.append(_dot(atts[h], vb) + _dot(qe, s.astype(BF16)))
        decay = decay_all[h * dk:(h + 1) * dk]
        s_ref[h] = _lane_tile(decay, dv) * s + _dot_tn(kd, vb)

    for h in heads:
        merged = _mix_merge(outs[h], p_ref[:, og0 + h * dv:og0 + (h + 1) * dv],
                            p_ref[:, ga0 + h * dv:ga0 + (h + 1) * dv], p_ref[:, gb0 + h * dv:gb0 + (h + 1) * dv],
                            pooled[h], gn, ps_ref[:, h * dv:(h + 1) * dv])
        m_ref[:, h * dv:(h + 1) * dv] = merged.astype(BF16)

    u_tail = p_ref[t - CARRY_ROWS:, u0:u0 + groups * pin]
    car_ref[...] = u_tail

    @pl.when(c == pl.num_programs(1) - 1)
    def _():
        pool_out_ref[...] = u_tail.astype(F32)[CARRY_ROWS - POOL_BUF:]


def _mix_prompt_call(p, lg, batch, seq_len, layer, gla_norm_g, w_pool, pool_scale, chunk=128):
    dqk = lg[0].shape[1]
    dk = dqk // N_HEADS
    groups, pin, dv = w_pool.shape[1:]
    d_model = pool_scale.shape[2]
    n_p = p.shape[1]
    nc = seq_len // chunk
    const = lambda b, c: (0, 0)
    in_specs = [
        pl.BlockSpec((chunk, n_p), lambda b, c: (b * nc + c, 0)),
        *[pl.BlockSpec((chunk, dqk), lambda b, c: (b * nc + c, 0)) for _ in lg],
        _layer_spec((1, dv), layer, const),
        _layer_spec((groups, pin, dv), layer, lambda b, c: (0, 0, 0)),
        _layer_spec((1, d_model), layer, const),
    ]
    out_specs = [
        pl.BlockSpec((chunk, d_model), lambda b, c: (b * nc + c, 0)),
        pl.BlockSpec((None, N_HEADS, dk, dv), lambda b, c: (b, 0, 0, 0)),
        pl.BlockSpec((None, POOL_BUF, groups * pin), lambda b, c: (b, 0, 0)),
    ]
    out_shape = [
        jax.ShapeDtypeStruct((batch * seq_len, d_model), BF16),
        jax.ShapeDtypeStruct((batch, N_HEADS, dk, dv), F32),
        jax.ShapeDtypeStruct((batch, POOL_BUF, groups * pin), F32),
    ]
    return pl.pallas_call(
        functools.partial(_mix_prompt_kernel, dqk=dqk, d_model=d_model),
        grid=(batch, nc),
        in_specs=in_specs,
        out_specs=out_specs,
        out_shape=out_shape,
        scratch_shapes=[pltpu.VMEM((CARRY_ROWS, groups * pin), BF16)],
        compiler_params=_cparams("arbitrary", "arbitrary"),
        name="mix_prompt",
    )(p, *lg, gla_norm_g, w_pool, pool_scale)


def _mix_sample_kernel(p_ref, *refs, seq, dqk, d_model, n_alias):
    lg_refs = refs[:LOG_DECAY_TERMS]
    s0_ref, buf_ref, gn_ref, wp_ref, ps_ref = refs[LOG_DECAY_TERMS:LOG_DECAY_TERMS + 5]
    m_ref, s_out_ref, pool_out_ref, car_ref = refs[LOG_DECAY_TERMS + 5 + n_alias:]
    nseq = s0_ref.shape[0]
    t = nseq * seq
    groups, pin, dv = wp_ref.shape
    dk = dqk // N_HEADS
    q0, k0, v0, og0, u0, ga0, gb0 = _p_columns(dqk, d_model, groups * pin)

    car_ref[...] = jnp.zeros_like(car_ref)
    for j in range(nseq):
        car_ref[j * CARRY_ROWS + 1:(j + 1) * CARRY_ROWS, :] = buf_ref[j]

    consts = _gla_consts(t, seq)
    loga = tuple(ref[...] for ref in lg_refs)
    e0_all = _dot_parts(consts["tri_local"], loga)
    g_all = jnp.exp(_dot_parts(consts["sel"], loga))
    tot = consts["total_kind"] * consts["nblk"]
    decay_all = [_column_tile(g_all[tot + j:tot + j + 1, :]) for j in range(nseq)]
    pos = PAST_LEN + _rem(_iota2((t, 1), 0), seq)
    keep = POOL_BUF - seq
    gn = gn_ref[...]

    u_all = p_ref[:, u0:u0 + groups * pin].astype(F32)
    heads = range(N_HEADS)
    ops = []
    for h in heads:
        q = p_ref[:, q0 + h * dk:q0 + (h + 1) * dk].astype(F32)
        k = p_ref[:, k0 + h * dk:k0 + (h + 1) * dk].astype(F32)
        ops.append(_gla_operands(q, k, e0_all[:, h * dk:(h + 1) * dk], g_all[:, h * dk:(h + 1) * dk], consts))
    atts = [_gla_scores(ops[h][0], consts).astype(BF16) for h in heads]
    outs = []
    for h in heads:
        _, qeb, kdb = ops[h]
        vb = p_ref[:, v0 + h * dv:v0 + (h + 1) * dv]
        o_intra = _dot(atts[h], vb)
        o_parts = []
        for j in range(nseq):
            s0 = s0_ref[j, h]
            lo, hi = j * seq, (j + 1) * seq
            o_parts.append(o_intra[lo:hi] + _dot(qeb[lo:hi], s0.astype(BF16)))
            decay = decay_all[j][h * dk:(h + 1) * dk]
            s_out_ref[j, h] = _lane_tile(decay, dv) * s0 + _dot_tn(kdb[lo:hi], vb[lo:hi])
        outs.append(_cat(o_parts, 0))

    for h in heads:
        d = _pool_delta(u_all[:, h * pin:(h + 1) * pin], (p_ref[:, u0 + h * pin:u0 + (h + 1) * pin],),
                        _split(car_ref[:, h * pin:(h + 1) * pin], CARRY_TERMS), POOL_WINDOWS[h], pos, seq)
        merged = _mix_merge(outs[h], p_ref[:, og0 + h * dv:og0 + (h + 1) * dv],
                            p_ref[:, ga0 + h * dv:ga0 + (h + 1) * dv], p_ref[:, gb0 + h * dv:gb0 + (h + 1) * dv],
                            _dot(d.astype(BF16), wp_ref[h]), gn, ps_ref[:, h * dv:(h + 1) * dv])
        m_ref[:, h * dv:(h + 1) * dv] = merged.astype(BF16)

    for j in range(nseq):
        pool_out_ref[j, 0:keep, :] = buf_ref[j, seq:POOL_BUF, :]
        pool_out_ref[j, keep:POOL_BUF, :] = u_all[j * seq:(j + 1) * seq]


def _mix_sample_call(p, lg, row_off, seq, layer, state_gla, state_pool, gla_norm_g, w_pool,
                     pool_scale, prev_gla, prev_pool, nseq=4):
    depth, bs, nh, dk, dv = state_gla.shape
    dqk = lg[0].shape[1]
    groups, pin, _ = w_pool.shape[1:]
    d_model = pool_scale.shape[2]
    n_p = p.shape[1]
    assert seq <= POOL_BUF and seq % 8 == 0
    t = nseq * seq
    off = row_off // t
    const = lambda i: (0, 0)
    in_specs = [
        pl.BlockSpec((t, n_p), lambda i: (off + i, 0)),
        *[pl.BlockSpec((t, dqk), lambda i: (off + i, 0)) for _ in lg],
        pl.BlockSpec((None, nseq, nh, dk, dv), lambda i: (layer, i, 0, 0, 0)),
        pl.BlockSpec((None, nseq, POOL_BUF, groups * pin), lambda i: (layer, i, 0, 0)),
        _layer_spec((1, dv), layer, const),
        _layer_spec((groups, pin, dv), layer, lambda i: (0, 0, 0)),
        _layer_spec((1, d_model), layer, const),
    ]
    args = [p, *lg, state_gla, state_pool, gla_norm_g, w_pool, pool_scale]
    aliases = {}
    n_alias = 0
    if prev_gla is not None:
        in_specs += [pl.BlockSpec(memory_space=pl.ANY), pl.BlockSpec(memory_space=pl.ANY)]
        aliases = {len(args): 1, len(args) + 1: 2}
        args += [prev_gla, prev_pool]
        n_alias = 2
    out_specs = [
        pl.BlockSpec((t, d_model), lambda i: (i, 0)),
        pl.BlockSpec((None, nseq, nh, dk, dv), lambda i: (layer, i, 0, 0, 0)),
        pl.BlockSpec((None, nseq, POOL_BUF, groups * pin), lambda i: (layer, i, 0, 0)),
    ]
    out_shape = [
        jax.ShapeDtypeStruct((bs * seq, d_model), BF16),
        jax.ShapeDtypeStruct(state_gla.shape, F32),
        jax.ShapeDtypeStruct(state_pool.shape, F32),
    ]
    return pl.pallas_call(
        functools.partial(_mix_sample_kernel, seq=seq, dqk=dqk, d_model=d_model, n_alias=n_alias),
        grid=(bs // nseq,),
        in_specs=in_specs,
        out_specs=out_specs,
        out_shape=out_shape,
        scratch_shapes=[pltpu.VMEM((nseq * CARRY_ROWS, groups * pin), F32)],
        input_output_aliases=aliases,
        compiler_params=_cparams("arbitrary"),
        name="mix_sample",
    )(*args)


def _mm_o_kernel(ap_ref, as_ref, xp_ref, xs_ref, g1p_ref, g1s_ref, scp_ref, scs_ref, shp_ref, shs_ref,
                 w_ref, ng_ref, x1p_ref, x1s_ref, h_ref, wb_ref, *, rg):
    @pl.when(pl.program_id(0) == 0)
    def _():
        wb_ref[...] = w_ref[...].astype(BF16)

    def body(a_ref, x_ref, g1_ref, sc_ref, sh_ref, x1_ref):
        rows = rg.rows
        x1 = _rows2d(x_ref) + _per_row(g1_ref, rows) * _dot(a_ref[...], wb_ref[...])
        x1_ref[...] = x1.reshape(x1_ref.shape)
        h_ref[...] = _modulated_norm(x1, ng_ref[...], _per_row(sc_ref, rows), _per_row(sh_ref, rows))

    rg.run(body, (ap_ref, xp_ref, g1p_ref, scp_ref, shp_ref, x1p_ref),
           (as_ref, xs_ref, g1s_ref, scs_ref, shs_ref, x1s_ref))


def _mm_o_call(merged_p, merged_s, xp, xs, w_o, mod_p, mod_s, layer, norm2_g, rows=256):
    rg = _RowGrid(xp.shape, xs.shape, rows)
    d = rg.d
    m_total = xp.shape[0] * xp.shape[1] + xs.shape[0] * xs.shape[1]
    xsp = rg.x_specs()
    return pl.pallas_call(
        functools.partial(_mm_o_kernel, rg=rg),
        grid=rg.grid,
        in_specs=[*rg.rows_specs(d), *xsp, *rg.mod_specs(layer, 2), *rg.mod_specs(layer, 4),
                  *rg.mod_specs(layer, 3), _resident_spec((d, d), layer),
                  _layer_spec((1, d), layer, lambda i: (0, 0))],
        out_specs=[*xsp, rg.joint_spec(d)],
        out_shape=[jax.ShapeDtypeStruct(xp.shape, F32), jax.ShapeDtypeStruct(xs.shape, F32),
                   jax.ShapeDtypeStruct((m_total, d), BF16)],
        scratch_shapes=[pltpu.VMEM((d, d), BF16)],
        compiler_params=_cparams("arbitrary"),
        name="out_proj",
    )(merged_p, merged_s, xp, xs, mod_p, mod_s, mod_p, mod_s, mod_p, mod_s, w_o, norm2_g)


def _mm_gu_kernel(a_ref, wg_ref, wu_ref, wd_ref, o_ref, wdb_ref, wgb_ref, wub_ref):
    @pl.when(pl.program_id(1) == 0)
    def _():
        wgb_ref[...] = wg_ref[...].astype(BF16)
        wub_ref[...] = wu_ref[...].astype(BF16)
        wdb_ref[...] = wd_ref[...].astype(BF16)

    a = a_ref[...]
    o_ref[...] = (_silu(_dot(a, wgb_ref[...])) * _dot(a, wub_ref[...])).astype(BF16)


def _mm_gu_call(h, w_gu, w_down, layer):
    m, k = h.shape
    tm = _largest_row_tile(m, ROW_TILE_CAP)
    dff = w_gu.shape[2] // 2
    d = w_down.shape[2]
    tn = _largest_tile(dff, 512)
    nj = dff // tn
    return pl.pallas_call(
        _mm_gu_kernel,
        grid=(nj, m // tm),
        in_specs=[
            pl.BlockSpec((tm, k), lambda j, i: (i, 0)),
            _layer_spec((k, tn), layer, lambda j, i: (0, j)),
            _layer_spec((k, tn), layer, lambda j, i: (0, nj + j)),
            _layer_spec((tn, d), layer, lambda j, i: (j, 0)),
        ],
        out_specs=[pl.BlockSpec((tm, tn), lambda j, i: (i, j)), pl.BlockSpec((tn, d), lambda j, i: (j, 0))],
        out_shape=[jax.ShapeDtypeStruct((m, dff), BF16), jax.ShapeDtypeStruct((dff, d), BF16)],
        scratch_shapes=[pltpu.VMEM((k, tn), BF16), pltpu.VMEM((k, tn), BF16)],
        compiler_params=_cparams("arbitrary", "arbitrary"),
        name="gate_up",
    )(h, w_gu, w_gu, w_down)


def _mm_down_kernel(a_ref, xp_ref, xs_ref, g2p_ref, g2s_ref, w_ref, ng_ref, *rest, rg, final):
    if final:
        yp_ref, ys_ref = rest

        def body(x_ref, g2_ref, y_ref):
            x2 = _rows2d(x_ref) + _per_row(g2_ref, rg.rows) * _dot(a_ref[...], w_ref[...])
            y_ref[...] = (_rms(x2) * ng_ref[...]).reshape(y_ref.shape)

        rg.run(body, (xp_ref, g2p_ref, yp_ref), (xs_ref, g2s_ref, ys_ref))
    else:
        scp_ref, scs_ref, shp_ref, shs_ref, wl_ref, wa2_ref, ba_ref, x2p_ref, x2s_ref, h_ref, *lg_refs = rest

        def body(x_ref, g2_ref, sc_ref, sh_ref, x2_ref):
            rows = rg.rows
            x2 = _rows2d(x_ref) + _per_row(g2_ref, rows) * _dot(a_ref[...], w_ref[...])
            x2_ref[...] = x2.reshape(x2_ref.shape)
            h = _modulated_norm(x2, ng_ref[...], _per_row(sc_ref, rows), _per_row(sh_ref, rows))
            h_ref[...] = h
            for ref, term in zip(lg_refs, _log_decay_terms(h, wl_ref, wa2_ref, ba_ref)):
                ref[...] = term

        rg.run(body, (xp_ref, g2p_ref, scp_ref, shp_ref, x2p_ref), (xs_ref, g2s_ref, scs_ref, shs_ref, x2s_ref))


def _mm_down_call(act, xp, xs, w_down, mod_p, mod_s, layer, norm_g, norm_layer, final, w_alr_t, w_a2, b_a,
                  rows=256):
    rg = _RowGrid(xp.shape, xs.shape, rows)
    d = rg.d
    dff = w_down.shape[0]
    m_total = act.shape[0]
    xsp = rg.x_specs()
    in_specs = [rg.joint_spec(dff), *xsp, *rg.mod_specs(layer, 5), _resident_spec((dff, d)),
                _layer_spec((1, d), norm_layer, lambda i: (0, 0))]
    args = [act, xp, xs, mod_p, mod_s, w_down, norm_g]
    x_shapes = [jax.ShapeDtypeStruct(xp.shape, F32), jax.ShapeDtypeStruct(xs.shape, F32)]
    if final:
        out_specs, out_shape = [*xsp], x_shapes
    else:
        dqk = w_a2.shape[2]
        gate_in, gate_out = _gate_specs(rg, layer + 1, d, dqk)
        in_specs += [*rg.mod_specs(layer + 1, 1), *rg.mod_specs(layer + 1, 0), *gate_in]
        args += [mod_p, mod_s, mod_p, mod_s, w_alr_t, w_a2, b_a]
        out_specs = [*xsp, rg.joint_spec(d), *gate_out]
        out_shape = x_shapes + [jax.ShapeDtypeStruct((m_total, d), BF16)] \
            + [jax.ShapeDtypeStruct((m_total, dqk), BF16)] * LOG_DECAY_TERMS
    return pl.pallas_call(
        functools.partial(_mm_down_kernel, rg=rg, final=final),
        grid=rg.grid,
        in_specs=in_specs,
        out_specs=out_specs,
        out_shape=out_shape,
        compiler_params=_cparams("arbitrary"),
        name="down_proj",
    )(*args)


def kernel(x_prompt, x_sample, state_gla, state_pool, c_prompt, c_sample, w_ada, b_ada, norm1_g, w_in, w_a2, b_a,
           gla_norm_g, w_pool, pool_scale, w_o, norm2_g, w_gu, w_down, final_norm_g):
    depth, d, n6 = w_ada.shape
    bp, lp, _ = x_prompt.shape
    bs, ls, _ = x_sample.shape
    mp = bp * lp
    rank, dqk = w_a2.shape[1:]
    n_main = w_in.shape[2] - rank

    c_all = jnp.concatenate([c_sample, c_prompt], axis=0)
    c_all = jnp.pad(c_all, ((0, (-c_all.shape[0]) % 8), (0, 0)))
    mod_s = _ada_call(c_all, w_ada, b_ada)
    mod_p = mod_s[:, bs:bs + bp].reshape(depth, bp, 1, n6)

    w_in_t = jnp.swapaxes(w_in, 1, 2)
    w_alr_b = jnp.pad(w_in_t[:, n_main:, :], ((0, 0), (0, LANE - rank), (0, 0))).astype(BF16)
    w_a2_b = jnp.pad(w_a2, ((0, 0), (0, LANE - rank), (0, 0))).astype(BF16)
    w_pool_b = w_pool.astype(BF16)
    n1 = norm1_g.reshape(depth, 1, d)
    n2 = norm2_g.reshape(depth, 1, d)
    nf = final_norm_g.reshape(1, 1, d)
    ba3 = b_a.reshape(depth, 1, dqk)
    gn3 = gla_norm_g.reshape(depth, 1, -1)
    ps3 = pool_scale.reshape(depth, 1, d)

    xp, xs = x_prompt, x_sample
    h, *lg = _pre_call(xp, xs, mod_p, mod_s, 0, n1, w_alr_b, w_a2_b, ba3)

    gla_p, pool_p = [], []
    gla_s = pool_s = None
    for l in range(depth):
        p = _mm_in_call(h, w_in_t, l, dqk, d, w_pool.shape[1] * w_pool.shape[2])
        merged_p, s_p, b_p = _mix_prompt_call(p, lg, bp, lp, l, gn3, w_pool_b, ps3)
        merged_s, gla_s, pool_s = _mix_sample_call(p, lg, mp, ls, l, state_gla, state_pool, gn3, w_pool_b, ps3,
                                                   gla_s, pool_s)
        gla_p.append(s_p)
        pool_p.append(b_p)

        xp, xs, h2 = _mm_o_call(merged_p, merged_s, xp, xs, w_o, mod_p, mod_s, l, n2)
        act, w_down_b = _mm_gu_call(h2, w_gu, w_down, l)
        if l + 1 < depth:
            xp, xs, h, *lg = _mm_down_call(act, xp, xs, w_down_b, mod_p, mod_s, l, n1, l + 1, False, w_alr_b,
                                           w_a2_b, ba3)
        else:
            yp, ys = _mm_down_call(act, xp, xs, w_down_b, mod_p, mod_s, l, nf, 0, True, None, None, None)
    return yp, ys, jnp.stack(gla_p), jnp.stack(pool_p), gla_s, pool_s
```

```python
import functools

import jax
import jax.numpy as jnp
from jax import lax
from jax.experimental import pallas as pl
from jax.experimental.pallas import tpu as pltpu

F32 = jnp.float32
BF16 = jnp.bfloat16

N_HEADS = 4
POOL_WINDOWS = (2, 4, 8, 16)
POOL_BUF = 15
CARRY_ROWS = POOL_BUF + 1
GATE_NORMALIZER = 16.0
PAST_LEN = 16384
EPS = 1e-6
STABLE_BLOCK = 16
LOG_DECAY_TERMS = 2
CARRY_TERMS = 3
LANE = 128
MXU_DIM = 256
ROW_TILE_CAP = 1024
VMEM_LIMIT = 56 * 1024 * 1024


def _cparams(*sem):
    return pltpu.CompilerParams(dimension_semantics=sem, vmem_limit_bytes=VMEM_LIMIT)


def _largest_tile(n, cap):
    best = None
    for t in range(LANE, cap + 1, LANE):
        if n % t == 0:
            best = t
    assert best is not None, (n, cap)
    return best


def _largest_row_tile(m, cap):
    best = None
    for t in range(MXU_DIM, cap + 1, MXU_DIM):
        if m % t == 0:
            best = t
    assert best is not None, (m, cap)
    return best


def _dot(a, b):
    return jnp.dot(a, b, preferred_element_type=F32)


def _dot_nt(a, b):
    return lax.dot_general(a, b, (((1,), (1,)), ((), ())), preferred_element_type=F32)


def _dot_tn(a, b):
    return lax.dot_general(a, b, (((0,), (0,)), ((), ())), preferred_element_type=F32)


def _split(x, terms):
    parts = []
    for i in range(terms):
        p = x.astype(BF16)
        parts.append(p)
        if i + 1 < terms:
            x = x - p.astype(F32)
    return tuple(parts)


def _dot_parts(m, parts):
    out = _dot(m, parts[0])
    for p in parts[1:]:
        out = out + _dot(m, p)
    return out


def _column_tile(row):
    return jnp.broadcast_to(row, (LANE, row.shape[1])).T


def _sigmoid(x):
    return 1.0 / (1.0 + jnp.exp(-x))


def _silu(x):
    return x * _sigmoid(x)


def _log_sigmoid(x):
    return jnp.minimum(x, 0.0) - jnp.log(1.0 + jnp.exp(-jnp.abs(x)))


def _rms(x):
    return x * lax.rsqrt(jnp.mean(x * x, axis=-1, keepdims=True) + EPS)


def _rows_bcast(x, row, n):
    return jnp.broadcast_to(x[row:row + 1, :], (n, x.shape[1]))


def _cat(parts, axis):
    return parts[0] if len(parts) == 1 else jnp.concatenate(parts, axis=axis)


def _iota2(shape, dim):
    return lax.broadcasted_iota(jnp.int32, shape, dim)


def _blk(x, n):
    assert n & (n - 1) == 0
    return lax.shift_right_logical(x, jnp.int32(n.bit_length() - 1))


def _rem(x, n):
    assert n & (n - 1) == 0
    return x & jnp.int32(n - 1)


def _layer_spec(block, layer, tail):
    return pl.BlockSpec((None,) + block, lambda *g: (layer,) + tail(*g))


def _resident_spec(block, layer=None):
    zeros = (0,) * len(block)
    if layer is None:
        return pl.BlockSpec(block, lambda *g: zeros, pipeline_mode=pl.Buffered(1))
    return pl.BlockSpec((None,) + block, lambda *g: (layer,) + zeros, pipeline_mode=pl.Buffered(1))


class _RowGrid:
    def __init__(self, xp_shape, xs_shape, rows):
        bp, lp, d = xp_shape
        bs, ls, _ = xs_shape
        assert lp % rows == 0 and rows % ls == 0 and bs % (rows // ls) == 0
        self.rows, self.d = rows, d
        self.tiles_per_seq = lp // rows
        self.n_p = bp * self.tiles_per_seq
        self.seqs_per_tile = rows // ls
        self.n_s = bs // self.seqs_per_tile
        self.ls = ls
        self.grid = (self.n_p + self.n_s,)

    def _ip(self, i):
        return jnp.minimum(i, self.n_p - 1)

    def _is(self, i):
        return jnp.maximum(i - self.n_p, 0)

    def x_specs(self):
        tps = self.tiles_per_seq
        return (pl.BlockSpec((1, self.rows, self.d), lambda i: (self._ip(i) // tps, self._ip(i) % tps, 0)),
                pl.BlockSpec((self.seqs_per_tile, self.ls, self.d), lambda i: (self._is(i), 0, 0)))

    def mod_specs(self, layer, j):
        tps = self.tiles_per_seq
        return (pl.BlockSpec((None, None, 1, self.d), lambda i: (layer, self._ip(i) // tps, 0, j)),
                pl.BlockSpec((None, self.seqs_per_tile, self.d), lambda i: (layer, self._is(i), j)))

    def rows_specs(self, width):
        return (pl.BlockSpec((self.rows, width), lambda i: (self._ip(i), 0)),
                pl.BlockSpec((self.rows, width), lambda i: (self._is(i), 0)))

    def joint_spec(self, width):
        return pl.BlockSpec((self.rows, width), lambda i: (i, 0))

    def run(self, body, prompt_refs, sample_refs):
        i = pl.program_id(0)

        @pl.when(i < self.n_p)
        def _():
            body(*prompt_refs)

        @pl.when(i >= self.n_p)
        def _():
            body(*sample_refs)


def _rows2d(x_ref):
    bb, tt, d = x_ref.shape
    return x_ref[...].reshape(bb * tt, d)


def _per_row(mod_ref, rows):
    m = mod_ref[...]
    n = m.shape[0]
    if n == 1:
        return m
    return _cat([_rows_bcast(m, s, rows // n) for s in range(n)], 0)


def _modulated_norm(x, g, sc, sh):
    return (_rms(x) * g * (1.0 + sc) + sh).astype(BF16)


def _ada_kernel(c_ref, w_ref, b_ref, o_ref):
    a = _silu(c_ref[...]).astype(BF16)
    o_ref[...] = _dot(a, w_ref[...].astype(BF16)) + b_ref[...]


def _ada_call(c_all, w_ada, b_ada):
    depth, d, n6 = w_ada.shape
    bp = c_all.shape[0]
    tn = _largest_tile(n6, 2048)
    return pl.pallas_call(
        _ada_kernel,
        grid=(depth, n6 // tn),
        in_specs=[
            pl.BlockSpec((bp, d), lambda l, j: (0, 0)),
            pl.BlockSpec((None, d, tn), lambda l, j: (l, 0, j)),
            pl.BlockSpec((None, 1, tn), lambda l, j: (l, 0, j)),
        ],
        out_specs=pl.BlockSpec((None, bp, tn), lambda l, j: (l, 0, j)),
        out_shape=jax.ShapeDtypeStruct((depth, bp, n6), F32),
        compiler_params=_cparams("arbitrary", "arbitrary"),
        name="ada_mod",
    )(c_all, w_ada, b_ada.reshape(depth, 1, n6))


def _log_decay_terms(h, wl_ref, wa2_ref, ba_ref):
    alr = _dot_nt(h, wl_ref[...])
    gk = _dot(alr.astype(BF16), wa2_ref[...]) + ba_ref[...]
    return _split(_log_sigmoid(gk) * (1.0 / GATE_NORMALIZER), LOG_DECAY_TERMS)


def _gate_specs(rg, layer, d, dqk):
    const = lambda i: (0, 0)
    ins = [_layer_spec((LANE, d), layer, const), _layer_spec((LANE, dqk), layer, const),
           _layer_spec((1, dqk), layer, const)]
    outs = [rg.joint_spec(dqk)] * LOG_DECAY_TERMS
    return ins, outs


def _pre_kernel(xp_ref, xs_ref, scp_ref, scs_ref, shp_ref, shs_ref, g_ref, wl_ref, wa2_ref, ba_ref,
                h_ref, *lg_refs, rg):
    def body(x_ref, sc_ref, sh_ref):
        h = _modulated_norm(_rows2d(x_ref), g_ref[...], _per_row(sc_ref, rg.rows), _per_row(sh_ref, rg.rows))
        h_ref[...] = h
        for ref, term in zip(lg_refs, _log_decay_terms(h, wl_ref, wa2_ref, ba_ref)):
            ref[...] = term

    rg.run(body, (xp_ref, scp_ref, shp_ref), (xs_ref, scs_ref, shs_ref))


def _pre_call(xp, xs, mod_p, mod_s, layer, norm_g, w_alr_t, w_a2, b_a, rows=512):
    rg = _RowGrid(xp.shape, xs.shape, rows)
    d = rg.d
    dqk = w_a2.shape[2]
    m_total = xp.shape[0] * xp.shape[1] + xs.shape[0] * xs.shape[1]
    gate_in, gate_out = _gate_specs(rg, layer, d, dqk)
    return pl.pallas_call(
        functools.partial(_pre_kernel, rg=rg),
        grid=rg.grid,
        in_specs=[*rg.x_specs(), *rg.mod_specs(layer, 1), *rg.mod_specs(layer, 0),
                  _layer_spec((1, d), layer, lambda i: (0, 0)), *gate_in],
        out_specs=[rg.joint_spec(d), *gate_out],
        out_shape=[jax.ShapeDtypeStruct((m_total, d), BF16)]
        + [jax.ShapeDtypeStruct((m_total, dqk), BF16)] * LOG_DECAY_TERMS,
        compiler_params=_cparams("arbitrary"),
        name="prologue",
    )(xp, xs, mod_p, mod_s, mod_p, mod_s, norm_g, w_alr_t, w_a2, b_a)


def _mm_in_kernel(a_ref, w_ref, p_ref, wb_ref, *, acts, scale):
    j = pl.program_id(0)

    @pl.when(pl.program_id(1) == 0)
    def _():
        wb_ref[...] = w_ref[...].astype(BF16)

    fns = {"id": lambda y: y, "scale": lambda y: y * scale, "silu": _silu, "sigmoid": _sigmoid}
    for name in sorted(set(acts)):
        cond = functools.reduce(jnp.logical_or, [j == jj for jj, a in enumerate(acts) if a == name])

        @pl.when(cond)
        def _(fn=fns[name]):
            p_ref[...] = fn(_dot_nt(a_ref[...], wb_ref[...])).astype(BF16)


def _mm_in_call(h, w_in_t, layer, dqk, d_model, d_pool):
    m, k = h.shape
    tm = _largest_row_tile(m, ROW_TILE_CAP)
    starts = _p_columns(dqk, d_model, d_pool)
    n_main = starts[-1] + d_model
    tn = LANE
    for cand in range(LANE, 1024 + 1, LANE):
        if all(s % cand == 0 for s in starts + (n_main,)):
            tn = cand
    kinds = ("scale", "id", "id", "silu", "id", "sigmoid", "sigmoid")
    bounds = starts[1:] + (n_main,)
    acts = []
    for jj in range(n_main // tn):
        acts.append(kinds[next(i for i, e in enumerate(bounds) if jj * tn < e)])
    dk = dqk // N_HEADS
    return pl.pallas_call(
        functools.partial(_mm_in_kernel, acts=tuple(acts), scale=dk ** -0.5),
        grid=(n_main // tn, m // tm),
        in_specs=[
            pl.BlockSpec((tm, k), lambda j, i: (i, 0)),
            _layer_spec((tn, k), layer, lambda j, i: (j, 0)),
        ],
        out_specs=pl.BlockSpec((tm, tn), lambda j, i: (i, j)),
        out_shape=jax.ShapeDtypeStruct((m, n_main), BF16),
        scratch_shapes=[pltpu.VMEM((tn, k), BF16)],
        compiler_params=_cparams("arbitrary", "arbitrary"),
        name="in_proj",
    )(h, w_in_t)


def _gla_consts(t, seq):
    nb = min(STABLE_BLOCK, seq)
    nblk = t // nb
    levels = []
    n = nb * 2
    while n <= seq:
        levels.append(n)
        n *= 2
    nkinds = 3 + len(levels)
    rows = -(-nkinds * nblk // 16) * 16
    row = _iota2((t, t), 0)
    col = _iota2((t, t), 1)
    rowv = _iota2((t, 1), 0)
    same0 = _blk(row, nb) == _blk(col, nb)
    r = _iota2((rows, t), 0)
    u = _iota2((rows, t), 1)
    kind = _blk(r, nblk)
    j = _rem(r, nblk)
    start = j * nb
    seq_start = start & jnp.int32(-seq)
    sel = (kind == 0) & (u >= seq_start) & (u < start)
    sel |= (kind == 1) & (u >= start) & (u < seq_start + seq)
    for li, n in enumerate(levels):
        mid = (start & jnp.int32(-n)) + n // 2
        between = ((u >= mid) & (u < start)) | ((u >= start) & (u < mid))
        sel |= (kind == 2 + li) & between
    sel |= (kind == nkinds - 1) & (_blk(u, seq) == j)
    return {
        "t": t, "seq": seq, "nb": nb, "nblk": nblk, "total_kind": nkinds - 1,
        "tri_local": (same0 & (col <= row)).astype(BF16),
        "sel": jnp.where(sel, 1.0, 0.0).astype(BF16),
        "mask0": same0 & (col <= row),
        "levels": [(_rem(rowv, n) >= n // 2, _blk(row, n) == _blk(col, n)) for n in levels],
    }


def _gla_operands(q, k, e0, g, c):
    nb, nblk = c["nb"], c["nblk"]

    def per_block(kind):
        return _cat([_rows_bcast(g, kind * nblk + j, nb) for j in range(nblk)], 0)

    qe0 = q * jnp.exp(e0)
    ke0 = k * jnp.exp(-e0)
    pairs = [(qe0.astype(BF16), ke0.astype(BF16))]
    for li, (second, _) in enumerate(c["levels"]):
        f = per_block(2 + li)
        pairs.append((jnp.where(second, qe0 * f, 0.0).astype(BF16), jnp.where(second, 0.0, ke0 * f).astype(BF16)))
    qe = qe0 if nb == c["seq"] else qe0 * per_block(0)
    kd = ke0 * per_block(1)
    return pairs, qe.astype(BF16), kd.astype(BF16)


def _gla_scores(pairs, c):
    att = jnp.where(c["mask0"], _dot_nt(*pairs[0]), 0.0)
    for (ql, kl), (_, same) in zip(pairs[1:], c["levels"]):
        att = att + jnp.where(same, _dot_nt(ql, kl), 0.0)
    return att


def _lane_tile(x, width):
    return _cat([x] * (width // x.shape[1]), 1)


def _pool_delta(u, u_parts, carry_parts, w, pos0, seq):
    t = u.shape[0]
    npiece = t // seq
    row = _iota2((t, t), 0)
    col = _iota2((t, t), 1)
    band = ((_blk(row, seq) == _blk(col, seq)) & (col <= row) & (col > row - w)).astype(BF16)
    rc = _iota2((t, npiece * CARRY_ROWS), 0)
    cc = _iota2((t, npiece * CARRY_ROWS), 1)
    cband = ((_blk(rc, seq) == _blk(cc, CARRY_ROWS))
             & (_rem(cc, CARRY_ROWS) > _rem(rc, seq) + CARRY_ROWS - w)).astype(BF16)
    s = _dot_parts(band, u_parts) + _dot_parts(cband, carry_parts)
    cnt = jnp.minimum(pos0 + 1, w).astype(F32)
    return s / cnt - u


def _mix_merge(o, silu_og, sig_ga, sig_gb, pooled, gn, ps):
    y_a = _rms(o) * gn * silu_og.astype(F32)
    return sig_ga.astype(F32) * y_a + sig_gb.astype(F32) * (pooled * ps)


def _p_columns(dqk, d_model, d_pool):
    q0, k0, v0 = 0, dqk, 2 * dqk
    og0 = v0 + d_model
    u0 = og0 + d_model
    ga0 = u0 + d_pool
    gb0 = ga0 + d_model
    return q0, k0, v0, og0, u0, ga0, gb0


def _mix_prompt_kernel(p_ref, *refs, dqk, d_model):
    lg_refs = refs[:LOG_DECAY_TERMS]
    gn_ref, wp_ref, ps_ref, m_ref, s_ref, pool_out_ref, car_ref = refs[LOG_DECAY_TERMS:]
    c = pl.program_id(1)
    t = p_ref.shape[0]
    groups, pin, dv = wp_ref.shape
    dk = dqk // N_HEADS
    q0, k0, v0, og0, u0, ga0, gb0 = _p_columns(dqk, d_model, groups * pin)

    @pl.when(c == 0)
    def _():
        s_ref[...] = jnp.zeros_like(s_ref)
        car_ref[...] = jnp.zeros_like(car_ref)

    consts = _gla_consts(t, t)
    loga = tuple(ref[...] for ref in lg_refs)
    e0_all = _dot_parts(consts["tri_local"], loga)
    g_all = jnp.exp(_dot_parts(consts["sel"], loga))
    tot = consts["total_kind"] * consts["nblk"]
    decay_all = _column_tile(g_all[tot:tot + 1, :])
    pos = c * t + _iota2((t, 1), 0)
    gn = gn_ref[...]

    heads = range(N_HEADS)
    deltas = []
    for h in heads:
        ub = p_ref[:, u0 + h * pin:u0 + (h + 1) * pin]
        deltas.append(_pool_delta(ub.astype(F32), (ub,), (car_ref[:, h * pin:(h + 1) * pin],), POOL_WINDOWS[h],
                                  pos, t).astype(BF16))
    ops = []
    for h in heads:
        q = p_ref[:, q0 + h * dk:q0 + (h + 1) * dk].astype(F32)
        k = p_ref[:, k0 + h * dk:k0 + (h + 1) * dk].astype(F32)
        ops.append(_gla_operands(q, k, e0_all[:, h * dk:(h + 1) * dk], g_all[:, h * dk:(h + 1) * dk], consts))
    pooled = [_dot(deltas[h], wp_ref[h]) for h in heads]
    atts = [_gla_scores(ops[h][0], consts).astype(BF16) for h in heads]
    outs = []
    for h in heads:
        _, qe, kd = ops[h]
        vb = p_ref[:, v0 + h * dv:v0 + (h + 1) * dv]
        s = s_ref[h]
        outs.append(_dot(atts[h], vb) + _dot(qe, s.astype(BF16)))
        decay = decay_all[h * dk:(h + 1) * dk]
        s_ref[h] = _lane_tile(decay, dv) * s + _dot_tn(kd, vb)

    for h in heads:
        merged = _mix_merge(outs[h], p_ref[:, og0 + h * dv:og0 + (h + 1) * dv],
                            p_ref[:, ga0 + h * dv:ga0 + (h + 1) * dv], p_ref[:, gb0 + h * dv:gb0 + (h + 1) * dv],
                            pooled[h], gn, ps_ref[:, h * dv:(h + 1) * dv])
        m_ref[:, h * dv:(h + 1) * dv] = merged.astype(BF16)

    u_tail = p_ref[t - CARRY_ROWS:, u0:u0 + groups * pin]
    car_ref[...] = u_tail

    @pl.when(c == pl.num_programs(1) - 1)
    def _():
        pool_out_ref[...] = u_tail.astype(F32)[CARRY_ROWS - POOL_BUF:]


def _mix_prompt_call(p, lg, batch, seq_len, layer, gla_norm_g, w_pool, pool_scale, chunk=128):
    dqk = lg[0].shape[1]
    dk = dqk // N_HEADS
    groups, pin, dv = w_pool.shape[1:]
    d_model = pool_scale.shape[2]
    n_p = p.shape[1]
    nc = seq_len // chunk
    const = lambda b, c: (0, 0)
    in_specs = [
        pl.BlockSpec((chunk, n_p), lambda b, c: (b * nc + c, 0)),
        *[pl.BlockSpec((chunk, dqk), lambda b, c: (b * nc + c, 0)) for _ in lg],
        _layer_spec((1, dv), layer, const),
        _layer_spec((groups, pin, dv), layer, lambda b, c: (0, 0, 0)),
        _layer_spec((1, d_model), layer, const),
    ]
    out_specs = [
        pl.BlockSpec((chunk, d_model), lambda b, c: (b * nc + c, 0)),
        pl.BlockSpec((None, N_HEADS, dk, dv), lambda b, c: (b, 0, 0, 0)),
        pl.BlockSpec((None, POOL_BUF, groups * pin), lambda b, c: (b, 0, 0)),
    ]
    out_shape = [
        jax.ShapeDtypeStruct((batch * seq_len, d_model), BF16),
        jax.ShapeDtypeStruct((batch, N_HEADS, dk, dv), F32),
        jax.ShapeDtypeStruct((batch, POOL_BUF, groups * pin), F32),
    ]
    return pl.pallas_call(
        functools.partial(_mix_prompt_kernel, dqk=dqk, d_model=d_model),
        grid=(batch, nc),
        in_specs=in_specs,
        out_specs=out_specs,
        out_shape=out_shape,
        scratch_shapes=[pltpu.VMEM((CARRY_ROWS, groups * pin), BF16)],
        compiler_params=_cparams("arbitrary", "arbitrary"),
        name="mix_prompt",
    )(p, *lg, gla_norm_g, w_pool, pool_scale)


def _mix_sample_kernel(p_ref, *refs, seq, dqk, d_model, n_alias):
    lg_refs = refs[:LOG_DECAY_TERMS]
    s0_ref, buf_ref, gn_ref, wp_ref, ps_ref = refs[LOG_DECAY_TERMS:LOG_DECAY_TERMS + 5]
    m_ref, s_out_ref, pool_out_ref, car_ref = refs[LOG_DECAY_TERMS + 5 + n_alias:]
    nseq = s0_ref.shape[0]
    t = nseq * seq
    groups, pin, dv = wp_ref.shape
    dk = dqk // N_HEADS
    q0, k0, v0, og0, u0, ga0, gb0 = _p_columns(dqk, d_model, groups * pin)

    car_ref[...] = jnp.zeros_like(car_ref)
    for j in range(nseq):
        car_ref[j * CARRY_ROWS + 1:(j + 1) * CARRY_ROWS, :] = buf_ref[j]

    consts = _gla_consts(t, seq)
    loga = tuple(ref[...] for ref in lg_refs)
    e0_all = _dot_parts(consts["tri_local"], loga)
    g_all = jnp.exp(_dot_parts(consts["sel"], loga))
    tot = consts["total_kind"] * consts["nblk"]
    decay_all = [_column_tile(g_all[tot + j:tot + j + 1, :]) for j in range(nseq)]
    pos = PAST_LEN + _rem(_iota2((t, 1), 0), seq)
    keep = POOL_BUF - seq
    gn = gn_ref[...]

    u_all = p_ref[:, u0:u0 + groups * pin].astype(F32)
    heads = range(N_HEADS)
    ops = []
    for h in heads:
        q = p_ref[:, q0 + h * dk:q0 + (h + 1) * dk].astype(F32)
        k = p_ref[:, k0 + h * dk:k0 + (h + 1) * dk].astype(F32)
        ops.append(_gla_operands(q, k, e0_all[:, h * dk:(h + 1) * dk], g_all[:, h * dk:(h + 1) * dk], consts))
    atts = [_gla_scores(ops[h][0], consts).astype(BF16) for h in heads]
    outs = []
    for h in heads:
        _, qeb, kdb = ops[h]
        vb = p_ref[:, v0 + h * dv:v0 + (h + 1) * dv]
        o_intra = _dot(atts[h], vb)
        o_parts = []
        for j in range(nseq):
            s0 = s0_ref[j, h]
            lo, hi = j * seq, (j + 1) * seq
            o_parts.append(o_intra[lo:hi] + _dot(qeb[lo:hi], s0.astype(BF16)))
            decay = decay_all[j][h * dk:(h + 1) * dk]
            s_out_ref[j, h] = _lane_tile(decay, dv) * s0 + _dot_tn(kdb[lo:hi], vb[lo:hi])
        outs.append(_cat(o_parts, 0))

    for h in heads:
        d = _pool_delta(u_all[:, h * pin:(h + 1) * pin], (p_ref[:, u0 + h * pin:u0 + (h + 1) * pin],),
                        _split(car_ref[:, h * pin:(h + 1) * pin], CARRY_TERMS), POOL_WINDOWS[h], pos, seq)
        merged = _mix_merge(outs[h], p_ref[:, og0 + h * dv:og0 + (h + 1) * dv],
                            p_ref[:, ga0 + h * dv:ga0 + (h + 1) * dv], p_ref[:, gb0 + h * dv:gb0 + (h + 1) * dv],
                            _dot(d.astype(BF16), wp_ref[h]), gn, ps_ref[:, h * dv:(h + 1) * dv])
        m_ref[:, h * dv:(h + 1) * dv] = merged.astype(BF16)

    for j in range(nseq):
        pool_out_ref[j, 0:keep, :] = buf_ref[j, seq:POOL_BUF, :]
        pool_out_ref[j, keep:POOL_BUF, :] = u_all[j * seq:(j + 1) * seq]


def _mix_sample_call(p, lg, row_off, seq, layer, state_gla, state_pool, gla_norm_g, w_pool,
                     pool_scale, prev_gla, prev_pool, nseq=4):
    depth, bs, nh, dk, dv = state_gla.shape
    dqk = lg[0].shape[1]
    groups, pin, _ = w_pool.shape[1:]
    d_model = pool_scale.shape[2]
    n_p = p.shape[1]
    assert seq <= POOL_BUF and seq % 8 == 0
    t = nseq * seq
    off = row_off // t
    const = lambda i: (0, 0)
    in_specs = [
        pl.BlockSpec((t, n_p), lambda i: (off + i, 0)),
        *[pl.BlockSpec((t, dqk), lambda i: (off + i, 0)) for _ in lg],
        pl.BlockSpec((None, nseq, nh, dk, dv), lambda i: (layer, i, 0, 0, 0)),
        pl.BlockSpec((None, nseq, POOL_BUF, groups * pin), lambda i: (layer, i, 0, 0)),
        _layer_spec((1, dv), layer, const),
        _layer_spec((groups, pin, dv), layer, lambda i: (0, 0, 0)),
        _layer_spec((1, d_model), layer, const),
    ]
    args = [p, *lg, state_gla, state_pool, gla_norm_g, w_pool, pool_scale]
    aliases = {}
    n_alias = 0
    if prev_gla is not None:
        in_specs += [pl.BlockSpec(memory_space=pl.ANY), pl.BlockSpec(memory_space=pl.ANY)]
        aliases = {len(args): 1, len(args) + 1: 2}
        args += [prev_gla, prev_pool]
        n_alias = 2
    out_specs = [
        pl.BlockSpec((t, d_model), lambda i: (i, 0)),
        pl.BlockSpec((None, nseq, nh, dk, dv), lambda i: (layer, i, 0, 0, 0)),
        pl.BlockSpec((None, nseq, POOL_BUF, groups * pin), lambda i: (layer, i, 0, 0)),
    ]
    out_shape = [
        jax.ShapeDtypeStruct((bs * seq, d_model), BF16),
        jax.ShapeDtypeStruct(state_gla.shape, F32),
        jax.ShapeDtypeStruct(state_pool.shape, F32),
    ]
    return pl.pallas_call(
        functools.partial(_mix_sample_kernel, seq=seq, dqk=dqk, d_model=d_model, n_alias=n_alias),
        grid=(bs // nseq,),
        in_specs=in_specs,
        out_specs=out_specs,
        out_shape=out_shape,
        scratch_shapes=[pltpu.VMEM((nseq * CARRY_ROWS, groups * pin), F32)],
        input_output_aliases=aliases,
        compiler_params=_cparams("arbitrary"),
        name="mix_sample",
    )(*args)


def _mm_o_kernel(ap_ref, as_ref, xp_ref, xs_ref, g1p_ref, g1s_ref, scp_ref, scs_ref, shp_ref, shs_ref,
                 w_ref, ng_ref, x1p_ref, x1s_ref, h_ref, wb_ref, *, rg):
    @pl.when(pl.program_id(0) == 0)
    def _():
        wb_ref[...] = w_ref[...].astype(BF16)

    def body(a_ref, x_ref, g1_ref, sc_ref, sh_ref, x1_ref):
        rows = rg.rows
        x1 = _rows2d(x_ref) + _per_row(g1_ref, rows) * _dot(a_ref[...], wb_ref[...])
        x1_ref[...] = x1.reshape(x1_ref.shape)
        h_ref[...] = _modulated_norm(x1, ng_ref[...], _per_row(sc_ref, rows), _per_row(sh_ref, rows))

    rg.run(body, (ap_ref, xp_ref, g1p_ref, scp_ref, shp_ref, x1p_ref),
           (as_ref, xs_ref, g1s_ref, scs_ref, shs_ref, x1s_ref))


def _mm_o_call(merged_p, merged_s, xp, xs, w_o, mod_p, mod_s, layer, norm2_g, rows=256):
    rg = _RowGrid(xp.shape, xs.shape, rows)
    d = rg.d
    m_total = xp.shape[0] * xp.shape[1] + xs.shape[0] * xs.shape[1]
    xsp = rg.x_specs()
    return pl.pallas_call(
        functools.partial(_mm_o_kernel, rg=rg),
        grid=rg.grid,
        in_specs=[*rg.rows_specs(d), *xsp, *rg.mod_specs(layer, 2), *rg.mod_specs(layer, 4),
                  *rg.mod_specs(layer, 3), _resident_spec((d, d), layer),
                  _layer_spec((1, d), layer, lambda i: (0, 0))],
        out_specs=[*xsp, rg.joint_spec(d)],
        out_shape=[jax.ShapeDtypeStruct(xp.shape, F32), jax.ShapeDtypeStruct(xs.shape, F32),
                   jax.ShapeDtypeStruct((m_total, d), BF16)],
        scratch_shapes=[pltpu.VMEM((d, d), BF16)],
        compiler_params=_cparams("arbitrary"),
        name="out_proj",
    )(merged_p, merged_s, xp, xs, mod_p, mod_s, mod_p, mod_s, mod_p, mod_s, w_o, norm2_g)


def _mm_gu_kernel(a_ref, wg_ref, wu_ref, wd_ref, o_ref, wdb_ref, wgb_ref, wub_ref, *, n_slabs):
    @pl.when(pl.program_id(1) == 0)
    def _():
        wgb_ref[...] = wg_ref[...].astype(BF16)
        wub_ref[...] = wu_ref[...].astype(BF16)

    @pl.when(pl.program_id(0) * pl.num_programs(1) + pl.program_id(1) < n_slabs)
    def _():
        wdb_ref[...] = wd_ref[...].astype(BF16)

    a = a_ref[...]
    o_ref[...] = (_silu(_dot(a, wgb_ref[...])) * _dot(a, wub_ref[...])).astype(BF16)


def _mm_gu_call(h, w_gu, w_down, layer):
    m, k = h.shape
    tm = _largest_row_tile(m, ROW_TILE_CAP)
    dff = w_gu.shape[2] // 2
    d = w_down.shape[2]
    tn = _largest_tile(dff, 512)
    nj, ni = dff // tn, m // tm
    slab = next(r for r in range(16, dff + 1, 16) if dff % r == 0 and dff // r <= nj * ni)
    n_slabs = dff // slab
    slab_index = lambda j, i: (jnp.minimum(j * ni + i, n_slabs - 1), 0)
    return pl.pallas_call(
        functools.partial(_mm_gu_kernel, n_slabs=n_slabs),
        grid=(nj, ni),
        in_specs=[
            pl.BlockSpec((tm, k), lambda j, i: (i, 0)),
            _layer_spec((k, tn), layer, lambda j, i: (0, j)),
            _layer_spec((k, tn), layer, lambda j, i: (0, nj + j)),
            _layer_spec((slab, d), layer, slab_index),
        ],
        out_specs=[pl.BlockSpec((tm, tn), lambda j, i: (i, j)), pl.BlockSpec((slab, d), slab_index)],
        out_shape=[jax.ShapeDtypeStruct((m, dff), BF16), jax.ShapeDtypeStruct((dff, d), BF16)],
        scratch_shapes=[pltpu.VMEM((k, tn), BF16), pltpu.VMEM((k, tn), BF16)],
        compiler_params=_cparams("arbitrary", "arbitrary"),
        name="gate_up",
    )(h, w_gu, w_gu, w_down)


def _mm_down_kernel(a_ref, xp_ref, xs_ref, g2p_ref, g2s_ref, w_ref, ng_ref, *rest, rg, final):
    if final:
        yp_ref, ys_ref = rest

        def body(x_ref, g2_ref, y_ref):
            x2 = _rows2d(x_ref) + _per_row(g2_ref, rg.rows) * _dot(a_ref[...], w_ref[...])
            y_ref[...] = (_rms(x2) * ng_ref[...]).reshape(y_ref.shape)

        rg.run(body, (xp_ref, g2p_ref, yp_ref), (xs_ref, g2s_ref, ys_ref))
    else:
        scp_ref, scs_ref, shp_ref, shs_ref, wl_ref, wa2_ref, ba_ref, x2p_ref, x2s_ref, h_ref, *lg_refs = rest

        def body(x_ref, g2_ref, sc_ref, sh_ref, x2_ref):
            rows = rg.rows
            x2 = _rows2d(x_ref) + _per_row(g2_ref, rows) * _dot(a_ref[...], w_ref[...])
            x2_ref[...] = x2.reshape(x2_ref.shape)
            h = _modulated_norm(x2, ng_ref[...], _per_row(sc_ref, rows), _per_row(sh_ref, rows))
            h_ref[...] = h
            for ref, term in zip(lg_refs, _log_decay_terms(h, wl_ref, wa2_ref, ba_ref)):
                ref[...] = term

        rg.run(body, (xp_ref, g2p_ref, scp_ref, shp_ref, x2p_ref), (xs_ref, g2s_ref, scs_ref, shs_ref, x2s_ref))


def _mm_down_call(act, xp, xs, w_down, mod_p, mod_s, layer, norm_g, norm_layer, final, w_alr_t, w_a2, b_a,
                  rows=256):
    rg = _RowGrid(xp.shape, xs.shape, rows)
    d = rg.d
    dff = w_down.shape[0]
    m_total = act.shape[0]
    xsp = rg.x_specs()
    in_specs = [rg.joint_spec(dff), *xsp, *rg.mod_specs(layer, 5), _resident_spec((dff, d)),
                _layer_spec((1, d), norm_layer, lambda i: (0, 0))]
    args = [act, xp, xs, mod_p, mod_s, w_down, norm_g]
    x_shapes = [jax.ShapeDtypeStruct(xp.shape, F32), jax.ShapeDtypeStruct(xs.shape, F32)]
    if final:
        out_specs, out_shape = [*xsp], x_shapes
    else:
        dqk = w_a2.shape[2]
        gate_in, gate_out = _gate_specs(rg, layer + 1, d, dqk)
        in_specs += [*rg.mod_specs(layer + 1, 1), *rg.mod_specs(layer + 1, 0), *gate_in]
        args += [mod_p, mod_s, mod_p, mod_s, w_alr_t, w_a2, b_a]
        out_specs = [*xsp, rg.joint_spec(d), *gate_out]
        out_shape = x_shapes + [jax.ShapeDtypeStruct((m_total, d), BF16)] \
            + [jax.ShapeDtypeStruct((m_total, dqk), BF16)] * LOG_DECAY_TERMS
    return pl.pallas_call(
        functools.partial(_mm_down_kernel, rg=rg, final=final),
        grid=rg.grid,
        in_specs=in_specs,
        out_specs=out_specs,
        out_shape=out_shape,
        compiler_params=_cparams("arbitrary"),
        name="down_proj",
    )(*args)


def kernel(x_prompt, x_sample, state_gla, state_pool, c_prompt, c_sample, w_ada, b_ada, norm1_g, w_in, w_a2, b_a,
           gla_norm_g, w_pool, pool_scale, w_o, norm2_g, w_gu, w_down, final_norm_g):
    depth, d, n6 = w_ada.shape
    bp, lp, _ = x_prompt.shape
    bs, ls, _ = x_sample.shape
    mp = bp * lp
    rank, dqk = w_a2.shape[1:]
    n_main = w_in.shape[2] - rank

    c_all = jnp.concatenate([c_sample, c_prompt], axis=0)
    c_all = jnp.pad(c_all, ((0, (-c_all.shape[0]) % 8), (0, 0)))
    mod_s = _ada_call(c_all, w_ada, b_ada)
    mod_p = mod_s[:, bs:bs + bp].reshape(depth, bp, 1, n6)

    w_in_t = jnp.swapaxes(w_in, 1, 2)
    w_alr_b = jnp.pad(w_in_t[:, n_main:, :], ((0, 0), (0, LANE - rank), (0, 0))).astype(BF16)
    w_a2_b = jnp.pad(w_a2, ((0, 0), (0, LANE - rank), (0, 0))).astype(BF16)
    w_pool_b = w_pool.astype(BF16)
    n1 = norm1_g.reshape(depth, 1, d)
    n2 = norm2_g.reshape(depth, 1, d)
    nf = final_norm_g.reshape(1, 1, d)
    ba3 = b_a.reshape(depth, 1, dqk)
    gn3 = gla_norm_g.reshape(depth, 1, -1)
    ps3 = pool_scale.reshape(depth, 1, d)

    xp, xs = x_prompt, x_sample
    h, *lg = _pre_call(xp, xs, mod_p, mod_s, 0, n1, w_alr_b, w_a2_b, ba3)

    gla_p, pool_p = [], []
    gla_s = pool_s = None
    for l in range(depth):
        p = _mm_in_call(h, w_in_t, l, dqk, d, w_pool.shape[1] * w_pool.shape[2])
        merged_p, s_p, b_p = _mix_prompt_call(p, lg, bp, lp, l, gn3, w_pool_b, ps3)
        merged_s, gla_s, pool_s = _mix_sample_call(p, lg, mp, ls, l, state_gla, state_pool, gn3, w_pool_b, ps3,
                                                   gla_s, pool_s)
        gla_p.append(s_p)
        pool_p.append(b_p)

        xp, xs, h2 = _mm_o_call(merged_p, merged_s, xp, xs, w_o, mod_p, mod_s, l, n2)
        act, w_down_b = _mm_gu_call(h2, w_gu, w_down, l)
        if l + 1 < depth:
            xp, xs, h, *lg = _mm_down_call(act, xp, xs, w_down_b, mod_p, mod_s, l, n1, l + 1, False, w_alr_b,
                                           w_a2_b, ba3)
        else:
            yp, ys = _mm_down_call(act, xp, xs, w_down_b, mod_p, mod_s, l, nf, 0, True, None, None, None)
    return yp, ys, jnp.stack(gla_p), jnp.stack(pool_p), gla_s, pool_s
```

```python
import functools

import jax
import jax.numpy as jnp
from jax import lax
from jax.experimental import pallas as pl
from jax.experimental.pallas import tpu as pltpu

F32 = jnp.float32
BF16 = jnp.bfloat16

N_HEADS = 4
POOL_WINDOWS = (2, 4, 8, 16)
POOL_BUF = 15
CARRY_ROWS = POOL_BUF + 1
GATE_NORMALIZER = 16.0
PAST_LEN = 16384
EPS = 1e-6
STABLE_BLOCK = 16
LOG_DECAY_TERMS = 2
CARRY_TERMS = 3
LANE = 128
MXU_DIM = 256
ROW_TILE_CAP = 1024
VMEM_LIMIT = 56 * 1024 * 1024


def _cparams(*sem):
    return pltpu.CompilerParams(dimension_semantics=sem, vmem_limit_bytes=VMEM_LIMIT)


def _largest_tile(n, cap):
    best = None
    for t in range(LANE, cap + 1, LANE):
        if n % t == 0:
            best = t
    assert best is not None, (n, cap)
    return best


def _largest_row_tile(m, cap):
    best = None
    for t in range(MXU_DIM, cap + 1, MXU_DIM):
        if m % t == 0:
            best = t
    assert best is not None, (m, cap)
    return best


def _dot(a, b):
    return jnp.dot(a, b, preferred_element_type=F32)


def _dot_nt(a, b):
    return lax.dot_general(a, b, (((1,), (1,)), ((), ())), preferred_element_type=F32)


def _dot_tn(a, b):
    return lax.dot_general(a, b, (((0,), (0,)), ((), ())), preferred_element_type=F32)


def _split(x, terms):
    parts = []
    for i in range(terms):
        p = x.astype(BF16)
        parts.append(p)
        if i + 1 < terms:
            x = x - p.astype(F32)
    return tuple(parts)


def _dot_parts(m, parts):
    out = _dot(m, parts[0])
    for p in parts[1:]:
        out = out + _dot(m, p)
    return out


def _column_tile(row):
    return jnp.broadcast_to(row, (LANE, row.shape[1])).T


def _sigmoid(x):
    return 1.0 / (1.0 + jnp.exp(-x))


def _silu(x):
    return x * _sigmoid(x)


def _log_sigmoid(x):
    return jnp.minimum(x, 0.0) - jnp.log(1.0 + jnp.exp(-jnp.abs(x)))


def _rms(x):
    return x * lax.rsqrt(jnp.mean(x * x, axis=-1, keepdims=True) + EPS)


def _rows_bcast(x, row, n):
    return jnp.broadcast_to(x[row:row + 1, :], (n, x.shape[1]))


def _cat(parts, axis):
    return parts[0] if len(parts) == 1 else jnp.concatenate(parts, axis=axis)


def _iota2(shape, dim):
    return lax.broadcasted_iota(jnp.int32, shape, dim)


def _blk(x, n):
    assert n & (n - 1) == 0
    return lax.shift_right_logical(x, jnp.int32(n.bit_length() - 1))


def _rem(x, n):
    assert n & (n - 1) == 0
    return x & jnp.int32(n - 1)


def _layer_spec(block, layer, tail):
    return pl.BlockSpec((None,) + block, lambda *g: (layer,) + tail(*g))


def _resident_spec(block, layer=None):
    zeros = (0,) * len(block)
    if layer is None:
        return pl.BlockSpec(block, lambda *g: zeros, pipeline_mode=pl.Buffered(1))
    return pl.BlockSpec((None,) + block, lambda *g: (layer,) + zeros, pipeline_mode=pl.Buffered(1))


class _RowGrid:
    def __init__(self, xp_shape, xs_shape, rows):
        bp, lp, d = xp_shape
        bs, ls, _ = xs_shape
        assert lp % rows == 0 and rows % ls == 0 and bs % (rows // ls) == 0
        self.rows, self.d = rows, d
        self.tiles_per_seq = lp // rows
        self.n_p = bp * self.tiles_per_seq
        self.seqs_per_tile = rows // ls
        self.n_s = bs // self.seqs_per_tile
        self.ls = ls
        self.grid = (self.n_p + self.n_s,)

    def _ip(self, i):
        return jnp.minimum(i, self.n_p - 1)

    def _is(self, i):
        return jnp.maximum(i - self.n_p, 0)

    def x_specs(self):
        tps = self.tiles_per_seq
        return (pl.BlockSpec((1, self.rows, self.d), lambda i: (self._ip(i) // tps, self._ip(i) % tps, 0)),
                pl.BlockSpec((self.seqs_per_tile, self.ls, self.d), lambda i: (self._is(i), 0, 0)))

    def mod_specs(self, layer, j):
        tps = self.tiles_per_seq
        return (pl.BlockSpec((None, None, 1, self.d), lambda i: (layer, self._ip(i) // tps, 0, j)),
                pl.BlockSpec((None, self.seqs_per_tile, self.d), lambda i: (layer, self._is(i), j)))

    def rows_specs(self, width):
        return (pl.BlockSpec((self.rows, width), lambda i: (self._ip(i), 0)),
                pl.BlockSpec((self.rows, width), lambda i: (self._is(i), 0)))

    def joint_spec(self, width):
        return pl.BlockSpec((self.rows, width), lambda i: (i, 0))

    def run(self, body, prompt_refs, sample_refs):
        i = pl.program_id(0)

        @pl.when(i < self.n_p)
        def _():
            body(*prompt_refs)

        @pl.when(i >= self.n_p)
        def _():
            body(*sample_refs)


def _rows2d(x_ref):
    bb, tt, d = x_ref.shape
    return x_ref[...].reshape(bb * tt, d)


def _per_row(mod_ref, rows):
    m = mod_ref[...]
    n = m.shape[0]
    if n == 1:
        return m
    return _cat([_rows_bcast(m, s, rows // n) for s in range(n)], 0)


def _modulated_norm(x, g, sc, sh):
    return (_rms(x) * g * (1.0 + sc) + sh).astype(BF16)


def _ada_kernel(c_ref, w_ref, b_ref, o_ref):
    a = _silu(c_ref[...]).astype(BF16)
    o_ref[...] = _dot(a, w_ref[...].astype(BF16)) + b_ref[...]


def _ada_call(c_all, w_ada, b_ada):
    depth, d, n6 = w_ada.shape
    bp = c_all.shape[0]
    tn = _largest_tile(n6, 2048)
    return pl.pallas_call(
        _ada_kernel,
        grid=(depth, n6 // tn),
        in_specs=[
            pl.BlockSpec((bp, d), lambda l, j: (0, 0)),
            pl.BlockSpec((None, d, tn), lambda l, j: (l, 0, j)),
            pl.BlockSpec((None, 1, tn), lambda l, j: (l, 0, j)),
        ],
        out_specs=pl.BlockSpec((None, bp, tn), lambda l, j: (l, 0, j)),
        out_shape=jax.ShapeDtypeStruct((depth, bp, n6), F32),
        compiler_params=_cparams("arbitrary", "arbitrary"),
        name="ada_mod",
    )(c_all, w_ada, b_ada.reshape(depth, 1, n6))


def _log_decay_terms(h, wl_ref, wa2_ref, ba_ref):
    alr = _dot_nt(h, wl_ref[...])
    gk = _dot(alr.astype(BF16), wa2_ref[...]) + ba_ref[...]
    return _split(_log_sigmoid(gk) * (1.0 / GATE_NORMALIZER), LOG_DECAY_TERMS)


def _gate_specs(rg, layer, d, dqk):
    const = lambda i: (0, 0)
    ins = [_layer_spec((LANE, d), layer, const), _layer_spec((LANE, dqk), layer, const),
           _layer_spec((1, dqk), layer, const)]
    outs = [rg.joint_spec(dqk)] * LOG_DECAY_TERMS
    return ins, outs


def _pre_kernel(xp_ref, xs_ref, scp_ref, scs_ref, shp_ref, shs_ref, g_ref, wl_ref, wa2_ref, ba_ref,
                h_ref, *lg_refs, rg):
    def body(x_ref, sc_ref, sh_ref):
        h = _modulated_norm(_rows2d(x_ref), g_ref[...], _per_row(sc_ref, rg.rows), _per_row(sh_ref, rg.rows))
        h_ref[...] = h
        for ref, term in zip(lg_refs, _log_decay_terms(h, wl_ref, wa2_ref, ba_ref)):
            ref[...] = term

    rg.run(body, (xp_ref, scp_ref, shp_ref), (xs_ref, scs_ref, shs_ref))


def _pre_call(xp, xs, mod_p, mod_s, layer, norm_g, w_alr_t, w_a2, b_a, rows=512):
    rg = _RowGrid(xp.shape, xs.shape, rows)
    d = rg.d
    dqk = w_a2.shape[2]
    m_total = xp.shape[0] * xp.shape[1] + xs.shape[0] * xs.shape[1]
    gate_in, gate_out = _gate_specs(rg, layer, d, dqk)
    return pl.pallas_call(
        functools.partial(_pre_kernel, rg=rg),
        grid=rg.grid,
        in_specs=[*rg.x_specs(), *rg.mod_specs(layer, 1), *rg.mod_specs(layer, 0),
                  _layer_spec((1, d), layer, lambda i: (0, 0)), *gate_in],
        out_specs=[rg.joint_spec(d), *gate_out],
        out_shape=[jax.ShapeDtypeStruct((m_total, d), BF16)]
        + [jax.ShapeDtypeStruct((m_total, dqk), BF16)] * LOG_DECAY_TERMS,
        compiler_params=_cparams("arbitrary"),
        name="prologue",
    )(xp, xs, mod_p, mod_s, mod_p, mod_s, norm_g, w_alr_t, w_a2, b_a)


def _mm_in_kernel(a_ref, w_ref, p_ref, wb_ref, *, acts, scale, n_slabs):
    s, i = pl.program_id(0), pl.program_id(1)
    slab_rows = w_ref.shape[0]

    @pl.when((s < len(acts)) & (i < n_slabs))
    def _():
        wb_ref[s % 2, pl.ds(pl.multiple_of(i * slab_rows, slab_rows), slab_rows), :] = w_ref[...].astype(BF16)

    fns = {"id": lambda y: y, "scale": lambda y: y * scale, "silu": _silu, "sigmoid": _sigmoid}
    for name in sorted(set(acts)):
        cond = functools.reduce(jnp.logical_or, [s == jj + 1 for jj, a in enumerate(acts) if a == name])

        @pl.when(cond)
        def _(fn=fns[name]):
            p_ref[...] = fn(_dot_nt(a_ref[...], wb_ref[(s + 1) % 2])).astype(BF16)


def _mm_in_call(h, w_in_t, layer, dqk, d_model, d_pool):
    m, k = h.shape
    tm = _largest_row_tile(m, ROW_TILE_CAP)
    ni = m // tm
    starts = _p_columns(dqk, d_model, d_pool)
    n_main = starts[-1] + d_model
    tn = LANE
    for cand in range(LANE, 1024 + 1, LANE):
        if all(s % cand == 0 for s in starts + (n_main,)):
            tn = cand
    nj = n_main // tn
    kinds = ("scale", "id", "id", "silu", "id", "sigmoid", "sigmoid")
    bounds = starts[1:] + (n_main,)
    acts = []
    for jj in range(nj):
        acts.append(kinds[next(i for i, e in enumerate(bounds) if jj * tn < e)])
    dk = dqk // N_HEADS
    n_slabs = max(n for n in range(1, ni + 1) if tn % n == 0 and (tn // n) % 16 == 0)
    slab_rows = tn // n_slabs
    slab_index = lambda s, i: (jnp.minimum(s * n_slabs + jnp.minimum(i, n_slabs - 1), nj * n_slabs - 1), 0)
    row_index = lambda s, i: jnp.where(s == 0, 0, i)
    return pl.pallas_call(
        functools.partial(_mm_in_kernel, acts=tuple(acts), scale=dk ** -0.5, n_slabs=n_slabs),
        grid=(nj + 1, ni),
        in_specs=[
            pl.BlockSpec((tm, k), lambda s, i: (row_index(s, i), 0)),
            _layer_spec((slab_rows, k), layer, slab_index),
        ],
        out_specs=pl.BlockSpec((tm, tn), lambda s, i: (row_index(s, i), jnp.maximum(s - 1, 0))),
        out_shape=jax.ShapeDtypeStruct((m, n_main), BF16),
        scratch_shapes=[pltpu.VMEM((2, tn, k), BF16)],
        compiler_params=_cparams("arbitrary", "arbitrary"),
        name="in_proj",
    )(h, w_in_t)


def _gla_consts(t, seq):
    nb = min(STABLE_BLOCK, seq)
    nblk = t // nb
    levels = []
    n = nb * 2
    while n <= seq:
        levels.append(n)
        n *= 2
    nkinds = 3 + len(levels)
    rows = -(-nkinds * nblk // 16) * 16
    row = _iota2((t, t), 0)
    col = _iota2((t, t), 1)
    rowv = _iota2((t, 1), 0)
    same0 = _blk(row, nb) == _blk(col, nb)
    r = _iota2((rows, t), 0)
    u = _iota2((rows, t), 1)
    kind = _blk(r, nblk)
    j = _rem(r, nblk)
    start = j * nb
    seq_start = start & jnp.int32(-seq)
    sel = (kind == 0) & (u >= seq_start) & (u < start)
    sel |= (kind == 1) & (u >= start) & (u < seq_start + seq)
    for li, n in enumerate(levels):
        mid = (start & jnp.int32(-n)) + n // 2
        between = ((u >= mid) & (u < start)) | ((u >= start) & (u < mid))
        sel |= (kind == 2 + li) & between
    sel |= (kind == nkinds - 1) & (_blk(u, seq) == j)
    return {
        "t": t, "seq": seq, "nb": nb, "nblk": nblk, "total_kind": nkinds - 1,
        "tri_local": (same0 & (col <= row)).astype(BF16),
        "sel": jnp.where(sel, 1.0, 0.0).astype(BF16),
        "mask0": same0 & (col <= row),
        "levels": [(_rem(rowv, n) >= n // 2, _blk(row, n) == _blk(col, n)) for n in levels],
    }


def _gla_operands(q, k, e0, g, c):
    nb, nblk = c["nb"], c["nblk"]

    def per_block(kind):
        return _cat([_rows_bcast(g, kind * nblk + j, nb) for j in range(nblk)], 0)

    qe0 = q * jnp.exp(e0)
    ke0 = k * jnp.exp(-e0)
    pairs = [(qe0.astype(BF16), ke0.astype(BF16))]
    for li, (second, _) in enumerate(c["levels"]):
        f = per_block(2 + li)
        pairs.append((jnp.where(second, qe0 * f, 0.0).astype(BF16), jnp.where(second, 0.0, ke0 * f).astype(BF16)))
    qe = qe0 if nb == c["seq"] else qe0 * per_block(0)
    kd = ke0 * per_block(1)
    return pairs, qe.astype(BF16), kd.astype(BF16)


def _gla_scores(pairs, c):
    att = jnp.where(c["mask0"], _dot_nt(*pairs[0]), 0.0)
    for (ql, kl), (_, same) in zip(pairs[1:], c["levels"]):
        att = att + jnp.where(same, _dot_nt(ql, kl), 0.0)
    return att


def _lane_tile(x, width):
    return _cat([x] * (width // x.shape[1]), 1)


def _pool_delta(u, u_parts, carry_parts, w, pos0, seq):
    t = u.shape[0]
    npiece = t // seq
    row = _iota2((t, t), 0)
    col = _iota2((t, t), 1)
    band = ((_blk(row, seq) == _blk(col, seq)) & (col <= row) & (col > row - w)).astype(BF16)
    rc = _iota2((t, npiece * CARRY_ROWS), 0)
    cc = _iota2((t, npiece * CARRY_ROWS), 1)
    cband = ((_blk(rc, seq) == _blk(cc, CARRY_ROWS))
             & (_rem(cc, CARRY_ROWS) > _rem(rc, seq) + CARRY_ROWS - w)).astype(BF16)
    s = _dot_parts(band, u_parts) + _dot_parts(cband, carry_parts)
    cnt = jnp.minimum(pos0 + 1, w).astype(F32)
    return s / cnt - u


def _mix_merge(o, silu_og, sig_ga, sig_gb, pooled, gn, ps):
    y_a = _rms(o) * gn * silu_og.astype(F32)
    return sig_ga.astype(F32) * y_a + sig_gb.astype(F32) * (pooled * ps)


def _p_columns(dqk, d_model, d_pool):
    q0, k0, v0 = 0, dqk, 2 * dqk
    og0 = v0 + d_model
    u0 = og0 + d_model
    ga0 = u0 + d_pool
    gb0 = ga0 + d_model
    return q0, k0, v0, og0, u0, ga0, gb0


def _mix_prompt_kernel(p_ref, *refs, dqk, d_model):
    lg_refs = refs[:LOG_DECAY_TERMS]
    gn_ref, wp_ref, ps_ref, m_ref, s_ref, pool_out_ref, car_ref = refs[LOG_DECAY_TERMS:]
    c = pl.program_id(1)
    t = p_ref.shape[0]
    groups, pin, dv = wp_ref.shape
    dk = dqk // N_HEADS
    q0, k0, v0, og0, u0, ga0, gb0 = _p_columns(dqk, d_model, groups * pin)

    @pl.when(c == 0)
    def _():
        s_ref[...] = jnp.zeros_like(s_ref)
        car_ref[...] = jnp.zeros_like(car_ref)

    consts = _gla_consts(t, t)
    loga = tuple(ref[...] for ref in lg_refs)
    e0_all = _dot_parts(consts["tri_local"], loga)
    g_all = jnp.exp(_dot_parts(consts["sel"], loga))
    tot = consts["total_kind"] * consts["nblk"]
    decay_all = _column_tile(g_all[tot:tot + 1, :])
    pos = c * t + _iota2((t, 1), 0)
    gn = gn_ref[...]

    heads = range(N_HEADS)
    deltas = []
    for h in heads:
        ub = p_ref[:, u0 + h * pin:u0 + (h + 1) * pin]
        deltas.append(_pool_delta(ub.astype(F32), (ub,), (car_ref[:, h * pin:(h + 1) * pin],), POOL_WINDOWS[h],
                                  pos, t).astype(BF16))
    ops = []
    for h in heads:
        q = p_ref[:, q0 + h * dk:q0 + (h + 1) * dk].astype(F32)
        k = p_ref[:, k0 + h * dk:k0 + (h + 1) * dk].astype(F32)
        ops.append(_gla_operands(q, k, e0_all[:, h * dk:(h + 1) * dk], g_all[:, h * dk:(h + 1) * dk], consts))
    pooled = [_dot(deltas[h], wp_ref[h]) for h in heads]
    atts = [_gla_scores(ops[h][0], consts).astype(BF16) for h in heads]
    outs = []
    for h in heads:
        _, qe, kd = ops[h]
        vb = p_ref[:, v0 + h * dv:v0 + (h + 1) * dv]
        s = s_ref[h]
        outs.append(_dot(atts[h], vb) + _dot(qe, s.astype(BF16)))
        decay = decay_all[h * dk:(h + 1) * dk]
        s_ref[h] = _lane_tile(decay, dv) * s + _dot_tn(kd, vb)

    for h in heads:
        merged = _mix_merge(outs[h], p_ref[:, og0 + h * dv:og0 + (h + 1) * dv],
                            p_ref[:, ga0 + h * dv:ga0 + (h + 1) * dv], p_ref[:, gb0 + h * dv:gb0 + (h + 1) * dv],
                            pooled[h], gn, ps_ref[:, h * dv:(h + 1) * dv])
        m_ref[:, h * dv:(h + 1) * dv] = merged.astype(BF16)

    u_tail = p_ref[t - CARRY_ROWS:, u0:u0 + groups * pin]
    car_ref[...] = u_tail

    @pl.when(c == pl.num_programs(1) - 1)
    def _():
        pool_out_ref[...] = u_tail.astype(F32)[CARRY_ROWS - POOL_BUF:]


def _mix_prompt_call(p, lg, batch, seq_len, layer, gla_norm_g, w_pool, pool_scale, chunk=128):
    dqk = lg[0].shape[1]
    dk = dqk // N_HEADS
    groups, pin, dv = w_pool.shape[1:]
    d_model = pool_scale.shape[2]
    n_p = p.shape[1]
    nc = seq_len // chunk
    const = lambda b, c: (0, 0)
    in_specs = [
        pl.BlockSpec((chunk, n_p), lambda b, c: (b * nc + c, 0)),
        *[pl.BlockSpec((chunk, dqk), lambda b, c: (b * nc + c, 0)) for _ in lg],
        _layer_spec((1, dv), layer, const),
        _layer_spec((groups, pin, dv), layer, lambda b, c: (0, 0, 0)),
        _layer_spec((1, d_model), layer, const),
    ]
    out_specs = [
        pl.BlockSpec((chunk, d_model), lambda b, c: (b * nc + c, 0)),
        pl.BlockSpec((None, N_HEADS, dk, dv), lambda b, c: (b, 0, 0, 0)),
        pl.BlockSpec((None, POOL_BUF, groups * pin), lambda b, c: (b, 0, 0)),
    ]
    out_shape = [
        jax.ShapeDtypeStruct((batch * seq_len, d_model), BF16),
        jax.ShapeDtypeStruct((batch, N_HEADS, dk, dv), F32),
        jax.ShapeDtypeStruct((batch, POOL_BUF, groups * pin), F32),
    ]
    return pl.pallas_call(
        functools.partial(_mix_prompt_kernel, dqk=dqk, d_model=d_model),
        grid=(batch, nc),
        in_specs=in_specs,
        out_specs=out_specs,
        out_shape=out_shape,
        scratch_shapes=[pltpu.VMEM((CARRY_ROWS, groups * pin), BF16)],
        compiler_params=_cparams("arbitrary", "arbitrary"),
        name="mix_prompt",
    )(p, *lg, gla_norm_g, w_pool, pool_scale)


def _mix_sample_kernel(p_ref, *refs, seq, dqk, d_model, n_alias):
    lg_refs = refs[:LOG_DECAY_TERMS]
    s0_ref, buf_ref, gn_ref, wp_ref, ps_ref = refs[LOG_DECAY_TERMS:LOG_DECAY_TERMS + 5]
    m_ref, s_out_ref, pool_out_ref, car_ref = refs[LOG_DECAY_TERMS + 5 + n_alias:]
    nseq = s0_ref.shape[0]
    t = nseq * seq
    groups, pin, dv = wp_ref.shape
    dk = dqk // N_HEADS
    q0, k0, v0, og0, u0, ga0, gb0 = _p_columns(dqk, d_model, groups * pin)

    car_ref[...] = jnp.zeros_like(car_ref)
    for j in range(nseq):
        car_ref[j * CARRY_ROWS + 1:(j + 1) * CARRY_ROWS, :] = buf_ref[j]

    consts = _gla_consts(t, seq)
    loga = tuple(ref[...] for ref in lg_refs)
    e0_all = _dot_parts(consts["tri_local"], loga)
    g_all = jnp.exp(_dot_parts(consts["sel"], loga))
    tot = consts["total_kind"] * consts["nblk"]
    decay_all = [_column_tile(g_all[tot + j:tot + j + 1, :]) for j in range(nseq)]
    pos = PAST_LEN + _rem(_iota2((t, 1), 0), seq)
    keep = POOL_BUF - seq
    gn = gn_ref[...]

    u_all = p_ref[:, u0:u0 + groups * pin].astype(F32)
    heads = range(N_HEADS)
    ops = []
    for h in heads:
        q = p_ref[:, q0 + h * dk:q0 + (h + 1) * dk].astype(F32)
        k = p_ref[:, k0 + h * dk:k0 + (h + 1) * dk].astype(F32)
        ops.append(_gla_operands(q, k, e0_all[:, h * dk:(h + 1) * dk], g_all[:, h * dk:(h + 1) * dk], consts))
    atts = [_gla_scores(ops[h][0], consts).astype(BF16) for h in heads]
    outs = []
    for h in heads:
        _, qeb, kdb = ops[h]
        vb = p_ref[:, v0 + h * dv:v0 + (h + 1) * dv]
        o_intra = _dot(atts[h], vb)
        o_parts = []
        for j in range(nseq):
            s0 = s0_ref[j, h]
            lo, hi = j * seq, (j + 1) * seq
            o_parts.append(o_intra[lo:hi] + _dot(qeb[lo:hi], s0.astype(BF16)))
            decay = decay_all[j][h * dk:(h + 1) * dk]
            s_out_ref[j, h] = _lane_tile(decay, dv) * s0 + _dot_tn(kdb[lo:hi], vb[lo:hi])
        outs.append(_cat(o_parts, 0))

    for h in heads:
        d = _pool_delta(u_all[:, h * pin:(h + 1) * pin], (p_ref[:, u0 + h * pin:u0 + (h + 1) * pin],),
                        _split(car_ref[:, h * pin:(h + 1) * pin], CARRY_TERMS), POOL_WINDOWS[h], pos, seq)
        merged = _mix_merge(outs[h], p_ref[:, og0 + h * dv:og0 + (h + 1) * dv],
                            p_ref[:, ga0 + h * dv:ga0 + (h + 1) * dv], p_ref[:, gb0 + h * dv:gb0 + (h + 1) * dv],
                            _dot(d.astype(BF16), wp_ref[h]), gn, ps_ref[:, h * dv:(h + 1) * dv])
        m_ref[:, h * dv:(h + 1) * dv] = merged.astype(BF16)

    for j in range(nseq):
        pool_out_ref[j, 0:keep, :] = buf_ref[j, seq:POOL_BUF, :]
        pool_out_ref[j, keep:POOL_BUF, :] = u_all[j * seq:(j + 1) * seq]


def _mix_sample_call(p, lg, row_off, seq, layer, state_gla, state_pool, gla_norm_g, w_pool,
                     pool_scale, prev_gla, prev_pool, nseq=4):
    depth, bs, nh, dk, dv = state_gla.shape
    dqk = lg[0].shape[1]
    groups, pin, _ = w_pool.shape[1:]
    d_model = pool_scale.shape[2]
    n_p = p.shape[1]
    assert seq <= POOL_BUF and seq % 8 == 0
    t = nseq * seq
    off = row_off // t
    const = lambda i: (0, 0)
    in_specs = [
        pl.BlockSpec((t, n_p), lambda i: (off + i, 0)),
        *[pl.BlockSpec((t, dqk), lambda i: (off + i, 0)) for _ in lg],
        pl.BlockSpec((None, nseq, nh, dk, dv), lambda i: (layer, i, 0, 0, 0)),
        pl.BlockSpec((None, nseq, POOL_BUF, groups * pin), lambda i: (layer, i, 0, 0)),
        _layer_spec((1, dv), layer, const),
        _layer_spec((groups, pin, dv), layer, lambda i: (0, 0, 0)),
        _layer_spec((1, d_model), layer, const),
    ]
    args = [p, *lg, state_gla, state_pool, gla_norm_g, w_pool, pool_scale]
    aliases = {}
    n_alias = 0
    if prev_gla is not None:
        in_specs += [pl.BlockSpec(memory_space=pl.ANY), pl.BlockSpec(memory_space=pl.ANY)]
        aliases = {len(args): 1, len(args) + 1: 2}
        args += [prev_gla, prev_pool]
        n_alias = 2
    out_specs = [
        pl.BlockSpec((t, d_model), lambda i: (i, 0)),
        pl.BlockSpec((None, nseq, nh, dk, dv), lambda i: (layer, i, 0, 0, 0)),
        pl.BlockSpec((None, nseq, POOL_BUF, groups * pin), lambda i: (layer, i, 0, 0)),
    ]
    out_shape = [
        jax.ShapeDtypeStruct((bs * seq, d_model), BF16),
        jax.ShapeDtypeStruct(state_gla.shape, F32),
        jax.ShapeDtypeStruct(state_pool.shape, F32),
    ]
    return pl.pallas_call(
        functools.partial(_mix_sample_kernel, seq=seq, dqk=dqk, d_model=d_model, n_alias=n_alias),
        grid=(bs // nseq,),
        in_specs=in_specs,
        out_specs=out_specs,
        out_shape=out_shape,
        scratch_shapes=[pltpu.VMEM((nseq * CARRY_ROWS, groups * pin), F32)],
        input_output_aliases=aliases,
        compiler_params=_cparams("arbitrary"),
        name="mix_sample",
    )(*args)


def _mm_o_kernel(ap_ref, as_ref, xp_ref, xs_ref, g1p_ref, g1s_ref, scp_ref, scs_ref, shp_ref, shs_ref,
                 w_ref, ng_ref, x1p_ref, x1s_ref, h_ref, wb_ref, *, rg):
    @pl.when(pl.program_id(0) == 0)
    def _():
        wb_ref[...] = w_ref[...].astype(BF16)

    def body(a_ref, x_ref, g1_ref, sc_ref, sh_ref, x1_ref):
        rows = rg.rows
        x1 = _rows2d(x_ref) + _per_row(g1_ref, rows) * _dot(a_ref[...], wb_ref[...])
        x1_ref[...] = x1.reshape(x1_ref.shape)
        h_ref[...] = _modulated_norm(x1, ng_ref[...], _per_row(sc_ref, rows), _per_row(sh_ref, rows))

    rg.run(body, (ap_ref, xp_ref, g1p_ref, scp_ref, shp_ref, x1p_ref),
           (as_ref, xs_ref, g1s_ref, scs_ref, shs_ref, x1s_ref))


def _mm_o_call(merged_p, merged_s, xp, xs, w_o, mod_p, mod_s, layer, norm2_g, rows=256):
    rg = _RowGrid(xp.shape, xs.shape, rows)
    d = rg.d
    m_total = xp.shape[0] * xp.shape[1] + xs.shape[0] * xs.shape[1]
    xsp = rg.x_specs()
    return pl.pallas_call(
        functools.partial(_mm_o_kernel, rg=rg),
        grid=rg.grid,
        in_specs=[*rg.rows_specs(d), *xsp, *rg.mod_specs(layer, 2), *rg.mod_specs(layer, 4),
                  *rg.mod_specs(layer, 3), _resident_spec((d, d), layer),
                  _layer_spec((1, d), layer, lambda i: (0, 0))],
        out_specs=[*xsp, rg.joint_spec(d)],
        out_shape=[jax.ShapeDtypeStruct(xp.shape, F32), jax.ShapeDtypeStruct(xs.shape, F32),
                   jax.ShapeDtypeStruct((m_total, d), BF16)],
        scratch_shapes=[pltpu.VMEM((d, d), BF16)],
        compiler_params=_cparams("arbitrary"),
        name="out_proj",
    )(merged_p, merged_s, xp, xs, mod_p, mod_s, mod_p, mod_s, mod_p, mod_s, w_o, norm2_g)


def _mm_gu_kernel(a_ref, wg_ref, wu_ref, wd_ref, o_ref, wdb_ref, wgb_ref, wub_ref, *, n_slabs):
    @pl.when(pl.program_id(1) == 0)
    def _():
        wgb_ref[...] = wg_ref[...].astype(BF16)
        wub_ref[...] = wu_ref[...].astype(BF16)

    @pl.when(pl.program_id(0) * pl.num_programs(1) + pl.program_id(1) < n_slabs)
    def _():
        wdb_ref[...] = wd_ref[...].astype(BF16)

    a = a_ref[...]
    o_ref[...] = (_silu(_dot(a, wgb_ref[...])) * _dot(a, wub_ref[...])).astype(BF16)


def _mm_gu_call(h, w_gu, w_down, layer):
    m, k = h.shape
    tm = _largest_row_tile(m, ROW_TILE_CAP)
    dff = w_gu.shape[2] // 2
    d = w_down.shape[2]
    tn = _largest_tile(dff, 512)
    nj, ni = dff // tn, m // tm
    slab = next(r for r in range(16, dff + 1, 16) if dff % r == 0 and dff // r <= nj * ni)
    n_slabs = dff // slab
    slab_index = lambda j, i: (jnp.minimum(j * ni + i, n_slabs - 1), 0)
    return pl.pallas_call(
        functools.partial(_mm_gu_kernel, n_slabs=n_slabs),
        grid=(nj, ni),
        in_specs=[
            pl.BlockSpec((tm, k), lambda j, i: (i, 0)),
            _layer_spec((k, tn), layer, lambda j, i: (0, j)),
            _layer_spec((k, tn), layer, lambda j, i: (0, nj + j)),
            _layer_spec((slab, d), layer, slab_index),
        ],
        out_specs=[pl.BlockSpec((tm, tn), lambda j, i: (i, j)), pl.BlockSpec((slab, d), slab_index)],
        out_shape=[jax.ShapeDtypeStruct((m, dff), BF16), jax.ShapeDtypeStruct((dff, d), BF16)],
        scratch_shapes=[pltpu.VMEM((k, tn), BF16), pltpu.VMEM((k, tn), BF16)],
        compiler_params=_cparams("arbitrary", "arbitrary"),
        name="gate_up",
    )(h, w_gu, w_gu, w_down)


def _mm_down_kernel(a_ref, xp_ref, xs_ref, g2p_ref, g2s_ref, w_ref, ng_ref, *rest, rg, final):
    if final:
        yp_ref, ys_ref = rest

        def body(x_ref, g2_ref, y_ref):
            x2 = _rows2d(x_ref) + _per_row(g2_ref, rg.rows) * _dot(a_ref[...], w_ref[...])
            y_ref[...] = (_rms(x2) * ng_ref[...]).reshape(y_ref.shape)

        rg.run(body, (xp_ref, g2p_ref, yp_ref), (xs_ref, g2s_ref, ys_ref))
    else:
        scp_ref, scs_ref, shp_ref, shs_ref, wl_ref, wa2_ref, ba_ref, x2p_ref, x2s_ref, h_ref, *lg_refs = rest

        def body(x_ref, g2_ref, sc_ref, sh_ref, x2_ref):
            rows = rg.rows
            x2 = _rows2d(x_ref) + _per_row(g2_ref, rows) * _dot(a_ref[...], w_ref[...])
            x2_ref[...] = x2.reshape(x2_ref.shape)
            h = _modulated_norm(x2, ng_ref[...], _per_row(sc_ref, rows), _per_row(sh_ref, rows))
            h_ref[...] = h
            for ref, term in zip(lg_refs, _log_decay_terms(h, wl_ref, wa2_ref, ba_ref)):
                ref[...] = term

        rg.run(body, (xp_ref, g2p_ref, scp_ref, shp_ref, x2p_ref), (xs_ref, g2s_ref, scs_ref, shs_ref, x2s_ref))


def _mm_down_call(act, xp, xs, w_down, mod_p, mod_s, layer, norm_g, norm_layer, final, w_alr_t, w_a2, b_a,
                  rows=256):
    rg = _RowGrid(xp.shape, xs.shape, rows)
    d = rg.d
    dff = w_down.shape[0]
    m_total = act.shape[0]
    xsp = rg.x_specs()
    in_specs = [rg.joint_spec(dff), *xsp, *rg.mod_specs(layer, 5), _resident_spec((dff, d)),
                _layer_spec((1, d), norm_layer, lambda i: (0, 0))]
    args = [act, xp, xs, mod_p, mod_s, w_down, norm_g]
    x_shapes = [jax.ShapeDtypeStruct(xp.shape, F32), jax.ShapeDtypeStruct(xs.shape, F32)]
    if final:
        out_specs, out_shape = [*xsp], x_shapes
    else:
        dqk = w_a2.shape[2]
        gate_in, gate_out = _gate_specs(rg, layer + 1, d, dqk)
        in_specs += [*rg.mod_specs(layer + 1, 1), *rg.mod_specs(layer + 1, 0), *gate_in]
        args += [mod_p, mod_s, mod_p, mod_s, w_alr_t, w_a2, b_a]
        out_specs = [*xsp, rg.joint_spec(d), *gate_out]
        out_shape = x_shapes + [jax.ShapeDtypeStruct((m_total, d), BF16)] \
            + [jax.ShapeDtypeStruct((m_total, dqk), BF16)] * LOG_DECAY_TERMS
    return pl.pallas_call(
        functools.partial(_mm_down_kernel, rg=rg, final=final),
        grid=rg.grid,
        in_specs=in_specs,
        out_specs=out_specs,
        out_shape=out_shape,
        compiler_params=_cparams("arbitrary"),
        name="down_proj",
    )(*args)


def kernel(x_prompt, x_sample, state_gla, state_pool, c_prompt, c_sample, w_ada, b_ada, norm1_g, w_in, w_a2, b_a,
           gla_norm_g, w_pool, pool_scale, w_o, norm2_g, w_gu, w_down, final_norm_g):
    depth, d, n6 = w_ada.shape
    bp, lp, _ = x_prompt.shape
    bs, ls, _ = x_sample.shape
    mp = bp * lp
    rank, dqk = w_a2.shape[1:]
    n_main = w_in.shape[2] - rank

    c_all = jnp.concatenate([c_sample, c_prompt], axis=0)
    c_all = jnp.pad(c_all, ((0, (-c_all.shape[0]) % 8), (0, 0)))
    mod_s = _ada_call(c_all, w_ada, b_ada)
    mod_p = mod_s[:, bs:bs + bp].reshape(depth, bp, 1, n6)

    w_in_t = jnp.swapaxes(w_in, 1, 2)
    w_alr_b = jnp.pad(w_in_t[:, n_main:, :], ((0, 0), (0, LANE - rank), (0, 0))).astype(BF16)
    w_a2_b = jnp.pad(w_a2, ((0, 0), (0, LANE - rank), (0, 0))).astype(BF16)
    w_pool_b = w_pool.astype(BF16)
    n1 = norm1_g.reshape(depth, 1, d)
    n2 = norm2_g.reshape(depth, 1, d)
    nf = final_norm_g.reshape(1, 1, d)
    ba3 = b_a.reshape(depth, 1, dqk)
    gn3 = gla_norm_g.reshape(depth, 1, -1)
    ps3 = pool_scale.reshape(depth, 1, d)

    xp, xs = x_prompt, x_sample
    h, *lg = _pre_call(xp, xs, mod_p, mod_s, 0, n1, w_alr_b, w_a2_b, ba3)

    gla_p, pool_p = [], []
    gla_s = pool_s = None
    for l in range(depth):
        p = _mm_in_call(h, w_in_t, l, dqk, d, w_pool.shape[1] * w_pool.shape[2])
        merged_p, s_p, b_p = _mix_prompt_call(p, lg, bp, lp, l, gn3, w_pool_b, ps3)
        merged_s, gla_s, pool_s = _mix_sample_call(p, lg, mp, ls, l, state_gla, state_pool, gn3, w_pool_b, ps3,
                                                   gla_s, pool_s)
        gla_p.append(s_p)
        pool_p.append(b_p)

        xp, xs, h2 = _mm_o_call(merged_p, merged_s, xp, xs, w_o, mod_p, mod_s, l, n2)
        act, w_down_b = _mm_gu_call(h2, w_gu, w_down, l)
        if l + 1 < depth:
            xp, xs, h, *lg = _mm_down_call(act, xp, xs, w_down_b, mod_p, mod_s, l, n1, l + 1, False, w_alr_b,
                                           w_a2_b, ba3)
        else:
            yp, ys = _mm_down_call(act, xp, xs, w_down_b, mod_p, mod_s, l, nf, 0, True, None, None, None)
    return yp, ys, jnp.stack(gla_p), jnp.stack(pool_p), gla_s, pool_s
```

```python
import functools

import jax
import jax.numpy as jnp
from jax import lax
from jax.experimental import pallas as pl
from jax.experimental.pallas import tpu as pltpu

F32 = jnp.float32
BF16 = jnp.bfloat16

N_HEADS = 4
POOL_WINDOWS = (2, 4, 8, 16)
POOL_BUF = 15
CARRY_ROWS = POOL_BUF + 1
GATE_NORMALIZER = 16.0
PAST_LEN = 16384
EPS = 1e-6
STABLE_BLOCK = 16
LOG_DECAY_TERMS = 2
CARRY_TERMS = 3
LANE = 128
MXU_DIM = 256
VMEM_LIMIT = 56 * 1024 * 1024

ROW_TILE_CAP = 1024
IN_PROJ_COLS_CAP = 1024
GATE_UP_COLS_CAP = 512
ADA_COLS_CAP = 2048
PROLOGUE_ROWS = 512
EPILOGUE_ROWS = 256
MIX_CHUNK = 128
SAMPLE_SEQS_PER_STEP = 4


def _cparams(*sem):
    return pltpu.CompilerParams(dimension_semantics=sem, vmem_limit_bytes=VMEM_LIMIT)


def _largest_tile(n, cap):
    best = None
    for t in range(LANE, cap + 1, LANE):
        if n % t == 0:
            best = t
    assert best is not None, (n, cap)
    return best


def _largest_row_tile(m, cap):
    best = None
    for t in range(MXU_DIM, cap + 1, MXU_DIM):
        if m % t == 0:
            best = t
    assert best is not None, (m, cap)
    return best


def _dot(a, b):
    return jnp.dot(a, b, preferred_element_type=F32)


def _dot_nt(a, b):
    return lax.dot_general(a, b, (((1,), (1,)), ((), ())), preferred_element_type=F32)


def _dot_tn(a, b):
    return lax.dot_general(a, b, (((0,), (0,)), ((), ())), preferred_element_type=F32)


def _split(x, terms):
    parts = []
    for i in range(terms):
        p = x.astype(BF16)
        parts.append(p)
        if i + 1 < terms:
            x = x - p.astype(F32)
    return tuple(parts)


def _dot_parts(m, parts):
    out = _dot(m, parts[0])
    for p in parts[1:]:
        out = out + _dot(m, p)
    return out


def _column_tile(row):
    return jnp.broadcast_to(row, (LANE, row.shape[1])).T


def _sigmoid(x):
    return 1.0 / (1.0 + jnp.exp(-x))


def _silu(x):
    return x * _sigmoid(x)


def _log_sigmoid(x):
    return jnp.minimum(x, 0.0) - jnp.log(1.0 + jnp.exp(-jnp.abs(x)))


def _rms(x):
    return x * lax.rsqrt(jnp.mean(x * x, axis=-1, keepdims=True) + EPS)


def _rows_bcast(x, row, n):
    return jnp.broadcast_to(x[row:row + 1, :], (n, x.shape[1]))


def _cat(parts, axis):
    return parts[0] if len(parts) == 1 else jnp.concatenate(parts, axis=axis)


def _iota2(shape, dim):
    return lax.broadcasted_iota(jnp.int32, shape, dim)


def _blk(x, n):
    assert n & (n - 1) == 0
    return lax.shift_right_logical(x, jnp.int32(n.bit_length() - 1))


def _rem(x, n):
    assert n & (n - 1) == 0
    return x & jnp.int32(n - 1)


def _layer_spec(block, layer, tail):
    return pl.BlockSpec((None,) + block, lambda *g: (layer,) + tail(*g))


def _resident_spec(block, layer=None):
    zeros = (0,) * len(block)
    if layer is None:
        return pl.BlockSpec(block, lambda *g: zeros, pipeline_mode=pl.Buffered(1))
    return pl.BlockSpec((None,) + block, lambda *g: (layer,) + zeros, pipeline_mode=pl.Buffered(1))


class _RowGrid:
    def __init__(self, xp_shape, xs_shape, rows):
        bp, lp, d = xp_shape
        bs, ls, _ = xs_shape
        assert lp % rows == 0 and rows % ls == 0 and bs % (rows // ls) == 0
        self.rows, self.d = rows, d
        self.tiles_per_seq = lp // rows
        self.n_p = bp * self.tiles_per_seq
        self.seqs_per_tile = rows // ls
        self.n_s = bs // self.seqs_per_tile
        self.ls = ls
        self.grid = (self.n_p + self.n_s,)

    def _ip(self, i):
        return jnp.minimum(i, self.n_p - 1)

    def _is(self, i):
        return jnp.maximum(i - self.n_p, 0)

    def x_specs(self):
        tps = self.tiles_per_seq
        return (pl.BlockSpec((1, self.rows, self.d), lambda i: (self._ip(i) // tps, self._ip(i) % tps, 0)),
                pl.BlockSpec((self.seqs_per_tile, self.ls, self.d), lambda i: (self._is(i), 0, 0)))

    def mod_specs(self, layer, j):
        tps = self.tiles_per_seq
        return (pl.BlockSpec((None, None, 1, self.d), lambda i: (layer, self._ip(i) // tps, 0, j)),
                pl.BlockSpec((None, self.seqs_per_tile, self.d), lambda i: (layer, self._is(i), j)))

    def rows_specs(self, width):
        return (pl.BlockSpec((self.rows, width), lambda i: (self._ip(i), 0)),
                pl.BlockSpec((self.rows, width), lambda i: (self._is(i), 0)))

    def joint_spec(self, width):
        return pl.BlockSpec((self.rows, width), lambda i: (i, 0))

    def run(self, body, prompt_refs, sample_refs):
        i = pl.program_id(0)

        @pl.when(i < self.n_p)
        def _():
            body(*prompt_refs)

        @pl.when(i >= self.n_p)
        def _():
            body(*sample_refs)


def _rows2d(x_ref):
    bb, tt, d = x_ref.shape
    return x_ref[...].reshape(bb * tt, d)


def _per_row(mod_ref, rows):
    m = mod_ref[...]
    n = m.shape[0]
    if n == 1:
        return m
    return _cat([_rows_bcast(m, s, rows // n) for s in range(n)], 0)


def _modulated_norm(x, g, sc, sh):
    return (_rms(x) * g * (1.0 + sc) + sh).astype(BF16)


def _ada_kernel(c_ref, w_ref, b_ref, o_ref):
    a = _silu(c_ref[...]).astype(BF16)
    o_ref[...] = _dot(a, w_ref[...].astype(BF16)) + b_ref[...]


def _ada_call(c_all, w_ada, b_ada):
    depth, d, n6 = w_ada.shape
    bp = c_all.shape[0]
    tn = _largest_tile(n6, ADA_COLS_CAP)
    return pl.pallas_call(
        _ada_kernel,
        grid=(depth, n6 // tn),
        in_specs=[
            pl.BlockSpec((bp, d), lambda l, j: (0, 0)),
            pl.BlockSpec((None, d, tn), lambda l, j: (l, 0, j)),
            pl.BlockSpec((None, 1, tn), lambda l, j: (l, 0, j)),
        ],
        out_specs=pl.BlockSpec((None, bp, tn), lambda l, j: (l, 0, j)),
        out_shape=jax.ShapeDtypeStruct((depth, bp, n6), F32),
        compiler_params=_cparams("arbitrary", "arbitrary"),
        name="ada_mod",
    )(c_all, w_ada, b_ada.reshape(depth, 1, n6))


def _log_decay_terms(h, wl_ref, wa2_ref, ba_ref):
    alr = _dot_nt(h, wl_ref[...])
    gk = _dot(alr.astype(BF16), wa2_ref[...]) + ba_ref[...]
    return _split(_log_sigmoid(gk) * (1.0 / GATE_NORMALIZER), LOG_DECAY_TERMS)


def _gate_specs(rg, layer, d, dqk):
    const = lambda i: (0, 0)
    ins = [_layer_spec((LANE, d), layer, const), _layer_spec((LANE, dqk), layer, const),
           _layer_spec((1, dqk), layer, const)]
    outs = [rg.joint_spec(dqk)] * LOG_DECAY_TERMS
    return ins, outs


def _pre_kernel(xp_ref, xs_ref, scp_ref, scs_ref, shp_ref, shs_ref, g_ref, wl_ref, wa2_ref, ba_ref,
                h_ref, *lg_refs, rg):
    def body(x_ref, sc_ref, sh_ref):
        h = _modulated_norm(_rows2d(x_ref), g_ref[...], _per_row(sc_ref, rg.rows), _per_row(sh_ref, rg.rows))
        h_ref[...] = h
        for ref, term in zip(lg_refs, _log_decay_terms(h, wl_ref, wa2_ref, ba_ref)):
            ref[...] = term

    rg.run(body, (xp_ref, scp_ref, shp_ref), (xs_ref, scs_ref, shs_ref))


def _pre_call(xp, xs, mod_p, mod_s, layer, norm_g, w_alr_t, w_a2, b_a, rows=PROLOGUE_ROWS):
    rg = _RowGrid(xp.shape, xs.shape, rows)
    d = rg.d
    dqk = w_a2.shape[2]
    m_total = xp.shape[0] * xp.shape[1] + xs.shape[0] * xs.shape[1]
    gate_in, gate_out = _gate_specs(rg, layer, d, dqk)
    return pl.pallas_call(
        functools.partial(_pre_kernel, rg=rg),
        grid=rg.grid,
        in_specs=[*rg.x_specs(), *rg.mod_specs(layer, 1), *rg.mod_specs(layer, 0),
                  _layer_spec((1, d), layer, lambda i: (0, 0)), *gate_in],
        out_specs=[rg.joint_spec(d), *gate_out],
        out_shape=[jax.ShapeDtypeStruct((m_total, d), BF16)]
        + [jax.ShapeDtypeStruct((m_total, dqk), BF16)] * LOG_DECAY_TERMS,
        compiler_params=_cparams("arbitrary"),
        name="prologue",
    )(xp, xs, mod_p, mod_s, mod_p, mod_s, norm_g, w_alr_t, w_a2, b_a)


def _mm_in_kernel(a_ref, w_ref, p_ref, wb_ref, *, acts, scale, n_slabs):
    s, i = pl.program_id(0), pl.program_id(1)
    slab_rows = w_ref.shape[0]

    @pl.when((s < len(acts)) & (i < n_slabs))
    def _():
        wb_ref[s % 2, pl.ds(pl.multiple_of(i * slab_rows, slab_rows), slab_rows), :] = w_ref[...].astype(BF16)

    fns = {"id": lambda y: y, "scale": lambda y: y * scale, "silu": _silu, "sigmoid": _sigmoid}
    for name in sorted(set(acts)):
        cond = functools.reduce(jnp.logical_or, [s == jj + 1 for jj, a in enumerate(acts) if a == name])

        @pl.when(cond)
        def _(fn=fns[name]):
            p_ref[...] = fn(_dot_nt(a_ref[...], wb_ref[(s + 1) % 2])).astype(BF16)


def _mm_in_call(h, w_in_t, layer, dqk, d_model, d_pool):
    m, k = h.shape
    tm = _largest_row_tile(m, ROW_TILE_CAP)
    ni = m // tm
    starts = _p_columns(dqk, d_model, d_pool)
    n_main = starts[-1] + d_model
    tn = LANE
    for cand in range(LANE, IN_PROJ_COLS_CAP + 1, LANE):
        if all(s % cand == 0 for s in starts + (n_main,)):
            tn = cand
    nj = n_main // tn
    kinds = ("scale", "id", "id", "silu", "id", "sigmoid", "sigmoid")
    bounds = starts[1:] + (n_main,)
    acts = []
    for jj in range(nj):
        acts.append(kinds[next(i for i, e in enumerate(bounds) if jj * tn < e)])
    dk = dqk // N_HEADS
    n_slabs = max(n for n in range(1, ni + 1) if tn % n == 0 and (tn // n) % 16 == 0)
    slab_rows = tn // n_slabs
    slab_index = lambda s, i: (jnp.minimum(s * n_slabs + jnp.minimum(i, n_slabs - 1), nj * n_slabs - 1), 0)
    row_index = lambda s, i: jnp.where(s == 0, 0, i)
    return pl.pallas_call(
        functools.partial(_mm_in_kernel, acts=tuple(acts), scale=dk ** -0.5, n_slabs=n_slabs),
        grid=(nj + 1, ni),
        in_specs=[
            pl.BlockSpec((tm, k), lambda s, i: (row_index(s, i), 0)),
            _layer_spec((slab_rows, k), layer, slab_index),
        ],
        out_specs=pl.BlockSpec((tm, tn), lambda s, i: (row_index(s, i), jnp.maximum(s - 1, 0))),
        out_shape=jax.ShapeDtypeStruct((m, n_main), BF16),
        scratch_shapes=[pltpu.VMEM((2, tn, k), BF16)],
        compiler_params=_cparams("arbitrary", "arbitrary"),
        name="in_proj",
    )(h, w_in_t)


def _gla_consts(t, seq):
    nb = min(STABLE_BLOCK, seq)
    nblk = t // nb
    levels = []
    n = nb * 2
    while n <= seq:
        levels.append(n)
        n *= 2
    nkinds = 3 + len(levels)
    rows = -(-nkinds * nblk // 16) * 16
    row = _iota2((t, t), 0)
    col = _iota2((t, t), 1)
    rowv = _iota2((t, 1), 0)
    same0 = _blk(row, nb) == _blk(col, nb)
    r = _iota2((rows, t), 0)
    u = _iota2((rows, t), 1)
    kind = _blk(r, nblk)
    j = _rem(r, nblk)
    start = j * nb
    seq_start = start & jnp.int32(-seq)
    sel = (kind == 0) & (u >= seq_start) & (u < start)
    sel |= (kind == 1) & (u >= start) & (u < seq_start + seq)
    for li, n in enumerate(levels):
        mid = (start & jnp.int32(-n)) + n // 2
        between = ((u >= mid) & (u < start)) | ((u >= start) & (u < mid))
        sel |= (kind == 2 + li) & between
    sel |= (kind == nkinds - 1) & (_blk(u, seq) == j)
    return {
        "t": t, "seq": seq, "nb": nb, "nblk": nblk, "total_kind": nkinds - 1,
        "tri_local": (same0 & (col <= row)).astype(BF16),
        "sel": jnp.where(sel, 1.0, 0.0).astype(BF16),
        "mask0": same0 & (col <= row),
        "levels": [(_rem(rowv, n) >= n // 2, _blk(row, n) == _blk(col, n)) for n in levels],
    }


def _gla_operands(q, k, e0, g, c):
    nb, nblk = c["nb"], c["nblk"]

    def per_block(kind):
        return _cat([_rows_bcast(g, kind * nblk + j, nb) for j in range(nblk)], 0)

    qe0 = q * jnp.exp(e0)
    ke0 = k * jnp.exp(-e0)
    pairs = [(qe0.astype(BF16), ke0.astype(BF16))]
    for li, (second, _) in enumerate(c["levels"]):
        f = per_block(2 + li)
        pairs.append((jnp.where(second, qe0 * f, 0.0).astype(BF16), jnp.where(second, 0.0, ke0 * f).astype(BF16)))
    qe = qe0 if nb == c["seq"] else qe0 * per_block(0)
    kd = ke0 * per_block(1)
    return pairs, qe.astype(BF16), kd.astype(BF16)


def _gla_scores(pairs, c):
    att = jnp.where(c["mask0"], _dot_nt(*pairs[0]), 0.0)
    for (ql, kl), (_, same) in zip(pairs[1:], c["levels"]):
        att = att + jnp.where(same, _dot_nt(ql, kl), 0.0)
    return att


def _lane_tile(x, width):
    return _cat([x] * (width // x.shape[1]), 1)


def _pool_delta(u, u_parts, carry_parts, w, pos0, seq):
    t = u.shape[0]
    npiece = t // seq
    row = _iota2((t, t), 0)
    col = _iota2((t, t), 1)
    band = ((_blk(row, seq) == _blk(col, seq)) & (col <= row) & (col > row - w)).astype(BF16)
    rc = _iota2((t, npiece * CARRY_ROWS), 0)
    cc = _iota2((t, npiece * CARRY_ROWS), 1)
    cband = ((_blk(rc, seq) == _blk(cc, CARRY_ROWS))
             & (_rem(cc, CARRY_ROWS) > _rem(rc, seq) + CARRY_ROWS - w)).astype(BF16)
    s = _dot_parts(band, u_parts) + _dot_parts(cband, carry_parts)
    cnt = jnp.minimum(pos0 + 1, w).astype(F32)
    return s / cnt - u


def _mix_merge(o, silu_og, sig_ga, sig_gb, pooled, gn, ps):
    y_a = _rms(o) * gn * silu_og.astype(F32)
    return sig_ga.astype(F32) * y_a + sig_gb.astype(F32) * (pooled * ps)


def _p_columns(dqk, d_model, d_pool):
    q0, k0, v0 = 0, dqk, 2 * dqk
    og0 = v0 + d_model
    u0 = og0 + d_model
    ga0 = u0 + d_pool
    gb0 = ga0 + d_model
    return q0, k0, v0, og0, u0, ga0, gb0


def _mix_prompt_kernel(p_ref, *refs, dqk, d_model):
    lg_refs = refs[:LOG_DECAY_TERMS]
    gn_ref, wp_ref, ps_ref, m_ref, s_ref, pool_out_ref, car_ref = refs[LOG_DECAY_TERMS:]
    c = pl.program_id(1)
    t = p_ref.shape[0]
    groups, pin, dv = wp_ref.shape
    dk = dqk // N_HEADS
    q0, k0, v0, og0, u0, ga0, gb0 = _p_columns(dqk, d_model, groups * pin)

    @pl.when(c == 0)
    def _():
        s_ref[...] = jnp.zeros_like(s_ref)
        car_ref[...] = jnp.zeros_like(car_ref)

    consts = _gla_consts(t, t)
    loga = tuple(ref[...] for ref in lg_refs)
    e0_all = _dot_parts(consts["tri_local"], loga)
    g_all = jnp.exp(_dot_parts(consts["sel"], loga))
    tot = consts["total_kind"] * consts["nblk"]
    decay_all = _column_tile(g_all[tot:tot + 1, :])
    pos = c * t + _iota2((t, 1), 0)
    gn = gn_ref[...]

    heads = range(N_HEADS)
    deltas = []
    for h in heads:
        ub = p_ref[:, u0 + h * pin:u0 + (h + 1) * pin]
        deltas.append(_pool_delta(ub.astype(F32), (ub,), (car_ref[:, h * pin:(h + 1) * pin],), POOL_WINDOWS[h],
                                  pos, t).astype(BF16))
    ops = []
    for h in heads:
        q = p_ref[:, q0 + h * dk:q0 + (h + 1) * dk].astype(F32)
        k = p_ref[:, k0 + h * dk:k0 + (h + 1) * dk].astype(F32)
        ops.append(_gla_operands(q, k, e0_all[:, h * dk:(h + 1) * dk], g_all[:, h * dk:(h + 1) * dk], consts))
    pooled = [_dot(deltas[h], wp_ref[h]) for h in heads]
    atts = [_gla_scores(ops[h][0], consts).astype(BF16) for h in heads]
    outs = []
    for h in heads:
        _, qe, kd = ops[h]
        vb = p_ref[:, v0 + h * dv:v0 + (h + 1) * dv]
        s = s_ref[h]
        outs.append(_dot(atts[h], vb) + _dot(qe, s.astype(BF16)))
        decay = decay_all[h * dk:(h + 1) * dk]
        s_ref[h] = _lane_tile(decay, dv) * s + _dot_tn(kd, vb)

    for h in heads:
        merged = _mix_merge(outs[h], p_ref[:, og0 + h * dv:og0 + (h + 1) * dv],
                            p_ref[:, ga0 + h * dv:ga0 + (h + 1) * dv], p_ref[:, gb0 + h * dv:gb0 + (h + 1) * dv],
                            pooled[h], gn, ps_ref[:, h * dv:(h + 1) * dv])
        m_ref[:, h * dv:(h + 1) * dv] = merged.astype(BF16)

    u_tail = p_ref[t - CARRY_ROWS:, u0:u0 + groups * pin]
    car_ref[...] = u_tail

    @pl.when(c == pl.num_programs(1) - 1)
    def _():
        pool_out_ref[...] = u_tail.astype(F32)[CARRY_ROWS - POOL_BUF:]


def _mix_prompt_call(p, lg, batch, seq_len, layer, gla_norm_g, w_pool, pool_scale, chunk=MIX_CHUNK):
    dqk = lg[0].shape[1]
    dk = dqk // N_HEADS
    groups, pin, dv = w_pool.shape[1:]
    d_model = pool_scale.shape[2]
    n_p = p.shape[1]
    nc = seq_len // chunk
    const = lambda b, c: (0, 0)
    in_specs = [
        pl.BlockSpec((chunk, n_p), lambda b, c: (b * nc + c, 0)),
        *[pl.BlockSpec((chunk, dqk), lambda b, c: (b * nc + c, 0)) for _ in lg],
        _layer_spec((1, dv), layer, const),
        _layer_spec((groups, pin, dv), layer, lambda b, c: (0, 0, 0)),
        _layer_spec((1, d_model), layer, const),
    ]
    out_specs = [
        pl.BlockSpec((chunk, d_model), lambda b, c: (b * nc + c, 0)),
        pl.BlockSpec((None, N_HEADS, dk, dv), lambda b, c: (b, 0, 0, 0)),
        pl.BlockSpec((None, POOL_BUF, groups * pin), lambda b, c: (b, 0, 0)),
    ]
    out_shape = [
        jax.ShapeDtypeStruct((batch * seq_len, d_model), BF16),
        jax.ShapeDtypeStruct((batch, N_HEADS, dk, dv), F32),
        jax.ShapeDtypeStruct((batch, POOL_BUF, groups * pin), F32),
    ]
    return pl.pallas_call(
        functools.partial(_mix_prompt_kernel, dqk=dqk, d_model=d_model),
        grid=(batch, nc),
        in_specs=in_specs,
        out_specs=out_specs,
        out_shape=out_shape,
        scratch_shapes=[pltpu.VMEM((CARRY_ROWS, groups * pin), BF16)],
        compiler_params=_cparams("arbitrary", "arbitrary"),
        name="mix_prompt",
    )(p, *lg, gla_norm_g, w_pool, pool_scale)


def _mix_sample_kernel(p_ref, *refs, seq, dqk, d_model, n_alias):
    lg_refs = refs[:LOG_DECAY_TERMS]
    s0_ref, buf_ref, gn_ref, wp_ref, ps_ref = refs[LOG_DECAY_TERMS:LOG_DECAY_TERMS + 5]
    m_ref, s_out_ref, pool_out_ref, car_ref = refs[LOG_DECAY_TERMS + 5 + n_alias:]
    nseq = s0_ref.shape[0]
    t = nseq * seq
    groups, pin, dv = wp_ref.shape
    dk = dqk // N_HEADS
    q0, k0, v0, og0, u0, ga0, gb0 = _p_columns(dqk, d_model, groups * pin)

    car_ref[...] = jnp.zeros_like(car_ref)
    for j in range(nseq):
        car_ref[j * CARRY_ROWS + 1:(j + 1) * CARRY_ROWS, :] = buf_ref[j]

    consts = _gla_consts(t, seq)
    loga = tuple(ref[...] for ref in lg_refs)
    e0_all = _dot_parts(consts["tri_local"], loga)
    g_all = jnp.exp(_dot_parts(consts["sel"], loga))
    tot = consts["total_kind"] * consts["nblk"]
    decay_all = [_column_tile(g_all[tot + j:tot + j + 1, :]) for j in range(nseq)]
    pos = PAST_LEN + _rem(_iota2((t, 1), 0), seq)
    keep = POOL_BUF - seq
    gn = gn_ref[...]

    u_all = p_ref[:, u0:u0 + groups * pin].astype(F32)
    heads = range(N_HEADS)
    ops = []
    for h in heads:
        q = p_ref[:, q0 + h * dk:q0 + (h + 1) * dk].astype(F32)
        k = p_ref[:, k0 + h * dk:k0 + (h + 1) * dk].astype(F32)
        ops.append(_gla_operands(q, k, e0_all[:, h * dk:(h + 1) * dk], g_all[:, h * dk:(h + 1) * dk], consts))
    atts = [_gla_scores(ops[h][0], consts).astype(BF16) for h in heads]
    outs = []
    for h in heads:
        _, qeb, kdb = ops[h]
        vb = p_ref[:, v0 + h * dv:v0 + (h + 1) * dv]
        o_intra = _dot(atts[h], vb)
        o_parts = []
        for j in range(nseq):
            s0 = s0_ref[j, h]
            lo, hi = j * seq, (j + 1) * seq
            o_parts.append(o_intra[lo:hi] + _dot(qeb[lo:hi], s0.astype(BF16)))
            decay = decay_all[j][h * dk:(h + 1) * dk]
            s_out_ref[j, h] = _lane_tile(decay, dv) * s0 + _dot_tn(kdb[lo:hi], vb[lo:hi])
        outs.append(_cat(o_parts, 0))

    for h in heads:
        d = _pool_delta(u_all[:, h * pin:(h + 1) * pin], (p_ref[:, u0 + h * pin:u0 + (h + 1) * pin],),
                        _split(car_ref[:, h * pin:(h + 1) * pin], CARRY_TERMS), POOL_WINDOWS[h], pos, seq)
        merged = _mix_merge(outs[h], p_ref[:, og0 + h * dv:og0 + (h + 1) * dv],
                            p_ref[:, ga0 + h * dv:ga0 + (h + 1) * dv], p_ref[:, gb0 + h * dv:gb0 + (h + 1) * dv],
                            _dot(d.astype(BF16), wp_ref[h]), gn, ps_ref[:, h * dv:(h + 1) * dv])
        m_ref[:, h * dv:(h + 1) * dv] = merged.astype(BF16)

    for j in range(nseq):
        pool_out_ref[j, 0:keep, :] = buf_ref[j, seq:POOL_BUF, :]
        pool_out_ref[j, keep:POOL_BUF, :] = u_all[j * seq:(j + 1) * seq]


def _mix_sample_call(p, lg, row_off, seq, layer, state_gla, state_pool, gla_norm_g, w_pool,
                     pool_scale, prev_gla, prev_pool, nseq=SAMPLE_SEQS_PER_STEP):
    depth, bs, nh, dk, dv = state_gla.shape
    dqk = lg[0].shape[1]
    groups, pin, _ = w_pool.shape[1:]
    d_model = pool_scale.shape[2]
    n_p = p.shape[1]
    assert seq <= POOL_BUF and seq % 8 == 0
    t = nseq * seq
    off = row_off // t
    const = lambda i: (0, 0)
    in_specs = [
        pl.BlockSpec((t, n_p), lambda i: (off + i, 0)),
        *[pl.BlockSpec((t, dqk), lambda i: (off + i, 0)) for _ in lg],
        pl.BlockSpec((None, nseq, nh, dk, dv), lambda i: (layer, i, 0, 0, 0)),
        pl.BlockSpec((None, nseq, POOL_BUF, groups * pin), lambda i: (layer, i, 0, 0)),
        _layer_spec((1, dv), layer, const),
        _layer_spec((groups, pin, dv), layer, lambda i: (0, 0, 0)),
        _layer_spec((1, d_model), layer, const),
    ]
    args = [p, *lg, state_gla, state_pool, gla_norm_g, w_pool, pool_scale]
    aliases = {}
    n_alias = 0
    if prev_gla is not None:
        in_specs += [pl.BlockSpec(memory_space=pl.ANY), pl.BlockSpec(memory_space=pl.ANY)]
        aliases = {len(args): 1, len(args) + 1: 2}
        args += [prev_gla, prev_pool]
        n_alias = 2
    out_specs = [
        pl.BlockSpec((t, d_model), lambda i: (i, 0)),
        pl.BlockSpec((None, nseq, nh, dk, dv), lambda i: (layer, i, 0, 0, 0)),
        pl.BlockSpec((None, nseq, POOL_BUF, groups * pin), lambda i: (layer, i, 0, 0)),
    ]
    out_shape = [
        jax.ShapeDtypeStruct((bs * seq, d_model), BF16),
        jax.ShapeDtypeStruct(state_gla.shape, F32),
        jax.ShapeDtypeStruct(state_pool.shape, F32),
    ]
    return pl.pallas_call(
        functools.partial(_mix_sample_kernel, seq=seq, dqk=dqk, d_model=d_model, n_alias=n_alias),
        grid=(bs // nseq,),
        in_specs=in_specs,
        out_specs=out_specs,
        out_shape=out_shape,
        scratch_shapes=[pltpu.VMEM((nseq * CARRY_ROWS, groups * pin), F32)],
        input_output_aliases=aliases,
        compiler_params=_cparams("arbitrary"),
        name="mix_sample",
    )(*args)


def _mm_o_kernel(ap_ref, as_ref, xp_ref, xs_ref, g1p_ref, g1s_ref, scp_ref, scs_ref, shp_ref, shs_ref,
                 w_ref, ng_ref, x1p_ref, x1s_ref, h_ref, wb_ref, *, rg):
    @pl.when(pl.program_id(0) == 0)
    def _():
        wb_ref[...] = w_ref[...].astype(BF16)

    def body(a_ref, x_ref, g1_ref, sc_ref, sh_ref, x1_ref):
        rows = rg.rows
        x1 = _rows2d(x_ref) + _per_row(g1_ref, rows) * _dot(a_ref[...], wb_ref[...])
        x1_ref[...] = x1.reshape(x1_ref.shape)
        h_ref[...] = _modulated_norm(x1, ng_ref[...], _per_row(sc_ref, rows), _per_row(sh_ref, rows))

    rg.run(body, (ap_ref, xp_ref, g1p_ref, scp_ref, shp_ref, x1p_ref),
           (as_ref, xs_ref, g1s_ref, scs_ref, shs_ref, x1s_ref))


def _mm_o_call(merged_p, merged_s, xp, xs, w_o, mod_p, mod_s, layer, norm2_g, rows=EPILOGUE_ROWS):
    rg = _RowGrid(xp.shape, xs.shape, rows)
    d = rg.d
    m_total = xp.shape[0] * xp.shape[1] + xs.shape[0] * xs.shape[1]
    xsp = rg.x_specs()
    return pl.pallas_call(
        functools.partial(_mm_o_kernel, rg=rg),
        grid=rg.grid,
        in_specs=[*rg.rows_specs(d), *xsp, *rg.mod_specs(layer, 2), *rg.mod_specs(layer, 4),
                  *rg.mod_specs(layer, 3), _resident_spec((d, d), layer),
                  _layer_spec((1, d), layer, lambda i: (0, 0))],
        out_specs=[*xsp, rg.joint_spec(d)],
        out_shape=[jax.ShapeDtypeStruct(xp.shape, F32), jax.ShapeDtypeStruct(xs.shape, F32),
                   jax.ShapeDtypeStruct((m_total, d), BF16)],
        scratch_shapes=[pltpu.VMEM((d, d), BF16)],
        compiler_params=_cparams("arbitrary"),
        name="out_proj",
    )(merged_p, merged_s, xp, xs, mod_p, mod_s, mod_p, mod_s, mod_p, mod_s, w_o, norm2_g)


def _mm_gu_kernel(a_ref, wg_ref, wu_ref, wd_ref, o_ref, wdb_ref, wgb_ref, wub_ref, *, n_slabs):
    @pl.when(pl.program_id(1) == 0)
    def _():
        wgb_ref[...] = wg_ref[...].astype(BF16)
        wub_ref[...] = wu_ref[...].astype(BF16)

    @pl.when(pl.program_id(0) * pl.num_programs(1) + pl.program_id(1) < n_slabs)
    def _():
        wdb_ref[...] = wd_ref[...].astype(BF16)

    a = a_ref[...]
    o_ref[...] = (_silu(_dot(a, wgb_ref[...])) * _dot(a, wub_ref[...])).astype(BF16)


def _mm_gu_call(h, w_gu, w_down, layer):
    m, k = h.shape
    tm = _largest_row_tile(m, ROW_TILE_CAP)
    dff = w_gu.shape[2] // 2
    d = w_down.shape[2]
    tn = _largest_tile(dff, GATE_UP_COLS_CAP)
    nj, ni = dff // tn, m // tm
    slab = next(r for r in range(16, dff + 1, 16) if dff % r == 0 and dff // r <= nj * ni)
    n_slabs = dff // slab
    slab_index = lambda j, i: (jnp.minimum(j * ni + i, n_slabs - 1), 0)
    return pl.pallas_call(
        functools.partial(_mm_gu_kernel, n_slabs=n_slabs),
        grid=(nj, ni),
        in_specs=[
            pl.BlockSpec((tm, k), lambda j, i: (i, 0)),
            _layer_spec((k, tn), layer, lambda j, i: (0, j)),
            _layer_spec((k, tn), layer, lambda j, i: (0, nj + j)),
            _layer_spec((slab, d), layer, slab_index),
        ],
        out_specs=[pl.BlockSpec((tm, tn), lambda j, i: (i, j)), pl.BlockSpec((slab, d), slab_index)],
        out_shape=[jax.ShapeDtypeStruct((m, dff), BF16), jax.ShapeDtypeStruct((dff, d), BF16)],
        scratch_shapes=[pltpu.VMEM((k, tn), BF16), pltpu.VMEM((k, tn), BF16)],
        compiler_params=_cparams("arbitrary", "arbitrary"),
        name="gate_up",
    )(h, w_gu, w_gu, w_down)


def _mm_down_kernel(a_ref, xp_ref, xs_ref, g2p_ref, g2s_ref, w_ref, ng_ref, *rest, rg, final):
    if final:
        yp_ref, ys_ref = rest

        def body(x_ref, g2_ref, y_ref):
            x2 = _rows2d(x_ref) + _per_row(g2_ref, rg.rows) * _dot(a_ref[...], w_ref[...])
            y_ref[...] = (_rms(x2) * ng_ref[...]).reshape(y_ref.shape)

        rg.run(body, (xp_ref, g2p_ref, yp_ref), (xs_ref, g2s_ref, ys_ref))
    else:
        scp_ref, scs_ref, shp_ref, shs_ref, wl_ref, wa2_ref, ba_ref, x2p_ref, x2s_ref, h_ref, *lg_refs = rest

        def body(x_ref, g2_ref, sc_ref, sh_ref, x2_ref):
            rows = rg.rows
            x2 = _rows2d(x_ref) + _per_row(g2_ref, rows) * _dot(a_ref[...], w_ref[...])
            x2_ref[...] = x2.reshape(x2_ref.shape)
            h = _modulated_norm(x2, ng_ref[...], _per_row(sc_ref, rows), _per_row(sh_ref, rows))
            h_ref[...] = h
            for ref, term in zip(lg_refs, _log_decay_terms(h, wl_ref, wa2_ref, ba_ref)):
                ref[...] = term

        rg.run(body, (xp_ref, g2p_ref, scp_ref, shp_ref, x2p_ref), (xs_ref, g2s_ref, scs_ref, shs_ref, x2s_ref))


def _mm_down_call(act, xp, xs, w_down, mod_p, mod_s, layer, norm_g, norm_layer, final, w_alr_t, w_a2, b_a,
                  rows=EPILOGUE_ROWS):
    rg = _RowGrid(xp.shape, xs.shape, rows)
    d = rg.d
    dff = w_down.shape[0]
    m_total = act.shape[0]
    xsp = rg.x_specs()
    in_specs = [rg.joint_spec(dff), *xsp, *rg.mod_specs(layer, 5), _resident_spec((dff, d)),
                _layer_spec((1, d), norm_layer, lambda i: (0, 0))]
    args = [act, xp, xs, mod_p, mod_s, w_down, norm_g]
    x_shapes = [jax.ShapeDtypeStruct(xp.shape, F32), jax.ShapeDtypeStruct(xs.shape, F32)]
    if final:
        out_specs, out_shape = [*xsp], x_shapes
    else:
        dqk = w_a2.shape[2]
        gate_in, gate_out = _gate_specs(rg, layer + 1, d, dqk)
        in_specs += [*rg.mod_specs(layer + 1, 1), *rg.mod_specs(layer + 1, 0), *gate_in]
        args += [mod_p, mod_s, mod_p, mod_s, w_alr_t, w_a2, b_a]
        out_specs = [*xsp, rg.joint_spec(d), *gate_out]
        out_shape = x_shapes + [jax.ShapeDtypeStruct((m_total, d), BF16)] \
            + [jax.ShapeDtypeStruct((m_total, dqk), BF16)] * LOG_DECAY_TERMS
    return pl.pallas_call(
        functools.partial(_mm_down_kernel, rg=rg, final=final),
        grid=rg.grid,
        in_specs=in_specs,
        out_specs=out_specs,
        out_shape=out_shape,
        compiler_params=_cparams("arbitrary"),
        name="down_proj",
    )(*args)


def kernel(x_prompt, x_sample, state_gla, state_pool, c_prompt, c_sample, w_ada, b_ada, norm1_g, w_in, w_a2, b_a,
           gla_norm_g, w_pool, pool_scale, w_o, norm2_g, w_gu, w_down, final_norm_g):
    depth, d, n6 = w_ada.shape
    bp, lp, _ = x_prompt.shape
    bs, ls, _ = x_sample.shape
    mp = bp * lp
    rank, dqk = w_a2.shape[1:]
    n_main = w_in.shape[2] - rank

    c_all = jnp.concatenate([c_sample, c_prompt], axis=0)
    c_all = jnp.pad(c_all, ((0, (-c_all.shape[0]) % 8), (0, 0)))
    mod_s = _ada_call(c_all, w_ada, b_ada)
    mod_p = mod_s[:, bs:bs + bp].reshape(depth, bp, 1, n6)

    w_in_t = jnp.swapaxes(w_in, 1, 2)
    w_alr_b = jnp.pad(w_in_t[:, n_main:, :], ((0, 0), (0, LANE - rank), (0, 0))).astype(BF16)
    w_a2_b = jnp.pad(w_a2, ((0, 0), (0, LANE - rank), (0, 0))).astype(BF16)
    w_pool_b = w_pool.astype(BF16)
    n1 = norm1_g.reshape(depth, 1, d)
    n2 = norm2_g.reshape(depth, 1, d)
    nf = final_norm_g.reshape(1, 1, d)
    ba3 = b_a.reshape(depth, 1, dqk)
    gn3 = gla_norm_g.reshape(depth, 1, -1)
    ps3 = pool_scale.reshape(depth, 1, d)

    xp, xs = x_prompt, x_sample
    h, *lg = _pre_call(xp, xs, mod_p, mod_s, 0, n1, w_alr_b, w_a2_b, ba3)

    gla_p, pool_p = [], []
    gla_s = pool_s = None
    for l in range(depth):
        p = _mm_in_call(h, w_in_t, l, dqk, d, w_pool.shape[1] * w_pool.shape[2])
        merged_p, s_p, b_p = _mix_prompt_call(p, lg, bp, lp, l, gn3, w_pool_b, ps3)
        merged_s, gla_s, pool_s = _mix_sample_call(p, lg, mp, ls, l, state_gla, state_pool, gn3, w_pool_b, ps3,
                                                   gla_s, pool_s)
        gla_p.append(s_p)
        pool_p.append(b_p)

        xp, xs, h2 = _mm_o_call(merged_p, merged_s, xp, xs, w_o, mod_p, mod_s, l, n2)
        act, w_down_b = _mm_gu_call(h2, w_gu, w_down, l)
        if l + 1 < depth:
            xp, xs, h, *lg = _mm_down_call(act, xp, xs, w_down_b, mod_p, mod_s, l, n1, l + 1, False, w_alr_b,
                                           w_a2_b, ba3)
        else:
            yp, ys = _mm_down_call(act, xp, xs, w_down_b, mod_p, mod_s, l, nf, 0, True, None, None, None)
    return yp, ys, jnp.stack(gla_p), jnp.stack(pool_p), gla_s, pool_s
```

```python
import functools

import jax
import jax.numpy as jnp
from jax import lax
from jax.experimental import pallas as pl
from jax.experimental.pallas import tpu as pltpu

F32 = jnp.float32
BF16 = jnp.bfloat16

N_HEADS = 4
POOL_WINDOWS = (2, 4, 8, 16)
POOL_BUF = 15
CARRY_ROWS = POOL_BUF + 1
GATE_NORMALIZER = 16.0
PAST_LEN = 16384
EPS = 1e-6
STABLE_BLOCK = 16
LOG_DECAY_TERMS = 2
CARRY_TERMS = 3
LANE = 128
MXU_DIM = 256
VMEM_LIMIT = 56 * 1024 * 1024

ROW_TILE_CAP = 1024
IN_PROJ_COLS_CAP = 1024
GATE_UP_COLS_CAP = 512
ADA_COLS_CAP = 2048
PROLOGUE_ROWS = 512
EPILOGUE_ROWS = 256
MIX_CHUNK = 256
SAMPLE_SEQS_PER_STEP = 4


def _cparams(*sem):
    return pltpu.CompilerParams(dimension_semantics=sem, vmem_limit_bytes=VMEM_LIMIT)


def _largest_tile(n, cap):
    best = None
    for t in range(LANE, cap + 1, LANE):
        if n % t == 0:
            best = t
    assert best is not None, (n, cap)
    return best


def _largest_row_tile(m, cap):
    best = None
    for t in range(MXU_DIM, cap + 1, MXU_DIM):
        if m % t == 0:
            best = t
    assert best is not None, (m, cap)
    return best


def _dot(a, b):
    return jnp.dot(a, b, preferred_element_type=F32)


def _dot_nt(a, b):
    return lax.dot_general(a, b, (((1,), (1,)), ((), ())), preferred_element_type=F32)


def _dot_tn(a, b):
    return lax.dot_general(a, b, (((0,), (0,)), ((), ())), preferred_element_type=F32)


def _split(x, terms):
    parts = []
    for i in range(terms):
        p = x.astype(BF16)
        parts.append(p)
        if i + 1 < terms:
            x = x - p.astype(F32)
    return tuple(parts)


def _dot_parts(m, parts):
    out = _dot(m, parts[0])
    for p in parts[1:]:
        out = out + _dot(m, p)
    return out


def _column_tile(row):
    return jnp.broadcast_to(row, (LANE, row.shape[1])).T


def _sigmoid(x):
    return 1.0 / (1.0 + jnp.exp(-x))


def _silu(x):
    return x * _sigmoid(x)


def _log_sigmoid(x):
    return jnp.minimum(x, 0.0) - jnp.log(1.0 + jnp.exp(-jnp.abs(x)))


def _rms(x):
    return x * lax.rsqrt(jnp.mean(x * x, axis=-1, keepdims=True) + EPS)


def _rows_bcast(x, row, n):
    return jnp.broadcast_to(x[row:row + 1, :], (n, x.shape[1]))


def _cat(parts, axis):
    return parts[0] if len(parts) == 1 else jnp.concatenate(parts, axis=axis)


def _iota2(shape, dim):
    return lax.broadcasted_iota(jnp.int32, shape, dim)


def _blk(x, n):
    assert n & (n - 1) == 0
    return lax.shift_right_logical(x, jnp.int32(n.bit_length() - 1))


def _rem(x, n):
    assert n & (n - 1) == 0
    return x & jnp.int32(n - 1)


def _layer_spec(block, layer, tail):
    return pl.BlockSpec((None,) + block, lambda *g: (layer,) + tail(*g))


def _resident_spec(block, layer=None):
    zeros = (0,) * len(block)
    if layer is None:
        return pl.BlockSpec(block, lambda *g: zeros, pipeline_mode=pl.Buffered(1))
    return pl.BlockSpec((None,) + block, lambda *g: (layer,) + zeros, pipeline_mode=pl.Buffered(1))


class _RowGrid:
    def __init__(self, xp_shape, xs_shape, rows):
        bp, lp, d = xp_shape
        bs, ls, _ = xs_shape
        assert lp % rows == 0 and rows % ls == 0 and bs % (rows // ls) == 0
        self.rows, self.d = rows, d
        self.tiles_per_seq = lp // rows
        self.n_p = bp * self.tiles_per_seq
        self.seqs_per_tile = rows // ls
        self.n_s = bs // self.seqs_per_tile
        self.ls = ls
        self.grid = (self.n_p + self.n_s,)

    def _ip(self, i):
        return jnp.minimum(i, self.n_p - 1)

    def _is(self, i):
        return jnp.maximum(i - self.n_p, 0)

    def x_specs(self):
        tps = self.tiles_per_seq
        return (pl.BlockSpec((1, self.rows, self.d), lambda i: (self._ip(i) // tps, self._ip(i) % tps, 0)),
                pl.BlockSpec((self.seqs_per_tile, self.ls, self.d), lambda i: (self._is(i), 0, 0)))

    def mod_specs(self, layer, j):
        tps = self.tiles_per_seq
        return (pl.BlockSpec((None, None, 1, self.d), lambda i: (layer, self._ip(i) // tps, 0, j)),
                pl.BlockSpec((None, self.seqs_per_tile, self.d), lambda i: (layer, self._is(i), j)))

    def rows_specs(self, width):
        return (pl.BlockSpec((self.rows, width), lambda i: (self._ip(i), 0)),
                pl.BlockSpec((self.rows, width), lambda i: (self._is(i), 0)))

    def joint_spec(self, width):
        return pl.BlockSpec((self.rows, width), lambda i: (i, 0))

    def run(self, body, prompt_refs, sample_refs):
        i = pl.program_id(0)

        @pl.when(i < self.n_p)
        def _():
            body(*prompt_refs)

        @pl.when(i >= self.n_p)
        def _():
            body(*sample_refs)


def _rows2d(x_ref):
    bb, tt, d = x_ref.shape
    return x_ref[...].reshape(bb * tt, d)


def _per_row(mod_ref, rows):
    m = mod_ref[...]
    n = m.shape[0]
    if n == 1:
        return m
    return _cat([_rows_bcast(m, s, rows // n) for s in range(n)], 0)


def _modulated_norm(x, g, sc, sh):
    return (_rms(x) * g * (1.0 + sc) + sh).astype(BF16)


def _ada_kernel(c_ref, w_ref, b_ref, o_ref):
    a = _silu(c_ref[...]).astype(BF16)
    o_ref[...] = _dot(a, w_ref[...].astype(BF16)) + b_ref[...]


def _ada_call(c_all, w_ada, b_ada):
    depth, d, n6 = w_ada.shape
    bp = c_all.shape[0]
    tn = _largest_tile(n6, ADA_COLS_CAP)
    return pl.pallas_call(
        _ada_kernel,
        grid=(depth, n6 // tn),
        in_specs=[
            pl.BlockSpec((bp, d), lambda l, j: (0, 0)),
            pl.BlockSpec((None, d, tn), lambda l, j: (l, 0, j)),
            pl.BlockSpec((None, 1, tn), lambda l, j: (l, 0, j)),
        ],
        out_specs=pl.BlockSpec((None, bp, tn), lambda l, j: (l, 0, j)),
        out_shape=jax.ShapeDtypeStruct((depth, bp, n6), F32),
        compiler_params=_cparams("arbitrary", "arbitrary"),
        name="ada_mod",
    )(c_all, w_ada, b_ada.reshape(depth, 1, n6))


def _log_decay_terms(h, wl_ref, wa2_ref, ba_ref):
    alr = _dot_nt(h, wl_ref[...])
    gk = _dot(alr.astype(BF16), wa2_ref[...]) + ba_ref[...]
    return _split(_log_sigmoid(gk) * (1.0 / GATE_NORMALIZER), LOG_DECAY_TERMS)


def _gate_specs(rg, layer, d, dqk):
    const = lambda i: (0, 0)
    ins = [_layer_spec((LANE, d), layer, const), _layer_spec((LANE, dqk), layer, const),
           _layer_spec((1, dqk), layer, const)]
    outs = [rg.joint_spec(dqk)] * LOG_DECAY_TERMS
    return ins, outs


def _pre_kernel(xp_ref, xs_ref, scp_ref, scs_ref, shp_ref, shs_ref, g_ref, wl_ref, wa2_ref, ba_ref,
                h_ref, *lg_refs, rg):
    def body(x_ref, sc_ref, sh_ref):
        h = _modulated_norm(_rows2d(x_ref), g_ref[...], _per_row(sc_ref, rg.rows), _per_row(sh_ref, rg.rows))
        h_ref[...] = h
        for ref, term in zip(lg_refs, _log_decay_terms(h, wl_ref, wa2_ref, ba_ref)):
            ref[...] = term

    rg.run(body, (xp_ref, scp_ref, shp_ref), (xs_ref, scs_ref, shs_ref))


def _pre_call(xp, xs, mod_p, mod_s, layer, norm_g, w_alr_t, w_a2, b_a, rows=PROLOGUE_ROWS):
    rg = _RowGrid(xp.shape, xs.shape, rows)
    d = rg.d
    dqk = w_a2.shape[2]
    m_total = xp.shape[0] * xp.shape[1] + xs.shape[0] * xs.shape[1]
    gate_in, gate_out = _gate_specs(rg, layer, d, dqk)
    return pl.pallas_call(
        functools.partial(_pre_kernel, rg=rg),
        grid=rg.grid,
        in_specs=[*rg.x_specs(), *rg.mod_specs(layer, 1), *rg.mod_specs(layer, 0),
                  _layer_spec((1, d), layer, lambda i: (0, 0)), *gate_in],
        out_specs=[rg.joint_spec(d), *gate_out],
        out_shape=[jax.ShapeDtypeStruct((m_total, d), BF16)]
        + [jax.ShapeDtypeStruct((m_total, dqk), BF16)] * LOG_DECAY_TERMS,
        compiler_params=_cparams("arbitrary"),
        name="prologue",
    )(xp, xs, mod_p, mod_s, mod_p, mod_s, norm_g, w_alr_t, w_a2, b_a)


def _mm_in_kernel(a_ref, w_ref, p_ref, wb_ref, *, acts, scale, n_slabs):
    s, i = pl.program_id(0), pl.program_id(1)
    slab_rows = w_ref.shape[0]

    @pl.when((s < len(acts)) & (i < n_slabs))
    def _():
        wb_ref[s % 2, pl.ds(pl.multiple_of(i * slab_rows, slab_rows), slab_rows), :] = w_ref[...].astype(BF16)

    fns = {"id": lambda y: y, "scale": lambda y: y * scale, "silu": _silu, "sigmoid": _sigmoid}
    for name in sorted(set(acts)):
        cond = functools.reduce(jnp.logical_or, [s == jj + 1 for jj, a in enumerate(acts) if a == name])

        @pl.when(cond)
        def _(fn=fns[name]):
            p_ref[...] = fn(_dot_nt(a_ref[...], wb_ref[(s + 1) % 2])).astype(BF16)


def _mm_in_call(h, w_in_t, layer, dqk, d_model, d_pool):
    m, k = h.shape
    tm = _largest_row_tile(m, ROW_TILE_CAP)
    ni = m // tm
    starts = _p_columns(dqk, d_model, d_pool)
    n_main = starts[-1] + d_model
    tn = LANE
    for cand in range(LANE, IN_PROJ_COLS_CAP + 1, LANE):
        if all(s % cand == 0 for s in starts + (n_main,)):
            tn = cand
    nj = n_main // tn
    kinds = ("scale", "id", "id", "silu", "id", "sigmoid", "sigmoid")
    bounds = starts[1:] + (n_main,)
    acts = []
    for jj in range(nj):
        acts.append(kinds[next(i for i, e in enumerate(bounds) if jj * tn < e)])
    dk = dqk // N_HEADS
    n_slabs = max(n for n in range(1, ni + 1) if tn % n == 0 and (tn // n) % 16 == 0)
    slab_rows = tn // n_slabs
    slab_index = lambda s, i: (jnp.minimum(s * n_slabs + jnp.minimum(i, n_slabs - 1), nj * n_slabs - 1), 0)
    row_index = lambda s, i: jnp.where(s == 0, 0, i)
    return pl.pallas_call(
        functools.partial(_mm_in_kernel, acts=tuple(acts), scale=dk ** -0.5, n_slabs=n_slabs),
        grid=(nj + 1, ni),
        in_specs=[
            pl.BlockSpec((tm, k), lambda s, i: (row_index(s, i), 0)),
            _layer_spec((slab_rows, k), layer, slab_index),
        ],
        out_specs=pl.BlockSpec((tm, tn), lambda s, i: (row_index(s, i), jnp.maximum(s - 1, 0))),
        out_shape=jax.ShapeDtypeStruct((m, n_main), BF16),
        scratch_shapes=[pltpu.VMEM((2, tn, k), BF16)],
        compiler_params=_cparams("arbitrary", "arbitrary"),
        name="in_proj",
    )(h, w_in_t)


def _gla_consts(t, seq):
    nb = min(STABLE_BLOCK, seq)
    nblk = t // nb
    levels = []
    n = nb * 2
    while n <= seq:
        levels.append(n)
        n *= 2
    nkinds = 3 + len(levels)
    rows = -(-nkinds * nblk // 16) * 16
    row = _iota2((t, t), 0)
    col = _iota2((t, t), 1)
    rowv = _iota2((t, 1), 0)
    same0 = _blk(row, nb) == _blk(col, nb)
    r = _iota2((rows, t), 0)
    u = _iota2((rows, t), 1)
    kind = _blk(r, nblk)
    j = _rem(r, nblk)
    start = j * nb
    seq_start = start & jnp.int32(-seq)
    sel = (kind == 0) & (u >= seq_start) & (u < start)
    sel |= (kind == 1) & (u >= start) & (u < seq_start + seq)
    for li, n in enumerate(levels):
        mid = (start & jnp.int32(-n)) + n // 2
        between = ((u >= mid) & (u < start)) | ((u >= start) & (u < mid))
        sel |= (kind == 2 + li) & between
    sel |= (kind == nkinds - 1) & (_blk(u, seq) == j)
    return {
        "t": t, "seq": seq, "nb": nb, "nblk": nblk, "total_kind": nkinds - 1,
        "tri_local": (same0 & (col <= row)).astype(BF16),
        "sel": jnp.where(sel, 1.0, 0.0).astype(BF16),
        "mask0": same0 & (col <= row),
        "levels": [(_rem(rowv, n) >= n // 2, _blk(row, n) == _blk(col, n)) for n in levels],
    }


def _gla_operands(q, k, e0, g, c):
    nb, nblk = c["nb"], c["nblk"]

    def per_block(kind):
        return _cat([_rows_bcast(g, kind * nblk + j, nb) for j in range(nblk)], 0)

    qe0 = q * jnp.exp(e0)
    ke0 = k * jnp.exp(-e0)
    pairs = [(qe0.astype(BF16), ke0.astype(BF16))]
    for li, (second, _) in enumerate(c["levels"]):
        f = per_block(2 + li)
        pairs.append((jnp.where(second, qe0 * f, 0.0).astype(BF16), jnp.where(second, 0.0, ke0 * f).astype(BF16)))
    qe = qe0 if nb == c["seq"] else qe0 * per_block(0)
    kd = ke0 * per_block(1)
    return pairs, qe.astype(BF16), kd.astype(BF16)


def _gla_scores(pairs, c):
    att = jnp.where(c["mask0"], _dot_nt(*pairs[0]), 0.0)
    for (ql, kl), (_, same) in zip(pairs[1:], c["levels"]):
        att = att + jnp.where(same, _dot_nt(ql, kl), 0.0)
    return att


def _lane_tile(x, width):
    return _cat([x] * (width // x.shape[1]), 1)


def _pool_delta(u, u_parts, carry_parts, w, pos0, seq):
    t = u.shape[0]
    npiece = t // seq
    row = _iota2((t, t), 0)
    col = _iota2((t, t), 1)
    band = ((_blk(row, seq) == _blk(col, seq)) & (col <= row) & (col > row - w)).astype(BF16)
    rc = _iota2((t, npiece * CARRY_ROWS), 0)
    cc = _iota2((t, npiece * CARRY_ROWS), 1)
    cband = ((_blk(rc, seq) == _blk(cc, CARRY_ROWS))
             & (_rem(cc, CARRY_ROWS) > _rem(rc, seq) + CARRY_ROWS - w)).astype(BF16)
    s = _dot_parts(band, u_parts) + _dot_parts(cband, carry_parts)
    cnt = jnp.minimum(pos0 + 1, w).astype(F32)
    return s / cnt - u


def _mix_merge(o, silu_og, sig_ga, sig_gb, pooled, gn, ps):
    y_a = _rms(o) * gn * silu_og.astype(F32)
    return sig_ga.astype(F32) * y_a + sig_gb.astype(F32) * (pooled * ps)


def _p_columns(dqk, d_model, d_pool):
    q0, k0, v0 = 0, dqk, 2 * dqk
    og0 = v0 + d_model
    u0 = og0 + d_model
    ga0 = u0 + d_pool
    gb0 = ga0 + d_model
    return q0, k0, v0, og0, u0, ga0, gb0


def _mix_prompt_kernel(p_ref, *refs, dqk, d_model):
    lg_refs = refs[:LOG_DECAY_TERMS]
    gn_ref, wp_ref, ps_ref, m_ref, s_ref, pool_out_ref, car_ref = refs[LOG_DECAY_TERMS:]
    c = pl.program_id(1)
    t = p_ref.shape[0]
    groups, pin, dv = wp_ref.shape
    dk = dqk // N_HEADS
    q0, k0, v0, og0, u0, ga0, gb0 = _p_columns(dqk, d_model, groups * pin)

    @pl.when(c == 0)
    def _():
        s_ref[...] = jnp.zeros_like(s_ref)
        car_ref[...] = jnp.zeros_like(car_ref)

    consts = _gla_consts(t, t)
    loga = tuple(ref[...] for ref in lg_refs)
    e0_all = _dot_parts(consts["tri_local"], loga)
    g_all = jnp.exp(_dot_parts(consts["sel"], loga))
    tot = consts["total_kind"] * consts["nblk"]
    decay_all = _column_tile(g_all[tot:tot + 1, :])
    pos = c * t + _iota2((t, 1), 0)
    gn = gn_ref[...]

    heads = range(N_HEADS)
    deltas = []
    for h in heads:
        ub = p_ref[:, u0 + h * pin:u0 + (h + 1) * pin]
        deltas.append(_pool_delta(ub.astype(F32), (ub,), (car_ref[:, h * pin:(h + 1) * pin],), POOL_WINDOWS[h],
                                  pos, t).astype(BF16))
    ops = []
    for h in heads:
        q = p_ref[:, q0 + h * dk:q0 + (h + 1) * dk].astype(F32)
        k = p_ref[:, k0 + h * dk:k0 + (h + 1) * dk].astype(F32)
        ops.append(_gla_operands(q, k, e0_all[:, h * dk:(h + 1) * dk], g_all[:, h * dk:(h + 1) * dk], consts))
    pooled = [_dot(deltas[h], wp_ref[h]) for h in heads]
    atts = [_gla_scores(ops[h][0], consts).astype(BF16) for h in heads]
    outs = []
    for h in heads:
        _, qe, kd = ops[h]
        vb = p_ref[:, v0 + h * dv:v0 + (h + 1) * dv]
        s = s_ref[h]
        outs.append(_dot(atts[h], vb) + _dot(qe, s.astype(BF16)))
        decay = decay_all[h * dk:(h + 1) * dk]
        s_ref[h] = _lane_tile(decay, dv) * s + _dot_tn(kd, vb)

    for h in heads:
        merged = _mix_merge(outs[h], p_ref[:, og0 + h * dv:og0 + (h + 1) * dv],
                            p_ref[:, ga0 + h * dv:ga0 + (h + 1) * dv], p_ref[:, gb0 + h * dv:gb0 + (h + 1) * dv],
                            pooled[h], gn, ps_ref[:, h * dv:(h + 1) * dv])
        m_ref[:, h * dv:(h + 1) * dv] = merged.astype(BF16)

    u_tail = p_ref[t - CARRY_ROWS:, u0:u0 + groups * pin]
    car_ref[...] = u_tail

    @pl.when(c == pl.num_programs(1) - 1)
    def _():
        pool_out_ref[...] = u_tail.astype(F32)[CARRY_ROWS - POOL_BUF:]


def _mix_prompt_call(p, lg, batch, seq_len, layer, gla_norm_g, w_pool, pool_scale, chunk=MIX_CHUNK):
    dqk = lg[0].shape[1]
    dk = dqk // N_HEADS
    groups, pin, dv = w_pool.shape[1:]
    d_model = pool_scale.shape[2]
    n_p = p.shape[1]
    nc = seq_len // chunk
    const = lambda b, c: (0, 0)
    in_specs = [
        pl.BlockSpec((chunk, n_p), lambda b, c: (b * nc + c, 0)),
        *[pl.BlockSpec((chunk, dqk), lambda b, c: (b * nc + c, 0)) for _ in lg],
        _layer_spec((1, dv), layer, const),
        _layer_spec((groups, pin, dv), layer, lambda b, c: (0, 0, 0)),
        _layer_spec((1, d_model), layer, const),
    ]
    out_specs = [
        pl.BlockSpec((chunk, d_model), lambda b, c: (b * nc + c, 0)),
        pl.BlockSpec((None, N_HEADS, dk, dv), lambda b, c: (b, 0, 0, 0)),
        pl.BlockSpec((None, POOL_BUF, groups * pin), lambda b, c: (b, 0, 0)),
    ]
    out_shape = [
        jax.ShapeDtypeStruct((batch * seq_len, d_model), BF16),
        jax.ShapeDtypeStruct((batch, N_HEADS, dk, dv), F32),
        jax.ShapeDtypeStruct((batch, POOL_BUF, groups * pin), F32),
    ]
    return pl.pallas_call(
        functools.partial(_mix_prompt_kernel, dqk=dqk, d_model=d_model),
        grid=(batch, nc),
        in_specs=in_specs,
        out_specs=out_specs,
        out_shape=out_shape,
        scratch_shapes=[pltpu.VMEM((CARRY_ROWS, groups * pin), BF16)],
        compiler_params=_cparams("arbitrary", "arbitrary"),
        name="mix_prompt",
    )(p, *lg, gla_norm_g, w_pool, pool_scale)


def _mix_sample_kernel(p_ref, *refs, seq, dqk, d_model, n_alias):
    lg_refs = refs[:LOG_DECAY_TERMS]
    s0_ref, buf_ref, gn_ref, wp_ref, ps_ref = refs[LOG_DECAY_TERMS:LOG_DECAY_TERMS + 5]
    m_ref, s_out_ref, pool_out_ref, car_ref = refs[LOG_DECAY_TERMS + 5 + n_alias:]
    nseq = s0_ref.shape[0]
    t = nseq * seq
    groups, pin, dv = wp_ref.shape
    dk = dqk // N_HEADS
    q0, k0, v0, og0, u0, ga0, gb0 = _p_columns(dqk, d_model, groups * pin)

    car_ref[...] = jnp.zeros_like(car_ref)
    for j in range(nseq):
        car_ref[j * CARRY_ROWS + 1:(j + 1) * CARRY_ROWS, :] = buf_ref[j]

    consts = _gla_consts(t, seq)
    loga = tuple(ref[...] for ref in lg_refs)
    e0_all = _dot_parts(consts["tri_local"], loga)
    g_all = jnp.exp(_dot_parts(consts["sel"], loga))
    tot = consts["total_kind"] * consts["nblk"]
    decay_all = [_column_tile(g_all[tot + j:tot + j + 1, :]) for j in range(nseq)]
    pos = PAST_LEN + _rem(_iota2((t, 1), 0), seq)
    keep = POOL_BUF - seq
    gn = gn_ref[...]

    u_all = p_ref[:, u0:u0 + groups * pin].astype(F32)
    heads = range(N_HEADS)
    ops = []
    for h in heads:
        q = p_ref[:, q0 + h * dk:q0 + (h + 1) * dk].astype(F32)
        k = p_ref[:, k0 + h * dk:k0 + (h + 1) * dk].astype(F32)
        ops.append(_gla_operands(q, k, e0_all[:, h * dk:(h + 1) * dk], g_all[:, h * dk:(h + 1) * dk], consts))
    atts = [_gla_scores(ops[h][0], consts).astype(BF16) for h in heads]
    outs = []
    for h in heads:
        _, qeb, kdb = ops[h]
        vb = p_ref[:, v0 + h * dv:v0 + (h + 1) * dv]
        o_intra = _dot(atts[h], vb)
        o_parts = []
        for j in range(nseq):
            s0 = s0_ref[j, h]
            lo, hi = j * seq, (j + 1) * seq
            o_parts.append(o_intra[lo:hi] + _dot(qeb[lo:hi], s0.astype(BF16)))
            decay = decay_all[j][h * dk:(h + 1) * dk]
            s_out_ref[j, h] = _lane_tile(decay, dv) * s0 + _dot_tn(kdb[lo:hi], vb[lo:hi])
        outs.append(_cat(o_parts, 0))

    for h in heads:
        d = _pool_delta(u_all[:, h * pin:(h + 1) * pin], (p_ref[:, u0 + h * pin:u0 + (h + 1) * pin],),
                        _split(car_ref[:, h * pin:(h + 1) * pin], CARRY_TERMS), POOL_WINDOWS[h], pos, seq)
        merged = _mix_merge(outs[h], p_ref[:, og0 + h * dv:og0 + (h + 1) * dv],
                            p_ref[:, ga0 + h * dv:ga0 + (h + 1) * dv], p_ref[:, gb0 + h * dv:gb0 + (h + 1) * dv],
                            _dot(d.astype(BF16), wp_ref[h]), gn, ps_ref[:, h * dv:(h + 1) * dv])
        m_ref[:, h * dv:(h + 1) * dv] = merged.astype(BF16)

    for j in range(nseq):
        pool_out_ref[j, 0:keep, :] = buf_ref[j, seq:POOL_BUF, :]
        pool_out_ref[j, keep:POOL_BUF, :] = u_all[j * seq:(j + 1) * seq]


def _mix_sample_call(p, lg, row_off, seq, layer, state_gla, state_pool, gla_norm_g, w_pool,
                     pool_scale, prev_gla, prev_pool, nseq=SAMPLE_SEQS_PER_STEP):
    depth, bs, nh, dk, dv = state_gla.shape
    dqk = lg[0].shape[1]
    groups, pin, _ = w_pool.shape[1:]
    d_model = pool_scale.shape[2]
    n_p = p.shape[1]
    assert seq <= POOL_BUF and seq % 8 == 0
    t = nseq * seq
    off = row_off // t
    const = lambda i: (0, 0)
    in_specs = [
        pl.BlockSpec((t, n_p), lambda i: (off + i, 0)),
        *[pl.BlockSpec((t, dqk), lambda i: (off + i, 0)) for _ in lg],
        pl.BlockSpec((None, nseq, nh, dk, dv), lambda i: (layer, i, 0, 0, 0)),
        pl.BlockSpec((None, nseq, POOL_BUF, groups * pin), lambda i: (layer, i, 0, 0)),
        _layer_spec((1, dv), layer, const),
        _layer_spec((groups, pin, dv), layer, lambda i: (0, 0, 0)),
        _layer_spec((1, d_model), layer, const),
    ]
    args = [p, *lg, state_gla, state_pool, gla_norm_g, w_pool, pool_scale]
    aliases = {}
    n_alias = 0
    if prev_gla is not None:
        in_specs += [pl.BlockSpec(memory_space=pl.ANY), pl.BlockSpec(memory_space=pl.ANY)]
        aliases = {len(args): 1, len(args) + 1: 2}
        args += [prev_gla, prev_pool]
        n_alias = 2
    out_specs = [
        pl.BlockSpec((t, d_model), lambda i: (i, 0)),
        pl.BlockSpec((None, nseq, nh, dk, dv), lambda i: (layer, i, 0, 0, 0)),
        pl.BlockSpec((None, nseq, POOL_BUF, groups * pin), lambda i: (layer, i, 0, 0)),
    ]
    out_shape = [
        jax.ShapeDtypeStruct((bs * seq, d_model), BF16),
        jax.ShapeDtypeStruct(state_gla.shape, F32),
        jax.ShapeDtypeStruct(state_pool.shape, F32),
    ]
    return pl.pallas_call(
        functools.partial(_mix_sample_kernel, seq=seq, dqk=dqk, d_model=d_model, n_alias=n_alias),
        grid=(bs // nseq,),
        in_specs=in_specs,
        out_specs=out_specs,
        out_shape=out_shape,
        scratch_shapes=[pltpu.VMEM((nseq * CARRY_ROWS, groups * pin), F32)],
        input_output_aliases=aliases,
        compiler_params=_cparams("arbitrary"),
        name="mix_sample",
    )(*args)


def _mm_o_kernel(ap_ref, as_ref, xp_ref, xs_ref, g1p_ref, g1s_ref, scp_ref, scs_ref, shp_ref, shs_ref,
                 w_ref, ng_ref, x1p_ref, x1s_ref, h_ref, wb_ref, *, rg):
    @pl.when(pl.program_id(0) == 0)
    def _():
        wb_ref[...] = w_ref[...].astype(BF16)

    def body(a_ref, x_ref, g1_ref, sc_ref, sh_ref, x1_ref):
        rows = rg.rows
        x1 = _rows2d(x_ref) + _per_row(g1_ref, rows) * _dot(a_ref[...], wb_ref[...])
        x1_ref[...] = x1.reshape(x1_ref.shape)
        h_ref[...] = _modulated_norm(x1, ng_ref[...], _per_row(sc_ref, rows), _per_row(sh_ref, rows))

    rg.run(body, (ap_ref, xp_ref, g1p_ref, scp_ref, shp_ref, x1p_ref),
           (as_ref, xs_ref, g1s_ref, scs_ref, shs_ref, x1s_ref))


def _mm_o_call(merged_p, merged_s, xp, xs, w_o, mod_p, mod_s, layer, norm2_g, rows=EPILOGUE_ROWS):
    rg = _RowGrid(xp.shape, xs.shape, rows)
    d = rg.d
    m_total = xp.shape[0] * xp.shape[1] + xs.shape[0] * xs.shape[1]
    xsp = rg.x_specs()
    return pl.pallas_call(
        functools.partial(_mm_o_kernel, rg=rg),
        grid=rg.grid,
        in_specs=[*rg.rows_specs(d), *xsp, *rg.mod_specs(layer, 2), *rg.mod_specs(layer, 4),
                  *rg.mod_specs(layer, 3), _resident_spec((d, d), layer),
                  _layer_spec((1, d), layer, lambda i: (0, 0))],
        out_specs=[*xsp, rg.joint_spec(d)],
        out_shape=[jax.ShapeDtypeStruct(xp.shape, F32), jax.ShapeDtypeStruct(xs.shape, F32),
                   jax.ShapeDtypeStruct((m_total, d), BF16)],
        scratch_shapes=[pltpu.VMEM((d, d), BF16)],
        compiler_params=_cparams("arbitrary"),
        name="out_proj",
    )(merged_p, merged_s, xp, xs, mod_p, mod_s, mod_p, mod_s, mod_p, mod_s, w_o, norm2_g)


def _mm_gu_kernel(a_ref, wg_ref, wu_ref, wd_ref, o_ref, wdb_ref, wgb_ref, wub_ref, *, n_slabs):
    @pl.when(pl.program_id(1) == 0)
    def _():
        wgb_ref[...] = wg_ref[...].astype(BF16)
        wub_ref[...] = wu_ref[...].astype(BF16)

    @pl.when(pl.program_id(0) * pl.num_programs(1) + pl.program_id(1) < n_slabs)
    def _():
        wdb_ref[...] = wd_ref[...].astype(BF16)

    a = a_ref[...]
    o_ref[...] = (_silu(_dot(a, wgb_ref[...])) * _dot(a, wub_ref[...])).astype(BF16)


def _mm_gu_call(h, w_gu, w_down, layer):
    m, k = h.shape
    tm = _largest_row_tile(m, ROW_TILE_CAP)
    dff = w_gu.shape[2] // 2
    d = w_down.shape[2]
    tn = _largest_tile(dff, GATE_UP_COLS_CAP)
    nj, ni = dff // tn, m // tm
    slab = next(r for r in range(16, dff + 1, 16) if dff % r == 0 and dff // r <= nj * ni)
    n_slabs = dff // slab
    slab_index = lambda j, i: (jnp.minimum(j * ni + i, n_slabs - 1), 0)
    return pl.pallas_call(
        functools.partial(_mm_gu_kernel, n_slabs=n_slabs),
        grid=(nj, ni),
        in_specs=[
            pl.BlockSpec((tm, k), lambda j, i: (i, 0)),
            _layer_spec((k, tn), layer, lambda j, i: (0, j)),
            _layer_spec((k, tn), layer, lambda j, i: (0, nj + j)),
            _layer_spec((slab, d), layer, slab_index),
        ],
        out_specs=[pl.BlockSpec((tm, tn), lambda j, i: (i, j)), pl.BlockSpec((slab, d), slab_index)],
        out_shape=[jax.ShapeDtypeStruct((m, dff), BF16), jax.ShapeDtypeStruct((dff, d), BF16)],
        scratch_shapes=[pltpu.VMEM((k, tn), BF16), pltpu.VMEM((k, tn), BF16)],
        compiler_params=_cparams("arbitrary", "arbitrary"),
        name="gate_up",
    )(h, w_gu, w_gu, w_down)


def _mm_down_kernel(a_ref, xp_ref, xs_ref, g2p_ref, g2s_ref, w_ref, ng_ref, *rest, rg, final):
    if final:
        yp_ref, ys_ref = rest

        def body(x_ref, g2_ref, y_ref):
            x2 = _rows2d(x_ref) + _per_row(g2_ref, rg.rows) * _dot(a_ref[...], w_ref[...])
            y_ref[...] = (_rms(x2) * ng_ref[...]).reshape(y_ref.shape)

        rg.run(body, (xp_ref, g2p_ref, yp_ref), (xs_ref, g2s_ref, ys_ref))
    else:
        scp_ref, scs_ref, shp_ref, shs_ref, wl_ref, wa2_ref, ba_ref, x2p_ref, x2s_ref, h_ref, *lg_refs = rest

        def body(x_ref, g2_ref, sc_ref, sh_ref, x2_ref):
            rows = rg.rows
            x2 = _rows2d(x_ref) + _per_row(g2_ref, rows) * _dot(a_ref[...], w_ref[...])
            x2_ref[...] = x2.reshape(x2_ref.shape)
            h = _modulated_norm(x2, ng_ref[...], _per_row(sc_ref, rows), _per_row(sh_ref, rows))
            h_ref[...] = h
            for ref, term in zip(lg_refs, _log_decay_terms(h, wl_ref, wa2_ref, ba_ref)):
                ref[...] = term

        rg.run(body, (xp_ref, g2p_ref, scp_ref, shp_ref, x2p_ref), (xs_ref, g2s_ref, scs_ref, shs_ref, x2s_ref))


def _mm_down_call(act, xp, xs, w_down, mod_p, mod_s, layer, norm_g, norm_layer, final, w_alr_t, w_a2, b_a,
                  rows=EPILOGUE_ROWS):
    rg = _RowGrid(xp.shape, xs.shape, rows)
    d = rg.d
    dff = w_down.shape[0]
    m_total = act.shape[0]
    xsp = rg.x_specs()
    in_specs = [rg.joint_spec(dff), *xsp, *rg.mod_specs(layer, 5), _resident_spec((dff, d)),
                _layer_spec((1, d), norm_layer, lambda i: (0, 0))]
    args = [act, xp, xs, mod_p, mod_s, w_down, norm_g]
    x_shapes = [jax.ShapeDtypeStruct(xp.shape, F32), jax.ShapeDtypeStruct(xs.shape, F32)]
    if final:
        out_specs, out_shape = [*xsp], x_shapes
    else:
        dqk = w_a2.shape[2]
        gate_in, gate_out = _gate_specs(rg, layer + 1, d, dqk)
        in_specs += [*rg.mod_specs(layer + 1, 1), *rg.mod_specs(layer + 1, 0), *gate_in]
        args += [mod_p, mod_s, mod_p, mod_s, w_alr_t, w_a2, b_a]
        out_specs = [*xsp, rg.joint_spec(d), *gate_out]
        out_shape = x_shapes + [jax.ShapeDtypeStruct((m_total, d), BF16)] \
            + [jax.ShapeDtypeStruct((m_total, dqk), BF16)] * LOG_DECAY_TERMS
    return pl.pallas_call(
        functools.partial(_mm_down_kernel, rg=rg, final=final),
        grid=rg.grid,
        in_specs=in_specs,
        out_specs=out_specs,
        out_shape=out_shape,
        compiler_params=_cparams("arbitrary"),
        name="down_proj",
    )(*args)


def kernel(x_prompt, x_sample, state_gla, state_pool, c_prompt, c_sample, w_ada, b_ada, norm1_g, w_in, w_a2, b_a,
           gla_norm_g, w_pool, pool_scale, w_o, norm2_g, w_gu, w_down, final_norm_g):
    depth, d, n6 = w_ada.shape
    bp, lp, _ = x_prompt.shape
    bs, ls, _ = x_sample.shape
    mp = bp * lp
    rank, dqk = w_a2.shape[1:]
    n_main = w_in.shape[2] - rank

    c_all = jnp.concatenate([c_sample, c_prompt], axis=0)
    c_all = jnp.pad(c_all, ((0, (-c_all.shape[0]) % 8), (0, 0)))
    mod_s = _ada_call(c_all, w_ada, b_ada)
    mod_p = mod_s[:, bs:bs + bp].reshape(depth, bp, 1, n6)

    w_in_t = jnp.swapaxes(w_in, 1, 2)
    w_alr_b = jnp.pad(w_in_t[:, n_main:, :], ((0, 0), (0, LANE - rank), (0, 0))).astype(BF16)
    w_a2_b = jnp.pad(w_a2, ((0, 0), (0, LANE - rank), (0, 0))).astype(BF16)
    w_pool_b = w_pool.astype(BF16)
    n1 = norm1_g.reshape(depth, 1, d)
    n2 = norm2_g.reshape(depth, 1, d)
    nf = final_norm_g.reshape(1, 1, d)
    ba3 = b_a.reshape(depth, 1, dqk)
    gn3 = gla_norm_g.reshape(depth, 1, -1)
    ps3 = pool_scale.reshape(depth, 1, d)

    xp, xs = x_prompt, x_sample
    h, *lg = _pre_call(xp, xs, mod_p, mod_s, 0, n1, w_alr_b, w_a2_b, ba3)

    gla_p, pool_p = [], []
    gla_s = pool_s = None
    for l in range(depth):
        p = _mm_in_call(h, w_in_t, l, dqk, d, w_pool.shape[1] * w_pool.shape[2])
        merged_p, s_p, b_p = _mix_prompt_call(p, lg, bp, lp, l, gn3, w_pool_b, ps3)
        merged_s, gla_s, pool_s = _mix_sample_call(p, lg, mp, ls, l, state_gla, state_pool, gn3, w_pool_b, ps3,
                                                   gla_s, pool_s)
        gla_p.append(s_p)
        pool_p.append(b_p)

        xp, xs, h2 = _mm_o_call(merged_p, merged_s, xp, xs, w_o, mod_p, mod_s, l, n2)
        act, w_down_b = _mm_gu_call(h2, w_gu, w_down, l)
        if l + 1 < depth:
            xp, xs, h, *lg = _mm_down_call(act, xp, xs, w_down_b, mod_p, mod_s, l, n1, l + 1, False, w_alr_b,
                                           w_a2_b, ba3)
        else:
            yp, ys = _mm_down_call(act, xp, xs, w_down_b, mod_p, mod_s, l, nf, 0, True, None, None, None)
    return yp, ys, jnp.stack(gla_p), jnp.stack(pool_p), gla_s, pool_s
```

```python
import functools

import jax
import jax.numpy as jnp
from jax import lax
from jax.experimental import pallas as pl
from jax.experimental.pallas import tpu as pltpu

F32 = jnp.float32
BF16 = jnp.bfloat16

N_HEADS = 4
POOL_WINDOWS = (2, 4, 8, 16)
POOL_BUF = 15
CARRY_ROWS = POOL_BUF + 1
GATE_NORMALIZER = 16.0
PAST_LEN = 16384
EPS = 1e-6
STABLE_BLOCK = 16
LOG_DECAY_TERMS = 2
CARRY_TERMS = 3
LANE = 128
MXU_DIM = 256
VMEM_LIMIT = 56 * 1024 * 1024

ROW_TILE_CAP = 1024
IN_PROJ_COLS_CAP = 1024
GATE_UP_COLS_CAP = 512
ADA_COLS_CAP = 2048
PROLOGUE_ROWS = 512
EPILOGUE_ROWS = 256
MIX_CHUNK = 256
SAMPLE_SEQS_PER_STEP = 4
STATE_BUFFERS = 3


def _cparams(*sem):
    return pltpu.CompilerParams(dimension_semantics=sem, vmem_limit_bytes=VMEM_LIMIT)


def _largest_tile(n, cap):
    best = None
    for t in range(LANE, cap + 1, LANE):
        if n % t == 0:
            best = t
    assert best is not None, (n, cap)
    return best


def _largest_row_tile(m, cap):
    best = None
    for t in range(MXU_DIM, cap + 1, MXU_DIM):
        if m % t == 0:
            best = t
    assert best is not None, (m, cap)
    return best


def _dot(a, b):
    return jnp.dot(a, b, preferred_element_type=F32)


def _dot_nt(a, b):
    return lax.dot_general(a, b, (((1,), (1,)), ((), ())), preferred_element_type=F32)


def _dot_tn(a, b):
    return lax.dot_general(a, b, (((0,), (0,)), ((), ())), preferred_element_type=F32)


def _split(x, terms):
    parts = []
    for i in range(terms):
        p = x.astype(BF16)
        parts.append(p)
        if i + 1 < terms:
            x = x - p.astype(F32)
    return tuple(parts)


def _dot_parts(m, parts):
    out = _dot(m, parts[0])
    for p in parts[1:]:
        out = out + _dot(m, p)
    return out


def _column_tile(row):
    return jnp.broadcast_to(row, (LANE, row.shape[1])).T


def _sigmoid(x):
    return 1.0 / (1.0 + jnp.exp(-x))


def _silu(x):
    return x * _sigmoid(x)


def _log_sigmoid(x):
    return jnp.minimum(x, 0.0) - jnp.log(1.0 + jnp.exp(-jnp.abs(x)))


def _rms(x):
    return x * lax.rsqrt(jnp.mean(x * x, axis=-1, keepdims=True) + EPS)


def _rows_bcast(x, row, n):
    return jnp.broadcast_to(x[row:row + 1, :], (n, x.shape[1]))


def _cat(parts, axis):
    return parts[0] if len(parts) == 1 else jnp.concatenate(parts, axis=axis)


def _iota2(shape, dim):
    return lax.broadcasted_iota(jnp.int32, shape, dim)


def _blk(x, n):
    assert n & (n - 1) == 0
    return lax.shift_right_logical(x, jnp.int32(n.bit_length() - 1))


def _rem(x, n):
    assert n & (n - 1) == 0
    return x & jnp.int32(n - 1)


def _layer_spec(block, layer, tail):
    return pl.BlockSpec((None,) + block, lambda *g: (layer,) + tail(*g))


def _resident_spec(block, layer=None):
    zeros = (0,) * len(block)
    if layer is None:
        return pl.BlockSpec(block, lambda *g: zeros, pipeline_mode=pl.Buffered(1))
    return pl.BlockSpec((None,) + block, lambda *g: (layer,) + zeros, pipeline_mode=pl.Buffered(1))


class _RowGrid:
    def __init__(self, xp_shape, xs_shape, rows):
        bp, lp, d = xp_shape
        bs, ls, _ = xs_shape
        assert lp % rows == 0 and rows % ls == 0 and bs % (rows // ls) == 0
        self.rows, self.d = rows, d
        self.tiles_per_seq = lp // rows
        self.n_p = bp * self.tiles_per_seq
        self.seqs_per_tile = rows // ls
        self.n_s = bs // self.seqs_per_tile
        self.ls = ls
        self.grid = (self.n_p + self.n_s,)

    def _ip(self, i):
        return jnp.minimum(i, self.n_p - 1)

    def _is(self, i):
        return jnp.maximum(i - self.n_p, 0)

    def x_specs(self):
        tps = self.tiles_per_seq
        return (pl.BlockSpec((1, self.rows, self.d), lambda i: (self._ip(i) // tps, self._ip(i) % tps, 0)),
                pl.BlockSpec((self.seqs_per_tile, self.ls, self.d), lambda i: (self._is(i), 0, 0)))

    def mod_specs(self, layer, j):
        tps = self.tiles_per_seq
        return (pl.BlockSpec((None, None, 1, self.d), lambda i: (layer, self._ip(i) // tps, 0, j)),
                pl.BlockSpec((None, self.seqs_per_tile, self.d), lambda i: (layer, self._is(i), j)))

    def rows_specs(self, width):
        return (pl.BlockSpec((self.rows, width), lambda i: (self._ip(i), 0)),
                pl.BlockSpec((self.rows, width), lambda i: (self._is(i), 0)))

    def joint_spec(self, width):
        return pl.BlockSpec((self.rows, width), lambda i: (i, 0))

    def run(self, body, prompt_refs, sample_refs):
        i = pl.program_id(0)

        @pl.when(i < self.n_p)
        def _():
            body(*prompt_refs)

        @pl.when(i >= self.n_p)
        def _():
            body(*sample_refs)


def _rows2d(x_ref):
    bb, tt, d = x_ref.shape
    return x_ref[...].reshape(bb * tt, d)


def _per_row(mod_ref, rows):
    m = mod_ref[...]
    n = m.shape[0]
    if n == 1:
        return m
    return _cat([_rows_bcast(m, s, rows // n) for s in range(n)], 0)


def _modulated_norm(x, g, sc, sh):
    return (_rms(x) * g * (1.0 + sc) + sh).astype(BF16)


def _ada_kernel(c_ref, w_ref, b_ref, o_ref):
    a = _silu(c_ref[...]).astype(BF16)
    o_ref[...] = _dot(a, w_ref[...].astype(BF16)) + b_ref[...]


def _ada_call(c_all, w_ada, b_ada):
    depth, d, n6 = w_ada.shape
    bp = c_all.shape[0]
    tn = _largest_tile(n6, ADA_COLS_CAP)
    return pl.pallas_call(
        _ada_kernel,
        grid=(depth, n6 // tn),
        in_specs=[
            pl.BlockSpec((bp, d), lambda l, j: (0, 0)),
            pl.BlockSpec((None, d, tn), lambda l, j: (l, 0, j)),
            pl.BlockSpec((None, 1, tn), lambda l, j: (l, 0, j)),
        ],
        out_specs=pl.BlockSpec((None, bp, tn), lambda l, j: (l, 0, j)),
        out_shape=jax.ShapeDtypeStruct((depth, bp, n6), F32),
        compiler_params=_cparams("arbitrary", "arbitrary"),
        name="ada_mod",
    )(c_all, w_ada, b_ada.reshape(depth, 1, n6))


def _log_decay_terms(h, wl_ref, wa2_ref, ba_ref):
    alr = _dot_nt(h, wl_ref[...])
    gk = _dot(alr.astype(BF16), wa2_ref[...]) + ba_ref[...]
    return _split(_log_sigmoid(gk) * (1.0 / GATE_NORMALIZER), LOG_DECAY_TERMS)


def _gate_specs(rg, layer, d, dqk):
    const = lambda i: (0, 0)
    ins = [_layer_spec((LANE, d), layer, const), _layer_spec((LANE, dqk), layer, const),
           _layer_spec((1, dqk), layer, const)]
    outs = [rg.joint_spec(dqk)] * LOG_DECAY_TERMS
    return ins, outs


def _pre_kernel(xp_ref, xs_ref, scp_ref, scs_ref, shp_ref, shs_ref, g_ref, wl_ref, wa2_ref, ba_ref,
                h_ref, *lg_refs, rg):
    def body(x_ref, sc_ref, sh_ref):
        h = _modulated_norm(_rows2d(x_ref), g_ref[...], _per_row(sc_ref, rg.rows), _per_row(sh_ref, rg.rows))
        h_ref[...] = h
        for ref, term in zip(lg_refs, _log_decay_terms(h, wl_ref, wa2_ref, ba_ref)):
            ref[...] = term

    rg.run(body, (xp_ref, scp_ref, shp_ref), (xs_ref, scs_ref, shs_ref))


def _pre_call(xp, xs, mod_p, mod_s, layer, norm_g, w_alr_t, w_a2, b_a, rows=PROLOGUE_ROWS):
    rg = _RowGrid(xp.shape, xs.shape, rows)
    d = rg.d
    dqk = w_a2.shape[2]
    m_total = xp.shape[0] * xp.shape[1] + xs.shape[0] * xs.shape[1]
    gate_in, gate_out = _gate_specs(rg, layer, d, dqk)
    return pl.pallas_call(
        functools.partial(_pre_kernel, rg=rg),
        grid=rg.grid,
        in_specs=[*rg.x_specs(), *rg.mod_specs(layer, 1), *rg.mod_specs(layer, 0),
                  _layer_spec((1, d), layer, lambda i: (0, 0)), *gate_in],
        out_specs=[rg.joint_spec(d), *gate_out],
        out_shape=[jax.ShapeDtypeStruct((m_total, d), BF16)]
        + [jax.ShapeDtypeStruct((m_total, dqk), BF16)] * LOG_DECAY_TERMS,
        compiler_params=_cparams("arbitrary"),
        name="prologue",
    )(xp, xs, mod_p, mod_s, mod_p, mod_s, norm_g, w_alr_t, w_a2, b_a)


def _mm_in_kernel(a_ref, w_ref, p_ref, wb_ref, *, acts, scale, n_slabs):
    s, i = pl.program_id(0), pl.program_id(1)
    slab_rows = w_ref.shape[0]

    @pl.when((s < len(acts)) & (i < n_slabs))
    def _():
        wb_ref[s % 2, pl.ds(pl.multiple_of(i * slab_rows, slab_rows), slab_rows), :] = w_ref[...].astype(BF16)

    fns = {"id": lambda y: y, "scale": lambda y: y * scale, "silu": _silu, "sigmoid": _sigmoid}
    for name in sorted(set(acts)):
        cond = functools.reduce(jnp.logical_or, [s == jj + 1 for jj, a in enumerate(acts) if a == name])

        @pl.when(cond)
        def _(fn=fns[name]):
            p_ref[...] = fn(_dot_nt(a_ref[...], wb_ref[(s + 1) % 2])).astype(BF16)


def _mm_in_call(h, w_in_t, layer, dqk, d_model, d_pool):
    m, k = h.shape
    tm = _largest_row_tile(m, ROW_TILE_CAP)
    ni = m // tm
    starts = _p_columns(dqk, d_model, d_pool)
    n_main = starts[-1] + d_model
    tn = LANE
    for cand in range(LANE, IN_PROJ_COLS_CAP + 1, LANE):
        if all(s % cand == 0 for s in starts + (n_main,)):
            tn = cand
    nj = n_main // tn
    kinds = ("scale", "id", "id", "silu", "id", "sigmoid", "sigmoid")
    bounds = starts[1:] + (n_main,)
    acts = []
    for jj in range(nj):
        acts.append(kinds[next(i for i, e in enumerate(bounds) if jj * tn < e)])
    dk = dqk // N_HEADS
    n_slabs = max(n for n in range(1, ni + 1) if tn % n == 0 and (tn // n) % 16 == 0)
    slab_rows = tn // n_slabs
    slab_index = lambda s, i: (jnp.minimum(s * n_slabs + jnp.minimum(i, n_slabs - 1), nj * n_slabs - 1), 0)
    row_index = lambda s, i: jnp.where(s == 0, 0, i)
    return pl.pallas_call(
        functools.partial(_mm_in_kernel, acts=tuple(acts), scale=dk ** -0.5, n_slabs=n_slabs),
        grid=(nj + 1, ni),
        in_specs=[
            pl.BlockSpec((tm, k), lambda s, i: (row_index(s, i), 0)),
            _layer_spec((slab_rows, k), layer, slab_index),
        ],
        out_specs=pl.BlockSpec((tm, tn), lambda s, i: (row_index(s, i), jnp.maximum(s - 1, 0))),
        out_shape=jax.ShapeDtypeStruct((m, n_main), BF16),
        scratch_shapes=[pltpu.VMEM((2, tn, k), BF16)],
        compiler_params=_cparams("arbitrary", "arbitrary"),
        name="in_proj",
    )(h, w_in_t)


def _gla_consts(t, seq):
    nb = min(STABLE_BLOCK, seq)
    nblk = t // nb
    levels = []
    n = nb * 2
    while n <= seq:
        levels.append(n)
        n *= 2
    nkinds = 3 + len(levels)
    rows = -(-nkinds * nblk // 16) * 16
    row = _iota2((t, t), 0)
    col = _iota2((t, t), 1)
    rowv = _iota2((t, 1), 0)
    same0 = _blk(row, nb) == _blk(col, nb)
    r = _iota2((rows, t), 0)
    u = _iota2((rows, t), 1)
    kind = _blk(r, nblk)
    j = _rem(r, nblk)
    start = j * nb
    seq_start = start & jnp.int32(-seq)
    sel = (kind == 0) & (u >= seq_start) & (u < start)
    sel |= (kind == 1) & (u >= start) & (u < seq_start + seq)
    for li, n in enumerate(levels):
        mid = (start & jnp.int32(-n)) + n // 2
        between = ((u >= mid) & (u < start)) | ((u >= start) & (u < mid))
        sel |= (kind == 2 + li) & between
    sel |= (kind == nkinds - 1) & (_blk(u, seq) == j)
    return {
        "t": t, "seq": seq, "nb": nb, "nblk": nblk, "total_kind": nkinds - 1,
        "tri_local": (same0 & (col <= row)).astype(BF16),
        "sel": jnp.where(sel, 1.0, 0.0).astype(BF16),
        "mask0": same0 & (col <= row),
        "levels": [(_rem(rowv, n) >= n // 2, _blk(row, n) == _blk(col, n)) for n in levels],
    }


def _gla_operands(q, k, e0, g, c):
    nb, nblk = c["nb"], c["nblk"]

    def per_block(kind):
        return _cat([_rows_bcast(g, kind * nblk + j, nb) for j in range(nblk)], 0)

    qe0 = q * jnp.exp(e0)
    ke0 = k * jnp.exp(-e0)
    pairs = [(qe0.astype(BF16), ke0.astype(BF16))]
    for li, (second, _) in enumerate(c["levels"]):
        f = per_block(2 + li)
        pairs.append((jnp.where(second, qe0 * f, 0.0).astype(BF16), jnp.where(second, 0.0, ke0 * f).astype(BF16)))
    qe = qe0 if nb == c["seq"] else qe0 * per_block(0)
    kd = ke0 * per_block(1)
    return pairs, qe.astype(BF16), kd.astype(BF16)


def _gla_scores(pairs, c):
    att = jnp.where(c["mask0"], _dot_nt(*pairs[0]), 0.0)
    for (ql, kl), (_, same) in zip(pairs[1:], c["levels"]):
        att = att + jnp.where(same, _dot_nt(ql, kl), 0.0)
    return att


def _lane_tile(x, width):
    return _cat([x] * (width // x.shape[1]), 1)


def _pool_delta(u, u_parts, carry_parts, w, pos0, seq):
    t = u.shape[0]
    npiece = t // seq
    row = _iota2((t, t), 0)
    col = _iota2((t, t), 1)
    band = ((_blk(row, seq) == _blk(col, seq)) & (col <= row) & (col > row - w)).astype(BF16)
    rc = _iota2((t, npiece * CARRY_ROWS), 0)
    cc = _iota2((t, npiece * CARRY_ROWS), 1)
    cband = ((_blk(rc, seq) == _blk(cc, CARRY_ROWS))
             & (_rem(cc, CARRY_ROWS) > _rem(rc, seq) + CARRY_ROWS - w)).astype(BF16)
    s = _dot_parts(band, u_parts) + _dot_parts(cband, carry_parts)
    cnt = jnp.minimum(pos0 + 1, w).astype(F32)
    return s / cnt - u


def _mix_merge(o, silu_og, sig_ga, sig_gb, pooled, gn, ps):
    y_a = _rms(o) * gn * silu_og.astype(F32)
    return sig_ga.astype(F32) * y_a + sig_gb.astype(F32) * (pooled * ps)


def _p_columns(dqk, d_model, d_pool):
    q0, k0, v0 = 0, dqk, 2 * dqk
    og0 = v0 + d_model
    u0 = og0 + d_model
    ga0 = u0 + d_pool
    gb0 = ga0 + d_model
    return q0, k0, v0, og0, u0, ga0, gb0


def _mix_prompt_kernel(p_ref, *refs, dqk, d_model):
    lg_refs = refs[:LOG_DECAY_TERMS]
    gn_ref, wp_ref, ps_ref, m_ref, s_ref, pool_out_ref, car_ref = refs[LOG_DECAY_TERMS:]
    c = pl.program_id(1)
    t = p_ref.shape[0]
    groups, pin, dv = wp_ref.shape
    dk = dqk // N_HEADS
    q0, k0, v0, og0, u0, ga0, gb0 = _p_columns(dqk, d_model, groups * pin)

    @pl.when(c == 0)
    def _():
        s_ref[...] = jnp.zeros_like(s_ref)
        car_ref[...] = jnp.zeros_like(car_ref)

    consts = _gla_consts(t, t)
    loga = tuple(ref[...] for ref in lg_refs)
    e0_all = _dot_parts(consts["tri_local"], loga)
    g_all = jnp.exp(_dot_parts(consts["sel"], loga))
    tot = consts["total_kind"] * consts["nblk"]
    decay_all = _column_tile(g_all[tot:tot + 1, :])
    pos = c * t + _iota2((t, 1), 0)
    gn = gn_ref[...]

    heads = range(N_HEADS)
    deltas = []
    for h in heads:
        ub = p_ref[:, u0 + h * pin:u0 + (h + 1) * pin]
        deltas.append(_pool_delta(ub.astype(F32), (ub,), (car_ref[:, h * pin:(h + 1) * pin],), POOL_WINDOWS[h],
                                  pos, t).astype(BF16))
    ops = []
    for h in heads:
        q = p_ref[:, q0 + h * dk:q0 + (h + 1) * dk].astype(F32)
        k = p_ref[:, k0 + h * dk:k0 + (h + 1) * dk].astype(F32)
        ops.append(_gla_operands(q, k, e0_all[:, h * dk:(h + 1) * dk], g_all[:, h * dk:(h + 1) * dk], consts))
    pooled = [_dot(deltas[h], wp_ref[h]) for h in heads]
    atts = [_gla_scores(ops[h][0], consts).astype(BF16) for h in heads]
    outs = []
    for h in heads:
        _, qe, kd = ops[h]
        vb = p_ref[:, v0 + h * dv:v0 + (h + 1) * dv]
        s = s_ref[h]
        outs.append(_dot(atts[h], vb) + _dot(qe, s.astype(BF16)))
        decay = decay_all[h * dk:(h + 1) * dk]
        s_ref[h] = _lane_tile(decay, dv) * s + _dot_tn(kd, vb)

    for h in heads:
        merged = _mix_merge(outs[h], p_ref[:, og0 + h * dv:og0 + (h + 1) * dv],
                            p_ref[:, ga0 + h * dv:ga0 + (h + 1) * dv], p_ref[:, gb0 + h * dv:gb0 + (h + 1) * dv],
                            pooled[h], gn, ps_ref[:, h * dv:(h + 1) * dv])
        m_ref[:, h * dv:(h + 1) * dv] = merged.astype(BF16)

    u_tail = p_ref[t - CARRY_ROWS:, u0:u0 + groups * pin]
    car_ref[...] = u_tail

    @pl.when(c == pl.num_programs(1) - 1)
    def _():
        pool_out_ref[...] = u_tail.astype(F32)[CARRY_ROWS - POOL_BUF:]


def _mix_prompt_call(p, lg, batch, seq_len, layer, gla_norm_g, w_pool, pool_scale, chunk=MIX_CHUNK):
    dqk = lg[0].shape[1]
    dk = dqk // N_HEADS
    groups, pin, dv = w_pool.shape[1:]
    d_model = pool_scale.shape[2]
    n_p = p.shape[1]
    nc = seq_len // chunk
    const = lambda b, c: (0, 0)
    in_specs = [
        pl.BlockSpec((chunk, n_p), lambda b, c: (b * nc + c, 0)),
        *[pl.BlockSpec((chunk, dqk), lambda b, c: (b * nc + c, 0)) for _ in lg],
        _layer_spec((1, dv), layer, const),
        _layer_spec((groups, pin, dv), layer, lambda b, c: (0, 0, 0)),
        _layer_spec((1, d_model), layer, const),
    ]
    out_specs = [
        pl.BlockSpec((chunk, d_model), lambda b, c: (b * nc + c, 0)),
        pl.BlockSpec((None, N_HEADS, dk, dv), lambda b, c: (b, 0, 0, 0)),
        pl.BlockSpec((None, POOL_BUF, groups * pin), lambda b, c: (b, 0, 0)),
    ]
    out_shape = [
        jax.ShapeDtypeStruct((batch * seq_len, d_model), BF16),
        jax.ShapeDtypeStruct((batch, N_HEADS, dk, dv), F32),
        jax.ShapeDtypeStruct((batch, POOL_BUF, groups * pin), F32),
    ]
    return pl.pallas_call(
        functools.partial(_mix_prompt_kernel, dqk=dqk, d_model=d_model),
        grid=(batch, nc),
        in_specs=in_specs,
        out_specs=out_specs,
        out_shape=out_shape,
        scratch_shapes=[pltpu.VMEM((CARRY_ROWS, groups * pin), BF16)],
        compiler_params=_cparams("arbitrary", "arbitrary"),
        name="mix_prompt",
    )(p, *lg, gla_norm_g, w_pool, pool_scale)


def _mix_sample_kernel(p_ref, *refs, seq, dqk, d_model, n_alias, layer, n_steps):
    lg_refs = refs[:LOG_DECAY_TERMS]
    s0_hbm, buf_ref, gn_ref, wp_ref, ps_ref = refs[LOG_DECAY_TERMS:LOG_DECAY_TERMS + 5]
    m_ref, s_out_ref, pool_out_ref, car_ref, s0_buf, s0_sem = refs[LOG_DECAY_TERMS + 5 + n_alias:]
    nseq = s0_buf.shape[1]
    t = nseq * seq

    def state_copy(step):
        slot = step % STATE_BUFFERS
        return pltpu.make_async_copy(s0_hbm.at[layer, pl.ds(step * nseq, nseq)], s0_buf.at[slot], s0_sem.at[slot])

    i = pl.program_id(0)
    lookahead = STATE_BUFFERS - 1

    @pl.when(i == 0)
    def _():
        for step in range(min(lookahead, n_steps)):
            state_copy(step).start()

    @pl.when(i + lookahead < n_steps)
    def _():
        state_copy(i + lookahead).start()

    state_copy(i).wait()
    s0_ref = s0_buf.at[i % STATE_BUFFERS]
    groups, pin, dv = wp_ref.shape
    dk = dqk // N_HEADS
    q0, k0, v0, og0, u0, ga0, gb0 = _p_columns(dqk, d_model, groups * pin)

    car_ref[...] = jnp.zeros_like(car_ref)
    for j in range(nseq):
        car_ref[j * CARRY_ROWS + 1:(j + 1) * CARRY_ROWS, :] = buf_ref[j]

    consts = _gla_consts(t, seq)
    loga = tuple(ref[...] for ref in lg_refs)
    e0_all = _dot_parts(consts["tri_local"], loga)
    g_all = jnp.exp(_dot_parts(consts["sel"], loga))
    tot = consts["total_kind"] * consts["nblk"]
    decay_all = [_column_tile(g_all[tot + j:tot + j + 1, :]) for j in range(nseq)]
    pos = PAST_LEN + _rem(_iota2((t, 1), 0), seq)
    keep = POOL_BUF - seq
    gn = gn_ref[...]

    u_all = p_ref[:, u0:u0 + groups * pin].astype(F32)
    heads = range(N_HEADS)
    ops = []
    for h in heads:
        q = p_ref[:, q0 + h * dk:q0 + (h + 1) * dk].astype(F32)
        k = p_ref[:, k0 + h * dk:k0 + (h + 1) * dk].astype(F32)
        ops.append(_gla_operands(q, k, e0_all[:, h * dk:(h + 1) * dk], g_all[:, h * dk:(h + 1) * dk], consts))
    atts = [_gla_scores(ops[h][0], consts).astype(BF16) for h in heads]
    outs = []
    for h in heads:
        _, qeb, kdb = ops[h]
        vb = p_ref[:, v0 + h * dv:v0 + (h + 1) * dv]
        o_intra = _dot(atts[h], vb)
        o_parts = []
        for j in range(nseq):
            s0 = s0_ref[j, h]
            lo, hi = j * seq, (j + 1) * seq
            o_parts.append(o_intra[lo:hi] + _dot(qeb[lo:hi], s0.astype(BF16)))
            decay = decay_all[j][h * dk:(h + 1) * dk]
            s_out_ref[j, h] = _lane_tile(decay, dv) * s0 + _dot_tn(kdb[lo:hi], vb[lo:hi])
        outs.append(_cat(o_parts, 0))

    for h in heads:
        d = _pool_delta(u_all[:, h * pin:(h + 1) * pin], (p_ref[:, u0 + h * pin:u0 + (h + 1) * pin],),
                        _split(car_ref[:, h * pin:(h + 1) * pin], CARRY_TERMS), POOL_WINDOWS[h], pos, seq)
        merged = _mix_merge(outs[h], p_ref[:, og0 + h * dv:og0 + (h + 1) * dv],
                            p_ref[:, ga0 + h * dv:ga0 + (h + 1) * dv], p_ref[:, gb0 + h * dv:gb0 + (h + 1) * dv],
                            _dot(d.astype(BF16), wp_ref[h]), gn, ps_ref[:, h * dv:(h + 1) * dv])
        m_ref[:, h * dv:(h + 1) * dv] = merged.astype(BF16)

    for j in range(nseq):
        pool_out_ref[j, 0:keep, :] = buf_ref[j, seq:POOL_BUF, :]
        pool_out_ref[j, keep:POOL_BUF, :] = u_all[j * seq:(j + 1) * seq]


def _mix_sample_call(p, lg, row_off, seq, layer, state_gla, state_pool, gla_norm_g, w_pool,
                     pool_scale, prev_gla, prev_pool, nseq=SAMPLE_SEQS_PER_STEP):
    depth, bs, nh, dk, dv = state_gla.shape
    dqk = lg[0].shape[1]
    groups, pin, _ = w_pool.shape[1:]
    d_model = pool_scale.shape[2]
    n_p = p.shape[1]
    assert seq <= POOL_BUF and seq % 8 == 0
    t = nseq * seq
    off = row_off // t
    const = lambda i: (0, 0)
    in_specs = [
        pl.BlockSpec((t, n_p), lambda i: (off + i, 0)),
        *[pl.BlockSpec((t, dqk), lambda i: (off + i, 0)) for _ in lg],
        pl.BlockSpec(memory_space=pl.ANY),
        pl.BlockSpec((None, nseq, POOL_BUF, groups * pin), lambda i: (layer, i, 0, 0)),
        _layer_spec((1, dv), layer, const),
        _layer_spec((groups, pin, dv), layer, lambda i: (0, 0, 0)),
        _layer_spec((1, d_model), layer, const),
    ]
    args = [p, *lg, state_gla, state_pool, gla_norm_g, w_pool, pool_scale]
    aliases = {}
    n_alias = 0
    if prev_gla is not None:
        in_specs += [pl.BlockSpec(memory_space=pl.ANY), pl.BlockSpec(memory_space=pl.ANY)]
        aliases = {len(args): 1, len(args) + 1: 2}
        args += [prev_gla, prev_pool]
        n_alias = 2
    out_specs = [
        pl.BlockSpec((t, d_model), lambda i: (i, 0)),
        pl.BlockSpec((None, nseq, nh, dk, dv), lambda i: (layer, i, 0, 0, 0)),
        pl.BlockSpec((None, nseq, POOL_BUF, groups * pin), lambda i: (layer, i, 0, 0)),
    ]
    out_shape = [
        jax.ShapeDtypeStruct((bs * seq, d_model), BF16),
        jax.ShapeDtypeStruct(state_gla.shape, F32),
        jax.ShapeDtypeStruct(state_pool.shape, F32),
    ]
    return pl.pallas_call(
        functools.partial(_mix_sample_kernel, seq=seq, dqk=dqk, d_model=d_model, n_alias=n_alias, layer=layer,
                          n_steps=bs // nseq),
        grid=(bs // nseq,),
        in_specs=in_specs,
        out_specs=out_specs,
        out_shape=out_shape,
        scratch_shapes=[pltpu.VMEM((nseq * CARRY_ROWS, groups * pin), F32),
                        pltpu.VMEM((STATE_BUFFERS, nseq, nh, dk, dv), F32),
                        pltpu.SemaphoreType.DMA((STATE_BUFFERS,))],
        input_output_aliases=aliases,
        compiler_params=_cparams("arbitrary"),
        name="mix_sample",
    )(*args)


def _mm_o_kernel(ap_ref, as_ref, xp_ref, xs_ref, g1p_ref, g1s_ref, scp_ref, scs_ref, shp_ref, shs_ref,
                 w_ref, ng_ref, x1p_ref, x1s_ref, h_ref, wb_ref, *, rg):
    @pl.when(pl.program_id(0) == 0)
    def _():
        wb_ref[...] = w_ref[...].astype(BF16)

    def body(a_ref, x_ref, g1_ref, sc_ref, sh_ref, x1_ref):
        rows = rg.rows
        x1 = _rows2d(x_ref) + _per_row(g1_ref, rows) * _dot(a_ref[...], wb_ref[...])
        x1_ref[...] = x1.reshape(x1_ref.shape)
        h_ref[...] = _modulated_norm(x1, ng_ref[...], _per_row(sc_ref, rows), _per_row(sh_ref, rows))

    rg.run(body, (ap_ref, xp_ref, g1p_ref, scp_ref, shp_ref, x1p_ref),
           (as_ref, xs_ref, g1s_ref, scs_ref, shs_ref, x1s_ref))


def _mm_o_call(merged_p, merged_s, xp, xs, w_o, mod_p, mod_s, layer, norm2_g, rows=EPILOGUE_ROWS):
    rg = _RowGrid(xp.shape, xs.shape, rows)
    d = rg.d
    m_total = xp.shape[0] * xp.shape[1] + xs.shape[0] * xs.shape[1]
    xsp = rg.x_specs()
    return pl.pallas_call(
        functools.partial(_mm_o_kernel, rg=rg),
        grid=rg.grid,
        in_specs=[*rg.rows_specs(d), *xsp, *rg.mod_specs(layer, 2), *rg.mod_specs(layer, 4),
                  *rg.mod_specs(layer, 3), _resident_spec((d, d), layer),
                  _layer_spec((1, d), layer, lambda i: (0, 0))],
        out_specs=[*xsp, rg.joint_spec(d)],
        out_shape=[jax.ShapeDtypeStruct(xp.shape, F32), jax.ShapeDtypeStruct(xs.shape, F32),
                   jax.ShapeDtypeStruct((m_total, d), BF16)],
        scratch_shapes=[pltpu.VMEM((d, d), BF16)],
        compiler_params=_cparams("arbitrary"),
        name="out_proj",
    )(merged_p, merged_s, xp, xs, mod_p, mod_s, mod_p, mod_s, mod_p, mod_s, w_o, norm2_g)


def _mm_gu_kernel(a_ref, wg_ref, wu_ref, wd_ref, o_ref, wdb_ref, wgb_ref, wub_ref, *, n_slabs):
    @pl.when(pl.program_id(1) == 0)
    def _():
        wgb_ref[...] = wg_ref[...].astype(BF16)
        wub_ref[...] = wu_ref[...].astype(BF16)

    @pl.when(pl.program_id(0) * pl.num_programs(1) + pl.program_id(1) < n_slabs)
    def _():
        wdb_ref[...] = wd_ref[...].astype(BF16)

    a = a_ref[...]
    o_ref[...] = (_silu(_dot(a, wgb_ref[...])) * _dot(a, wub_ref[...])).astype(BF16)


def _mm_gu_call(h, w_gu, w_down, layer):
    m, k = h.shape
    tm = _largest_row_tile(m, ROW_TILE_CAP)
    dff = w_gu.shape[2] // 2
    d = w_down.shape[2]
    tn = _largest_tile(dff, GATE_UP_COLS_CAP)
    nj, ni = dff // tn, m // tm
    slab = next(r for r in range(16, dff + 1, 16) if dff % r == 0 and dff // r <= nj * ni)
    n_slabs = dff // slab
    slab_index = lambda j, i: (jnp.minimum(j * ni + i, n_slabs - 1), 0)
    return pl.pallas_call(
        functools.partial(_mm_gu_kernel, n_slabs=n_slabs),
        grid=(nj, ni),
        in_specs=[
            pl.BlockSpec((tm, k), lambda j, i: (i, 0)),
            _layer_spec((k, tn), layer, lambda j, i: (0, j)),
            _layer_spec((k, tn), layer, lambda j, i: (0, nj + j)),
            _layer_spec((slab, d), layer, slab_index),
        ],
        out_specs=[pl.BlockSpec((tm, tn), lambda j, i: (i, j)), pl.BlockSpec((slab, d), slab_index)],
        out_shape=[jax.ShapeDtypeStruct((m, dff), BF16), jax.ShapeDtypeStruct((dff, d), BF16)],
        scratch_shapes=[pltpu.VMEM((k, tn), BF16), pltpu.VMEM((k, tn), BF16)],
        compiler_params=_cparams("arbitrary", "arbitrary"),
        name="gate_up",
    )(h, w_gu, w_gu, w_down)


def _mm_down_kernel(a_ref, xp_ref, xs_ref, g2p_ref, g2s_ref, w_ref, ng_ref, *rest, rg, final):
    if final:
        yp_ref, ys_ref = rest

        def body(x_ref, g2_ref, y_ref):
            x2 = _rows2d(x_ref) + _per_row(g2_ref, rg.rows) * _dot(a_ref[...], w_ref[...])
            y_ref[...] = (_rms(x2) * ng_ref[...]).reshape(y_ref.shape)

        rg.run(body, (xp_ref, g2p_ref, yp_ref), (xs_ref, g2s_ref, ys_ref))
    else:
        scp_ref, scs_ref, shp_ref, shs_ref, wl_ref, wa2_ref, ba_ref, x2p_ref, x2s_ref, h_ref, *lg_refs = rest

        def body(x_ref, g2_ref, sc_ref, sh_ref, x2_ref):
            rows = rg.rows
            x2 = _rows2d(x_ref) + _per_row(g2_ref, rows) * _dot(a_ref[...], w_ref[...])
            x2_ref[...] = x2.reshape(x2_ref.shape)
            h = _modulated_norm(x2, ng_ref[...], _per_row(sc_ref, rows), _per_row(sh_ref, rows))
            h_ref[...] = h
            for ref, term in zip(lg_refs, _log_decay_terms(h, wl_ref, wa2_ref, ba_ref)):
                ref[...] = term

        rg.run(body, (xp_ref, g2p_ref, scp_ref, shp_ref, x2p_ref), (xs_ref, g2s_ref, scs_ref, shs_ref, x2s_ref))


def _mm_down_call(act, xp, xs, w_down, mod_p, mod_s, layer, norm_g, norm_layer, final, w_alr_t, w_a2, b_a,
                  rows=EPILOGUE_ROWS):
    rg = _RowGrid(xp.shape, xs.shape, rows)
    d = rg.d
    dff = w_down.shape[0]
    m_total = act.shape[0]
    xsp = rg.x_specs()
    in_specs = [rg.joint_spec(dff), *xsp, *rg.mod_specs(layer, 5), _resident_spec((dff, d)),
                _layer_spec((1, d), norm_layer, lambda i: (0, 0))]
    args = [act, xp, xs, mod_p, mod_s, w_down, norm_g]
    x_shapes = [jax.ShapeDtypeStruct(xp.shape, F32), jax.ShapeDtypeStruct(xs.shape, F32)]
    if final:
        out_specs, out_shape = [*xsp], x_shapes
    else:
        dqk = w_a2.shape[2]
        gate_in, gate_out = _gate_specs(rg, layer + 1, d, dqk)
        in_specs += [*rg.mod_specs(layer + 1, 1), *rg.mod_specs(layer + 1, 0), *gate_in]
        args += [mod_p, mod_s, mod_p, mod_s, w_alr_t, w_a2, b_a]
        out_specs = [*xsp, rg.joint_spec(d), *gate_out]
        out_shape = x_shapes + [jax.ShapeDtypeStruct((m_total, d), BF16)] \
            + [jax.ShapeDtypeStruct((m_total, dqk), BF16)] * LOG_DECAY_TERMS
    return pl.pallas_call(
        functools.partial(_mm_down_kernel, rg=rg, final=final),
        grid=rg.grid,
        in_specs=in_specs,
        out_specs=out_specs,
        out_shape=out_shape,
        compiler_params=_cparams("arbitrary"),
        name="down_proj",
    )(*args)


def kernel(x_prompt, x_sample, state_gla, state_pool, c_prompt, c_sample, w_ada, b_ada, norm1_g, w_in, w_a2, b_a,
           gla_norm_g, w_pool, pool_scale, w_o, norm2_g, w_gu, w_down, final_norm_g):
    depth, d, n6 = w_ada.shape
    bp, lp, _ = x_prompt.shape
    bs, ls, _ = x_sample.shape
    mp = bp * lp
    rank, dqk = w_a2.shape[1:]
    n_main = w_in.shape[2] - rank

    c_all = jnp.concatenate([c_sample, c_prompt], axis=0)
    c_all = jnp.pad(c_all, ((0, (-c_all.shape[0]) % 8), (0, 0)))
    mod_s = _ada_call(c_all, w_ada, b_ada)
    mod_p = mod_s[:, bs:bs + bp].reshape(depth, bp, 1, n6)

    w_in_t = jnp.swapaxes(w_in, 1, 2)
    w_alr_b = jnp.pad(w_in_t[:, n_main:, :], ((0, 0), (0, LANE - rank), (0, 0))).astype(BF16)
    w_a2_b = jnp.pad(w_a2, ((0, 0), (0, LANE - rank), (0, 0))).astype(BF16)
    w_pool_b = w_pool.astype(BF16)
    n1 = norm1_g.reshape(depth, 1, d)
    n2 = norm2_g.reshape(depth, 1, d)
    nf = final_norm_g.reshape(1, 1, d)
    ba3 = b_a.reshape(depth, 1, dqk)
    gn3 = gla_norm_g.reshape(depth, 1, -1)
    ps3 = pool_scale.reshape(depth, 1, d)

    xp, xs = x_prompt, x_sample
    h, *lg = _pre_call(xp, xs, mod_p, mod_s, 0, n1, w_alr_b, w_a2_b, ba3)

    gla_p, pool_p = [], []
    gla_s = pool_s = None
    for l in range(depth):
        p = _mm_in_call(h, w_in_t, l, dqk, d, w_pool.shape[1] * w_pool.shape[2])
        merged_p, s_p, b_p = _mix_prompt_call(p, lg, bp, lp, l, gn3, w_pool_b, ps3)
        merged_s, gla_s, pool_s = _mix_sample_call(p, lg, mp, ls, l, state_gla, state_pool, gn3, w_pool_b, ps3,
                                                   gla_s, pool_s)
        gla_p.append(s_p)
        pool_p.append(b_p)

        xp, xs, h2 = _mm_o_call(merged_p, merged_s, xp, xs, w_o, mod_p, mod_s, l, n2)
        act, w_down_b = _mm_gu_call(h2, w_gu, w_down, l)
        if l + 1 < depth:
            xp, xs, h, *lg = _mm_down_call(act, xp, xs, w_down_b, mod_p, mod_s, l, n1, l + 1, False, w_alr_b,
                                           w_a2_b, ba3)
        else:
            yp, ys = _mm_down_call(act, xp, xs, w_down_b, mod_p, mod_s, l, nf, 0, True, None, None, None)
    return yp, ys, jnp.stack(gla_p), jnp.stack(pool_p), gla_s, pool_s
```
